```python
import jax, jax.numpy as jnp
from jax import lax
import numpy as np


D_MODEL = 1024
BATCH = 2
SEQ = 8192
DEPTH = 1

GRID_W = 64
CTX_LEN = 256
CONV_W = 1024
RET_HEADS = 4
RET_DK = 256
RET_DV = 512
RET_QK = RET_HEADS * RET_DK
RET_V = RET_HEADS * RET_DV
RET_CHUNK = 128
ROPE_PAIRS = RET_DK // 4
ROPE_BASE = 10000.0
D_FF = 2816
N_MOD = 6
EPS = 1e-6

OFF_CX = 0
OFF_CB = OFF_CX + CONV_W
OFF_CC = OFF_CB + CONV_W
OFF_Q = OFF_CC + CONV_W
OFF_K = OFF_Q + RET_QK
OFF_V = OFF_K + RET_QK
OFF_G = OFF_V + RET_V
OFF_GA = OFF_G + RET_V
OFF_GB = OFF_GA + D_MODEL
IN_COLS = OFF_GB + D_MODEL

kernel_name = 'hybrid_conv_retention_diffusion_block'


def rms_norm(x, gain):
    xf = x.astype(jnp.float32)
    y = xf * lax.rsqrt(jnp.mean(xf * xf, axis=-1, keepdims=True) + EPS)
    return y.astype(x.dtype) * gain


def dwconv3(x, w, b, axis):
    n = x.shape[axis]
    pad = [(0, 0)] * x.ndim
    pad[axis] = (1, 1)
    xp = jnp.pad(x, pad)
    return (w[0] * lax.slice_in_dim(xp, 0, n, axis=axis)
            + w[1] * lax.slice_in_dim(xp, 1, n + 1, axis=axis)
            + w[2] * lax.slice_in_dim(xp, 2, n + 2, axis=axis) + b)


def grid_conv(x, w, b, axis):
    bsz, n, ch = x.shape
    rows = n // GRID_W
    xg = x.reshape(bsz, rows, GRID_W, ch)
    return dwconv3(xg, w, b, axis).reshape(bsz, n, ch)


def conv_along_rows(x, w, b):
    return grid_conv(x, w, b, 2)


def conv_along_cols(x, w, b):
    return grid_conv(x, w, b, 1)


def conv_along_seq(x, w, b):
    return dwconv3(x, w, b, 1)


def rotary_tables(n):
    t = jnp.arange(n)
    row = (t // GRID_W).astype(jnp.float32)
    col = (t % GRID_W).astype(jnp.float32)
    inv = ROPE_BASE ** (-jnp.arange(ROPE_PAIRS, dtype=jnp.float32) / ROPE_PAIRS)
    ang = jnp.concatenate([row[:, None] * inv, col[:, None] * inv], axis=-1)
    return jnp.cos(ang), jnp.sin(ang)


def apply_rotary(t, cos, sin):
    half = RET_DK // 2
    t1, t2 = t[..., :half], t[..., half:]
    return jnp.concatenate([t1 * cos - t2 * sin, t2 * cos + t1 * sin], axis=-1)


def to_heads(t, dh):
    bsz, n, _ = t.shape
    return t.reshape(bsz, n, RET_HEADS, dh).transpose(0, 2, 1, 3).astype(jnp.float32)


def flip_seq(t):
    return jnp.flip(t, axis=2)


def retention_scan(q, k, v, log_gamma, s0, strict):
    bsz, nh, n, dk = q.shape
    nc = n // RET_CHUNK
    idx = jnp.arange(RET_CHUNK, dtype=jnp.float32)
    diff = idx[:, None] - idx[None, :]
    mask = diff > 0 if strict else diff >= 0
    dmat = jnp.where(mask, jnp.exp(jnp.where(mask, diff, 0.0)[None] * log_gamma[:, None, None]), 0.0)
    xi = jnp.exp((idx + 1.0)[None, :] * log_gamma[:, None])
    zeta = jnp.exp((RET_CHUNK - 1.0 - idx)[None, :] * log_gamma[:, None])
    chunk_decay = jnp.exp(RET_CHUNK * log_gamma)

    def to_chunks(t):
        return jnp.moveaxis(t.reshape(bsz, nh, nc, RET_CHUNK, t.shape[-1]), 2, 0)

    def step(s, qkv):
        qc, kc, vc = qkv
        scores = jnp.einsum('bhnd,bhmd->bhnm', qc, kc) * dmat
        o = (jnp.einsum('bhnm,bhmv->bhnv', scores, vc)
             + jnp.einsum('bhnd,bhdv->bhnv', qc * xi[..., None], s))
        s = chunk_decay[:, None, None] * s + jnp.einsum('bhmd,bhmv->bhdv', kc * zeta[..., None], vc)
        return s, o

    s_final, o = lax.scan(step, s0, (to_chunks(q), to_chunks(k), to_chunks(v)))
    o = jnp.moveaxis(o, 0, 2).reshape(bsz, nh, n, v.shape[-1])
    return o, s_final


def context_states(kc, vc, lg_fw, lg_bw):
    lc = kc.shape[2]
    m = jnp.arange(lc, dtype=jnp.float32)
    w_fw = jnp.exp((lc - 1.0 - m)[None, :] * lg_fw[:, None])
    w_bw = jnp.exp(m[None, :] * lg_bw[:, None])
    s_fw = jnp.einsum('bhmd,hm,bhmv->bhdv', kc, w_fw, vc)
    s_bw = jnp.einsum('bhmd,hm,bhmv->bhdv', kc, w_bw, vc)
    return s_fw, s_bw


def mixer_branches(p, q, k, v, s_fw, s_bw, lg_fw, lg_bw, conv_w, conv_b, w_conv_out, w_ret_out, w_o, conv_fn):
    y_conv = (p[..., OFF_CB:OFF_CC]
              * conv_fn(p[..., OFF_CC:OFF_Q] * p[..., OFF_CX:OFF_CB], conv_w, conv_b)) @ w_conv_out
    o_fw, _ = retention_scan(q, k, v, lg_fw, s_fw, False)
    o_bw, _ = retention_scan(flip_seq(q), flip_seq(k), flip_seq(v), lg_bw, s_bw, True)
    o = o_fw + flip_seq(o_bw)
    o = o * lax.rsqrt(jnp.mean(o * o, axis=-1, keepdims=True) + EPS)
    bsz, _, n, _ = o.shape
    o = o.transpose(0, 2, 1, 3).reshape(bsz, n, RET_V).astype(p.dtype)
    y_ret = (jax.nn.silu(p[..., OFF_G:OFF_GA]) * o) @ w_ret_out
    merged = (jax.nn.sigmoid(p[..., OFF_GA:OFF_GB]) * y_conv
              + jax.nn.sigmoid(p[..., OFF_GB:IN_COLS]) * y_ret)
    return merged @ w_o


def conv_ffn(h, w_up, conv_w, conv_b, w_down, conv_fn):
    u = conv_fn(h @ w_up, conv_w, conv_b)
    return (jax.nn.silu(u[..., :D_FF]) * u[..., D_FF:]) @ w_down


def setup_inputs(seed: int = 0) -> dict:
    key = jax.random.key(seed)
    ks = jax.random.split(key, 22)

    def nrm(k, shape, scale):
        return jax.random.normal(k, shape, jnp.float32) * scale

    L = DEPTH
    D = D_MODEL
    theta0 = jnp.log(2.0 ** (5.0 + jnp.arange(RET_HEADS, dtype=jnp.float32)) - 1.0)
    return {
        'x': nrm(ks[0], (BATCH, SEQ, D), 1.0),
        'c': nrm(ks[1], (BATCH, D), 1.0),
        'ctx': nrm(ks[2], (BATCH, CTX_LEN, D), 1.0),
        'c_ctx': nrm(ks[3], (D,), 1.0),
        'w_ada': nrm(ks[4], (L, D, N_MOD * D), D ** -0.5),
        'b_ada': nrm(ks[5], (L, N_MOD * D), 0.01),
        'norm_mix_pre': 1.0 + nrm(ks[6], (L, D), 0.02),
        'norm_mix_post': 1.0 + nrm(ks[7], (L, D), 0.02),
        'w_in': nrm(ks[8], (L, D, IN_COLS), D ** -0.5),
        'conv_w': nrm(ks[9], (L, 3, CONV_W), 3 ** -0.5),
        'conv_b': nrm(ks[10], (L, CONV_W), 0.01),
        'w_conv_out': nrm(ks[11], (L, CONV_W, D), CONV_W ** -0.5),
        'ret_decay_fw': theta0 + nrm(ks[12], (L, RET_HEADS), 0.05),
        'ret_decay_bw': theta0 + nrm(ks[13], (L, RET_HEADS), 0.05),
        'w_ret_out': nrm(ks[14], (L, RET_V, D), RET_V ** -0.5),
        'w_o': nrm(ks[15], (L, D, D), D ** -0.5),
        'norm_ffn_pre': 1.0 + nrm(ks[16], (L, D), 0.02),
        'norm_ffn_post': 1.0 + nrm(ks[17], (L, D), 0.02),
        'w_ffn_up': nrm(ks[18], (L, D, 2 * D_FF), D ** -0.5),
        'ffn_conv_w': nrm(ks[19], (L, 3, 2 * D_FF), 3 ** -0.5),
        'ffn_conv_b': nrm(ks[20], (L, 2 * D_FF), 0.01),
        'w_ffn_down': nrm(ks[21], (L, D_FF, D), D_FF ** -0.5),
    }


def reference(x, c, ctx, c_ctx, w_ada, b_ada, norm_mix_pre, norm_mix_post, w_in, conv_w, conv_b,
              w_conv_out, ret_decay_fw, ret_decay_bw, w_ret_out, w_o, norm_ffn_pre, norm_ffn_post,
              w_ffn_up, ffn_conv_w, ffn_conv_b, w_ffn_down):
    bsz, n, _ = x.shape
    rows = n // GRID_W
    cos, sin = rotary_tables(rows * GRID_W)
    xc = ctx
    for layer in range(DEPTH):
        update_ctx = layer + 1 < DEPTH
        mod = (jax.nn.silu(c) @ w_ada[layer] + b_ada[layer]).reshape(bsz, N_MOD, 1, D_MODEL)
        mod_c = (jax.nn.silu(c_ctx) @ w_ada[layer] + b_ada[layer]).reshape(N_MOD, D_MODEL)
        lg_fw = jax.nn.log_sigmoid(ret_decay_fw[layer].astype(jnp.float32))
        lg_bw = jax.nn.log_sigmoid(ret_decay_bw[layer].astype(jnp.float32))

        h = rms_norm(x, norm_mix_pre[layer]) * (1.0 + mod[:, 1]) + mod[:, 0]
        hc = rms_norm(xc, norm_mix_pre[layer]) * (1.0 + mod_c[1]) + mod_c[0]
        if update_ctx:
            pc = hc @ w_in[layer]
            pc_kv = pc[..., OFF_K:OFF_G]
        else:
            pc_kv = hc @ w_in[layer][:, OFF_K:OFF_G]
        kc = to_heads(pc_kv[..., :RET_QK], RET_DK) * RET_DK ** -0.5
        vc = to_heads(pc_kv[..., RET_QK:], RET_DV)
        s_fw, s_bw = context_states(kc, vc, lg_fw, lg_bw)

        p = h @ w_in[layer]
        q = apply_rotary(to_heads(p[..., OFF_Q:OFF_K], RET_DK), cos, sin)
        k = apply_rotary(to_heads(p[..., OFF_K:OFF_V], RET_DK), cos, sin) * RET_DK ** -0.5
        v = to_heads(p[..., OFF_V:OFF_G], RET_DV)
        y = mixer_branches(p, q, k, v, s_fw, s_bw, lg_fw, lg_bw, conv_w[layer], conv_b[layer],
                           w_conv_out[layer], w_ret_out[layer], w_o[layer], conv_along_rows)
        if update_ctx:
            zero_state = jnp.zeros_like(s_fw)
            qc = to_heads(pc[..., OFF_Q:OFF_K], RET_DK)
            yc = mixer_branches(pc, qc, kc, vc, zero_state, zero_state, lg_fw, lg_bw, conv_w[layer],
                                conv_b[layer], w_conv_out[layer], w_ret_out[layer], w_o[layer],
                                conv_along_seq)
            xc = xc + mod_c[2] * rms_norm(yc, norm_mix_post[layer])
        x = x + mod[:, 2] * rms_norm(y, norm_mix_post[layer])

        h = rms_norm(x, norm_ffn_pre[layer]) * (1.0 + mod[:, 4]) + mod[:, 3]
        f = conv_ffn(h, w_ffn_up[layer], ffn_conv_w[layer], ffn_conv_b[layer], w_ffn_down[layer],
                     conv_along_cols)
        x = x + mod[:, 5] * rms_norm(f, norm_ffn_post[layer])
        if update_ctx:
            hc = rms_norm(xc, norm_ffn_pre[layer]) * (1.0 + mod_c[4]) + mod_c[3]
            fc = conv_ffn(hc, w_ffn_up[layer], ffn_conv_w[layer], ffn_conv_b[layer],
                          w_ffn_down[layer], conv_along_seq)
            xc = xc + mod_c[5] * rms_norm(fc, norm_ffn_post[layer])
    return x
```

```python
import functools

import jax
import jax.numpy as jnp
from jax import lax
from jax.experimental import pallas as pl
from jax.experimental.pallas import tpu as pltpu

F32 = jnp.float32
BF16 = jnp.bfloat16

D_MODEL = 1024
GRID_W = 64
CONV_W = 1024
RET_HEADS = 4
RET_DK = 256
RET_DV = 512
RET_QK = RET_HEADS * RET_DK
RET_V = RET_HEADS * RET_DV
ROPE_PAIRS = RET_DK // 4
ROPE_BASE = 10000.0
D_FF = 2816
N_MOD = 6
EPS = 1e-6

OFF_CX = 0
OFF_CB = OFF_CX + CONV_W
OFF_CC = OFF_CB + CONV_W
OFF_Q = OFF_CC + CONV_W
OFF_K = OFF_Q + RET_QK
OFF_V = OFF_K + RET_QK
OFF_G = OFF_V + RET_V
OFF_GA = OFF_G + RET_V
OFF_GB = OFF_GA + D_MODEL
IN_COLS = OFF_GB + D_MODEL

RET_C = 256
TM_IN = 256
TS_RET = 512
TM_FFN = 1024
FFN_BLOCKS = ((0, 512), (512, 512), (1024, 512), (1536, 512), (2048, 512), (2560, 256))
ADA_BN = 1536
MOD_ROWS = 8
VMEM_LIMIT = 60 * 1024 * 1024


def _const_spec(shape):
    zeros = (0,) * len(shape)
    return pl.BlockSpec(shape, lambda *_: zeros, pipeline_mode=pl.Buffered(1))


def _rms(xf):
    return xf * lax.rsqrt(jnp.mean(xf * xf, axis=-1, keepdims=True) + EPS)


def _sigmoid(x):
    return 1.0 / (1.0 + jnp.exp(-x))


def _dot(a, b):
    return jnp.dot(a, b, preferred_element_type=F32)


def _dot_nt(a, b):
    return lax.dot_general(a, b, (((1,), (1,)), ((), ())), preferred_element_type=F32)


def _dot_tn(a, b):
    return lax.dot_general(a, b, (((0,), (0,)), ((), ())), preferred_element_type=F32)


def _ada_kernel(c_ref, w_ref, b_ref, o_ref):
    cv = c_ref[...]
    s = cv * _sigmoid(cv)
    o_ref[...] = _dot(s.astype(BF16), w_ref[...].astype(BF16)) + b_ref[...]


def _ada(cin, w_ada, b_ada):
    ncol = w_ada.shape[1]
    return pl.pallas_call(
        _ada_kernel,
        name="ada",
        grid=(ncol // ADA_BN,),
        in_specs=[
            pl.BlockSpec((MOD_ROWS, D_MODEL), lambda j: (0, 0)),
            pl.BlockSpec((D_MODEL, ADA_BN), lambda j: (0, j)),
            pl.BlockSpec((1, ADA_BN), lambda j: (0, j)),
        ],
        out_specs=pl.BlockSpec((MOD_ROWS, ADA_BN), lambda j: (0, j)),
        out_shape=jax.ShapeDtypeStruct((MOD_ROWS, ncol), F32),
        compiler_params=pltpu.CompilerParams(dimension_semantics=("arbitrary",)),
    )(cin, w_ada, b_ada)


def _tables_kernel(theta_ref, dmat_ref, rowt_ref, cdt_ref, ctxw_ref, *, ctx_len):
    th = theta_ref[...]
    lg = -(jnp.maximum(-th, 0.0) + jnp.log1p(jnp.exp(-jnp.abs(th))))
    n_i = lax.broadcasted_iota(jnp.int32, (RET_C, RET_C), 0)
    m_i = lax.broadcasted_iota(jnp.int32, (RET_C, RET_C), 1)
    diff = (n_i - m_i).astype(F32)
    idx = lax.broadcasted_iota(jnp.int32, (RET_C, RET_DK), 0).astype(F32)
    midx = lax.broadcasted_iota(jnp.int32, (ctx_len, RET_DK), 0).astype(F32)
    for h in range(RET_HEADS):
        lf = lg[h:h + 1, :]
        lb = lg[RET_HEADS + h:RET_HEADS + h + 1, :]
        lf_c, lb_c = lf[:, :RET_C], lb[:, :RET_C]
        dmat_ref[h] = jnp.exp(jnp.where(diff >= 0.0, diff * lf_c, -diff * lb_c))
        lf_k, lb_k = lf[:, :RET_DK], lb[:, :RET_DK]
        rowt_ref[h, 0] = jnp.exp((idx + 1.0) * lf_k)
        rowt_ref[h, 1] = jnp.exp((RET_C - 1.0 - idx) * lf_k)
        rowt_ref[h, 2] = jnp.exp((RET_C - idx) * lb_k)
        rowt_ref[h, 3] = jnp.exp(idx * lb_k)
        cdt_ref[h, 0] = jnp.exp(RET_C * lf)
        cdt_ref[h, 1] = jnp.exp(RET_C * lb)
        ctxw_ref[h, 0] = jnp.exp((ctx_len - 1.0 - midx) * lf_k)
        ctxw_ref[h, 1] = jnp.exp(midx * lb_k)


def _tables(theta, ctx_len):
    return pl.pallas_call(
        functools.partial(_tables_kernel, ctx_len=ctx_len),
        name="tables",
        out_shape=(
            jax.ShapeDtypeStruct((RET_HEADS, RET_C, RET_C), F32),
            jax.ShapeDtypeStruct((RET_HEADS, 4, RET_C, RET_DK), F32),
            jax.ShapeDtypeStruct((RET_HEADS, 2, 1, RET_DV), F32),
            jax.ShapeDtypeStruct((RET_HEADS, 2, ctx_len, RET_DK), F32),
        ),
    )(theta)


def _ctx_kernel(ctx_ref, modc_ref, gain_ref, wkv_ref, ctxw_ref, sfw_ref, sbw_ref):
    mc = modc_ref[...]
    hc = _rms(ctx_ref[0]) * gain_ref[...] * (1.0 + mc[1:2]) + mc[0:1]
    kv = _dot(hc.astype(BF16), wkv_ref[...])
    for h in range(RET_HEADS):
        kh = kv[:, h * RET_DK:(h + 1) * RET_DK] * RET_DK ** -0.5
        vh = kv[:, RET_QK + h * RET_DV:RET_QK + (h + 1) * RET_DV].astype(BF16)
        sfw_ref[0, h] = _dot_tn((kh * ctxw_ref[h, 0]).astype(BF16), vh)
        sbw_ref[0, h] = _dot_tn((kh * ctxw_ref[h, 1]).astype(BF16), vh)


def _ctx_states(ctx, modc, gain, wkv, ctxw):
    bsz, lc, _ = ctx.shape
    state = jax.ShapeDtypeStruct((bsz, RET_HEADS, RET_DK, RET_DV), F32)
    sspec = pl.BlockSpec((1, RET_HEADS, RET_DK, RET_DV), lambda b: (b, 0, 0, 0))
    return pl.pallas_call(
        _ctx_kernel,
        name="ctx",
        grid=(bsz,),
        in_specs=[
            pl.BlockSpec((1, lc, D_MODEL), lambda b: (b, 0, 0)),
            pl.BlockSpec((MOD_ROWS, D_MODEL), lambda b: (0, 0)),
            pl.BlockSpec((1, D_MODEL), lambda b: (0, 0)),
            pl.BlockSpec(wkv.shape, lambda b: (0, 0)),
            pl.BlockSpec(ctxw.shape, lambda b: (0, 0, 0, 0)),
        ],
        out_specs=(sspec, sspec),
        out_shape=(state, state),
        compiler_params=pltpu.CompilerParams(dimension_semantics=("arbitrary",),
                                             vmem_limit_bytes=VMEM_LIMIT),
    )(ctx, modc, gain, wkv, ctxw)


def _inproj_kernel(x_ref, mod_ref, gain_ref, cos_ref, sin_ref, win_ref, wco_ref, cw_ref, cb_ref,
                   yc_ref, q_ref, k_ref, v_ref, sg_ref, sgb_ref):
    m = mod_ref[0]
    h = _rms(x_ref[...]) * gain_ref[...] * (1.0 + m[1:2]) + m[0:1]
    hb = h.astype(BF16)

    def proj(off, width):
        return _dot(hb, win_ref[:, off:off + width])

    u = proj(OFF_CC, CONV_W) * proj(OFF_CX, CONV_W)
    col = lax.broadcasted_iota(jnp.int32, (TM_IN, CONV_W), 0) % GRID_W
    prev = jnp.where(col == 0, 0.0, pltpu.roll(u, 1, 0))
    nxt = jnp.where(col == GRID_W - 1, 0.0, pltpu.roll(u, TM_IN - 1, 0))
    cw = cw_ref[...]
    conv = cw[0:1] * prev + cw[1:2] * u + cw[2:3] * nxt + cb_ref[...]
    z = (proj(OFF_CB, CONV_W) * conv).astype(BF16)
    yc_ref[...] = (_sigmoid(proj(OFF_GA, D_MODEL)) * _dot(z, wco_ref[...])).astype(BF16)
    sgb_ref[...] = _sigmoid(proj(OFF_GB, D_MODEL)).astype(BF16)

    cos = cos_ref[...]
    sin = sin_ref[...]
    half = RET_DK // 2
    for off, ref, scale in ((OFF_Q, q_ref, 1.0), (OFF_K, k_ref, RET_DK ** -0.5)):
        t = proj(off, RET_QK)
        for hh in range(RET_HEADS):
            lo = hh * RET_DK
            t1 = t[:, lo:lo + half]
            t2 = t[:, lo + half:lo + RET_DK]
            ref[:, lo:lo + half] = ((t1 * cos - t2 * sin) * scale).astype(BF16)
            ref[:, lo + half:lo + RET_DK] = ((t2 * cos + t1 * sin) * scale).astype(BF16)
    v_ref[...] = proj(OFF_V, RET_V).astype(BF16)
    g = proj(OFF_G, RET_V)
    sg_ref[...] = (g * _sigmoid(g)).astype(BF16)


def _inproj(x2, modt, gain, cos, sin, win, wco, cw, cb, seq):
    rows = x2.shape[0]
    tiles_per_seq = seq // TM_IN

    def row_spec(width):
        return pl.BlockSpec((TM_IN, width), lambda i: (i, 0))

    def out(width):
        return jax.ShapeDtypeStruct((rows, width), BF16)

    return pl.pallas_call(
        _inproj_kernel,
        name="inproj",
        grid=(rows // TM_IN,),
        in_specs=[
            row_spec(D_MODEL),
            pl.BlockSpec((1, MOD_ROWS, D_MODEL), lambda i: (i // tiles_per_seq, 0, 0)),
            pl.BlockSpec((1, D_MODEL), lambda i: (0, 0)),
            pl.BlockSpec((TM_IN, RET_DK // 2), lambda i: (i % tiles_per_seq, 0)),
            pl.BlockSpec((TM_IN, RET_DK // 2), lambda i: (i % tiles_per_seq, 0)),
            _const_spec(win.shape),
            _const_spec(wco.shape),
            pl.BlockSpec(cw.shape, lambda i: (0, 0)),
            pl.BlockSpec((1, CONV_W), lambda i: (0, 0)),
        ],
        out_specs=(row_spec(D_MODEL), row_spec(RET_QK), row_spec(RET_QK), row_spec(RET_V),
                   row_spec(RET_V), row_spec(D_MODEL)),
        out_shape=(out(D_MODEL), out(RET_QK), out(RET_QK), out(RET_V), out(RET_V), out(D_MODEL)),
        compiler_params=pltpu.CompilerParams(dimension_semantics=("arbitrary",),
                                             vmem_limit_bytes=VMEM_LIMIT),
    )(x2, modt, gain, cos, sin, win, wco, cw, cb)


def _ret_bw_kernel(q_ref, k_ref, v_ref, s0_ref, rowt_ref, cdt_ref, ob_ref, t_ref):
    @pl.when(pl.program_id(1) == 0)
    def _():
        t_ref[...] = s0_ref[0]

    for ci in reversed(range(TS_RET // RET_C)):
        r0 = ci * RET_C
        for h in range(RET_HEADS):
            qh = q_ref[r0:r0 + RET_C, h * RET_DK:(h + 1) * RET_DK].astype(F32)
            kh = k_ref[r0:r0 + RET_C, h * RET_DK:(h + 1) * RET_DK].astype(F32)
            vh = v_ref[r0:r0 + RET_C, h * RET_DV:(h + 1) * RET_DV]
            t = t_ref[h]
            qs = (qh * rowt_ref[h, 2]).astype(BF16)
            ob_ref[r0:r0 + RET_C, h * RET_DV:(h + 1) * RET_DV] = _dot(qs, t.astype(BF16)).astype(BF16)
            ks = (kh * rowt_ref[h, 3]).astype(BF16)
            t_ref[h] = cdt_ref[h, 1] * t + _dot_tn(ks, vh)


def _ret_bw(q, k, v, s_bw, rowt, cdt, bsz, seq):
    ns = seq // TS_RET

    def rev(width):
        return pl.BlockSpec((TS_RET, width), lambda b, j: (b * ns + ns - 1 - j, 0))

    return pl.pallas_call(
        _ret_bw_kernel,
        name="ret_bw",
        grid=(bsz, ns),
        in_specs=[
            rev(RET_QK), rev(RET_QK), rev(RET_V),
            pl.BlockSpec((1, RET_HEADS, RET_DK, RET_DV), lambda b, j: (b, 0, 0, 0)),
            _const_spec(rowt.shape),
            _const_spec(cdt.shape),
        ],
        out_specs=rev(RET_V),
        out_shape=jax.ShapeDtypeStruct((bsz * seq, RET_V), BF16),
        scratch_shapes=[pltpu.VMEM((RET_HEADS, RET_DK, RET_DV), F32)],
        compiler_params=pltpu.CompilerParams(dimension_semantics=("arbitrary", "arbitrary"),
                                             vmem_limit_bytes=VMEM_LIMIT),
    )(q, k, v, s_bw, rowt, cdt)


def _ret_fw_kernel(q_ref, k_ref, v_ref, sg_ref, ob_ref, yc_ref, sgb_ref, x_ref, s0_ref, dmat_ref,
                   rowt_ref, cdt_ref, wro_ref, wo_ref, mod_ref, gpost_ref, gffn_ref,
                   x1_ref, h2_ref, s_ref, gat_ref):
    @pl.when(pl.program_id(1) == 0)
    def _():
        s_ref[...] = s0_ref[0]

    for ci in range(TS_RET // RET_C):
        r0 = ci * RET_C
        for h in range(RET_HEADS):
            qb = q_ref[r0:r0 + RET_C, h * RET_DK:(h + 1) * RET_DK]
            kb = k_ref[r0:r0 + RET_C, h * RET_DK:(h + 1) * RET_DK]
            vh = v_ref[r0:r0 + RET_C, h * RET_DV:(h + 1) * RET_DV]
            scores = (_dot_nt(qb, kb) * dmat_ref[h]).astype(BF16)
            s = s_ref[h]
            qs = (qb.astype(F32) * rowt_ref[h, 0]).astype(BF16)
            o = (_dot(scores, vh) + _dot(qs, s.astype(BF16))
                 + ob_ref[r0:r0 + RET_C, h * RET_DV:(h + 1) * RET_DV].astype(F32))
            ks = (kb.astype(F32) * rowt_ref[h, 1]).astype(BF16)
            s_ref[h] = cdt_ref[h, 0] * s + _dot_tn(ks, vh)
            gate = sg_ref[r0:r0 + RET_C, h * RET_DV:(h + 1) * RET_DV].astype(F32)
            gat_ref[r0:r0 + RET_C, h * RET_DV:(h + 1) * RET_DV] = (gate * _rms(o)).astype(BF16)

    y_ret = _dot(gat_ref[...], wro_ref[...])
    merged = yc_ref[...].astype(F32) + sgb_ref[...].astype(F32) * y_ret
    y = _dot(merged.astype(BF16), wo_ref[...])
    m = mod_ref[0]
    x1 = x_ref[...] + m[2:3] * (_rms(y) * gpost_ref[...])
    x1_ref[...] = x1
    h2_ref[...] = (_rms(x1) * gffn_ref[...] * (1.0 + m[4:5]) + m[3:4]).astype(BF16)


def _ret_fw(q, k, v, sg, ob, yc, sgb, x2, s_fw, dmat, rowt, cdt, wro, wo, modt, gpost, gffn, bsz, seq):
    ns = seq // TS_RET

    def row_spec(width):
        return pl.BlockSpec((TS_RET, width), lambda b, j: (b * ns + j, 0))

    vec = pl.BlockSpec((1, D_MODEL), lambda b, j: (0, 0))
    return pl.pallas_call(
        _ret_fw_kernel,
        name="ret_fw",
        grid=(bsz, ns),
        in_specs=[
            row_spec(RET_QK), row_spec(RET_QK), row_spec(RET_V), row_spec(RET_V), row_spec(RET_V),
            row_spec(D_MODEL), row_spec(D_MODEL), row_spec(D_MODEL),
            pl.BlockSpec((1, RET_HEADS, RET_DK, RET_DV), lambda b, j: (b, 0, 0, 0)),
            _const_spec(dmat.shape), _const_spec(rowt.shape), _const_spec(cdt.shape),
            _const_spec(wro.shape), _const_spec(wo.shape),
            pl.BlockSpec((1, MOD_ROWS, D_MODEL), lambda b, j: (b, 0, 0)),
            vec, vec,
        ],
        out_specs=(row_spec(D_MODEL), row_spec(D_MODEL)),
        out_shape=(jax.ShapeDtypeStruct((bsz * seq, D_MODEL), F32),
                   jax.ShapeDtypeStruct((bsz * seq, D_MODEL), BF16)),
        scratch_shapes=[pltpu.VMEM((RET_HEADS, RET_DK, RET_DV), F32),
                        pltpu.VMEM((TS_RET, RET_V), BF16)],
        compiler_params=pltpu.CompilerParams(dimension_semantics=("arbitrary", "arbitrary"),
                                             vmem_limit_bytes=VMEM_LIMIT),
    )(q, k, v, sg, ob, yc, sgb, x2, s_fw, dmat, rowt, cdt, wro, wo, modt, gpost, gffn)


def _ffn_kernel(hp_ref, hm_ref, hn_ref, x1_ref, mod_ref, gain_ref, wup_ref, cw_ref, cb_ref, wdn_ref,
                out_ref, acc_ref, *, tiles_per_seq):
    t = pl.program_id(0) % tiles_per_seq
    hp = hp_ref[...]
    hn = hn_ref[...]
    hp = jnp.where(t == 0, jnp.zeros_like(hp), hp)
    hn = jnp.where(t == tiles_per_seq - 1, jnp.zeros_like(hn), hn)
    hext = jnp.concatenate([hp, hm_ref[...], hn], axis=0)

    def conv(u, off, width):
        w = cw_ref[:, off:off + width]
        return (w[0:1] * u[0:TM_FFN] + w[1:2] * u[GRID_W:GRID_W + TM_FFN]
                + w[2:3] * u[2 * GRID_W:2 * GRID_W + TM_FFN] + cb_ref[:, off:off + width])

    for c0, width in FFN_BLOCKS:
        g = conv(_dot(hext, wup_ref[:, c0:c0 + width]), c0, width)
        val = conv(_dot(hext, wup_ref[:, D_FF + c0:D_FF + c0 + width]), D_FF + c0, width)
        a = (g * _sigmoid(g) * val).astype(BF16)
        part = _dot(a, wdn_ref[c0:c0 + width, :])
        if c0 == 0:
            acc_ref[...] = part
        else:
            acc_ref[...] += part

    m = mod_ref[0]
    out_ref[...] = x1_ref[...] + m[5:6] * (_rms(acc_ref[...]) * gain_ref[...])


def _ffn(h2, x1, modt, gain, wup, cw, cb, wdn, seq):
    rows = h2.shape[0]
    tiles_per_seq = seq // TM_FFN
    halo_per_tile = TM_FFN // GRID_W
    last_halo = rows // GRID_W - 1
    row_spec = pl.BlockSpec((TM_FFN, D_MODEL), lambda i: (i, 0))
    return pl.pallas_call(
        functools.partial(_ffn_kernel, tiles_per_seq=tiles_per_seq),
        name="ffn",
        grid=(rows // TM_FFN,),
        in_specs=[
            pl.BlockSpec((GRID_W, D_MODEL), lambda i: (jnp.maximum(i * halo_per_tile - 1, 0), 0)),
            row_spec,
            pl.BlockSpec((GRID_W, D_MODEL),
                         lambda i: (jnp.minimum((i + 1) * halo_per_tile, last_halo), 0)),
            row_spec,
            pl.BlockSpec((1, MOD_ROWS, D_MODEL), lambda i: (i // tiles_per_seq, 0, 0)),
            pl.BlockSpec((1, D_MODEL), lambda i: (0, 0)),
            _const_spec(wup.shape),
            pl.BlockSpec(cw.shape, lambda i: (0, 0)),
            pl.BlockSpec((1, 2 * D_FF), lambda i: (0, 0)),
            _const_spec(wdn.shape),
        ],
        out_specs=row_spec,
        out_shape=jax.ShapeDtypeStruct((rows, D_MODEL), F32),
        scratch_shapes=[pltpu.VMEM((TM_FFN, D_MODEL), F32)],
        compiler_params=pltpu.CompilerParams(dimension_semantics=("arbitrary",),
                                             vmem_limit_bytes=VMEM_LIMIT),
    )(h2, h2, h2, x1, modt, gain, wup, cw, cb, wdn)


def _rotary_tables(n):
    t = jnp.arange(n)
    row = (t // GRID_W).astype(F32)
    col = (t % GRID_W).astype(F32)
    inv = ROPE_BASE ** (-jnp.arange(ROPE_PAIRS, dtype=F32) / ROPE_PAIRS)
    ang = jnp.concatenate([row[:, None] * inv, col[:, None] * inv], axis=-1)
    return jnp.cos(ang), jnp.sin(ang)


def kernel(x, c, ctx, c_ctx, w_ada, b_ada, norm_mix_pre, norm_mix_post, w_in, conv_w, conv_b,
           w_conv_out, ret_decay_fw, ret_decay_bw, w_ret_out, w_o, norm_ffn_pre, norm_ffn_post,
           w_ffn_up, ffn_conv_w, ffn_conv_b, w_ffn_down):
    bsz, seq, d = x.shape
    depth = w_ada.shape[0]
    assert d == D_MODEL and depth == 1 and bsz + 1 <= MOD_ROWS
    assert seq % TM_FFN == 0 and seq % TS_RET == 0 and seq % TM_IN == 0 and TM_IN % GRID_W == 0
    lc = ctx.shape[1]
    cos, sin = _rotary_tables(seq)
    x2 = x.reshape(bsz * seq, d)

    cin = jnp.zeros((MOD_ROWS, d), F32).at[:bsz].set(c).at[bsz].set(c_ctx)
    mod_all = _ada(cin, w_ada[0], b_ada).reshape(MOD_ROWS, N_MOD, d)
    pad = ((0, 0), (0, MOD_ROWS - N_MOD), (0, 0))
    modt = jnp.pad(mod_all[:bsz], pad)
    modc = jnp.pad(mod_all[bsz:bsz + 1], pad)[0]

    theta = jnp.concatenate([ret_decay_fw[0], ret_decay_bw[0]]).astype(F32)
    theta = jnp.broadcast_to(theta[:, None], (2 * RET_HEADS, RET_DV))
    dmat, rowt, cdt, ctxw = _tables(theta, lc)

    win = w_in[0].astype(BF16)
    s_fw, s_bw = _ctx_states(ctx, modc, norm_mix_pre, win[:, OFF_K:OFF_G], ctxw)

    yc, q, k, v, sg, sgb = _inproj(x2, modt, norm_mix_pre, cos, sin, win, w_conv_out[0].astype(BF16),
                                   conv_w[0], conv_b, seq)
    ob = _ret_bw(q, k, v, s_bw, rowt, cdt, bsz, seq)
    x1, h2 = _ret_fw(q, k, v, sg, ob, yc, sgb, x2, s_fw, dmat, rowt, cdt, w_ret_out[0].astype(BF16),
                     w_o[0].astype(BF16), modt, norm_mix_post, norm_ffn_pre, bsz, seq)
    out = _ffn(h2, x1, modt, norm_ffn_post, w_ffn_up[0].astype(BF16), ffn_conv_w[0], ffn_conv_b,
               w_ffn_down[0].astype(BF16), seq)
    return out.reshape(bsz, seq, d)
```

```python
import functools

import jax
import jax.numpy as jnp
from jax import lax
from jax.experimental import pallas as pl
from jax.experimental.pallas import tpu as pltpu

F32 = jnp.float32
BF16 = jnp.bfloat16

D_MODEL = 1024
GRID_W = 64
CONV_W = 1024
RET_HEADS = 4
RET_DK = 256
RET_DV = 512
RET_QK = RET_HEADS * RET_DK
RET_V = RET_HEADS * RET_DV
ROPE_PAIRS = RET_DK // 4
ROPE_BASE = 10000.0
D_FF = 2816
N_MOD = 6
EPS = 1e-6

OFF_CX = 0
OFF_CB = OFF_CX + CONV_W
OFF_CC = OFF_CB + CONV_W
OFF_Q = OFF_CC + CONV_W
OFF_K = OFF_Q + RET_QK
OFF_V = OFF_K + RET_QK
OFF_G = OFF_V + RET_V
OFF_GA = OFF_G + RET_V
OFF_GB = OFF_GA + D_MODEL
IN_COLS = OFF_GB + D_MODEL

RET_C = 256
TM_IN = RET_C
TS_RET = 512
FFN_COLS = 8
FFN_BLOCKS = ((0, 512), (512, 512), (1024, 512), (1536, 512), (2048, 512), (2560, 256))
ADA_BN = 1536
MOD_ROWS = 8
VMEM_LIMIT = 60 * 1024 * 1024


def _const_spec(shape):
    zeros = (0,) * len(shape)
    return pl.BlockSpec(shape, lambda *_: zeros, pipeline_mode=pl.Buffered(1))


def _rms(xf):
    return xf * lax.rsqrt(jnp.mean(xf * xf, axis=-1, keepdims=True) + EPS)


def _sigmoid(x):
    return 1.0 / (1.0 + jnp.exp(-x))


def _dot(a, b):
    return jnp.dot(a, b, preferred_element_type=F32)


def _dot_nt(a, b):
    return lax.dot_general(a, b, (((1,), (1,)), ((), ())), preferred_element_type=F32)


def _dot_tn(a, b):
    return lax.dot_general(a, b, (((0,), (0,)), ((), ())), preferred_element_type=F32)


def _ada_kernel(c_ref, w_ref, b_ref, o_ref):
    cv = c_ref[...]
    s = cv * _sigmoid(cv)
    o_ref[...] = _dot(s.astype(BF16), w_ref[...].astype(BF16)) + b_ref[...]


def _ada(cin, w_ada, b_ada):
    ncol = w_ada.shape[1]
    return pl.pallas_call(
        _ada_kernel,
        name="ada",
        grid=(ncol // ADA_BN,),
        in_specs=[
            pl.BlockSpec((MOD_ROWS, D_MODEL), lambda j: (0, 0)),
            pl.BlockSpec((D_MODEL, ADA_BN), lambda j: (0, j)),
            pl.BlockSpec((1, ADA_BN), lambda j: (0, j)),
        ],
        out_specs=pl.BlockSpec((MOD_ROWS, ADA_BN), lambda j: (0, j)),
        out_shape=jax.ShapeDtypeStruct((MOD_ROWS, ncol), F32),
        compiler_params=pltpu.CompilerParams(dimension_semantics=("arbitrary",)),
    )(cin, w_ada, b_ada)


def _tables_kernel(theta_ref, dmat_ref, rowf_ref, rowb_ref, cdt_ref, ctxw_ref, *, ctx_len):
    th = theta_ref[...]
    lg = -(jnp.maximum(-th, 0.0) + jnp.log1p(jnp.exp(-jnp.abs(th))))
    n_i = lax.broadcasted_iota(jnp.int32, (RET_C, RET_C), 0)
    m_i = lax.broadcasted_iota(jnp.int32, (RET_C, RET_C), 1)
    diff = (n_i - m_i).astype(F32)
    idx = lax.broadcasted_iota(jnp.int32, (RET_C, RET_DK), 0).astype(F32)
    midx = lax.broadcasted_iota(jnp.int32, (ctx_len, RET_DK), 0).astype(F32)
    for h in range(RET_HEADS):
        lf = lg[h:h + 1, :]
        lb = lg[RET_HEADS + h:RET_HEADS + h + 1, :]
        lf_c, lb_c = lf[:, :RET_C], lb[:, :RET_C]
        dmat_ref[h] = jnp.exp(jnp.where(diff >= 0.0, diff * lf_c, -diff * lb_c))
        lf_k, lb_k = lf[:, :RET_DK], lb[:, :RET_DK]
        rowf_ref[h, 0] = jnp.exp((idx + 1.0) * lf_k)
        rowf_ref[h, 1] = jnp.exp((RET_C - 1.0 - idx) * lf_k)
        rowb_ref[h, 0] = jnp.exp((RET_C - idx) * lb_k)
        rowb_ref[h, 1] = jnp.exp(idx * lb_k)
        cdt_ref[h, 0] = jnp.exp(RET_C * lf)
        cdt_ref[h, 1] = jnp.exp(RET_C * lb)
        ctxw_ref[h, 0] = jnp.exp((ctx_len - 1.0 - midx) * lf_k)
        ctxw_ref[h, 1] = jnp.exp(midx * lb_k)


def _tables(theta, ctx_len):
    rows = jax.ShapeDtypeStruct((RET_HEADS, 2, RET_C, RET_DK), F32)
    return pl.pallas_call(
        functools.partial(_tables_kernel, ctx_len=ctx_len),
        name="tables",
        out_shape=(
            jax.ShapeDtypeStruct((RET_HEADS, RET_C, RET_C), F32),
            rows, rows,
            jax.ShapeDtypeStruct((RET_HEADS, 2, 1, RET_DV), F32),
            jax.ShapeDtypeStruct((RET_HEADS, 2, ctx_len, RET_DK), F32),
        ),
    )(theta)


def _ctx_kernel(ctx_ref, modc_ref, gain_ref, wkv_ref, ctxw_ref, sfw_ref, sbw_ref):
    mc = modc_ref[...]
    hc = _rms(ctx_ref[0]) * gain_ref[...] * (1.0 + mc[1:2]) + mc[0:1]
    kv = _dot(hc.astype(BF16), wkv_ref[...])
    for h in range(RET_HEADS):
        kh = kv[:, h * RET_DK:(h + 1) * RET_DK] * RET_DK ** -0.5
        vh = kv[:, RET_QK + h * RET_DV:RET_QK + (h + 1) * RET_DV].astype(BF16)
        sfw_ref[0, h] = _dot_tn((kh * ctxw_ref[h, 0]).astype(BF16), vh)
        sbw_ref[0, h] = _dot_tn((kh * ctxw_ref[h, 1]).astype(BF16), vh)


def _ctx_states(ctx, modc, gain, wkv, ctxw):
    bsz, lc, _ = ctx.shape
    state = jax.ShapeDtypeStruct((bsz, RET_HEADS, RET_DK, RET_DV), F32)
    sspec = pl.BlockSpec((1, RET_HEADS, RET_DK, RET_DV), lambda b: (b, 0, 0, 0))
    return pl.pallas_call(
        _ctx_kernel,
        name="ctx",
        grid=(bsz,),
        in_specs=[
            pl.BlockSpec((1, lc, D_MODEL), lambda b: (b, 0, 0)),
            pl.BlockSpec((MOD_ROWS, D_MODEL), lambda b: (0, 0)),
            pl.BlockSpec((1, D_MODEL), lambda b: (0, 0)),
            pl.BlockSpec(wkv.shape, lambda b: (0, 0)),
            pl.BlockSpec(ctxw.shape, lambda b: (0, 0, 0, 0)),
        ],
        out_specs=(sspec, sspec),
        out_shape=(state, state),
        compiler_params=pltpu.CompilerParams(dimension_semantics=("arbitrary",),
                                             vmem_limit_bytes=VMEM_LIMIT),
    )(ctx, modc, gain, wkv, ctxw)


def _inproj_kernel(x_ref, mod_ref, gain_ref, cos_ref, sin_ref, win_ref, wco_ref, cw_ref, cb_ref,
                   s0_ref, rowb_ref, cdt_ref,
                   yc_ref, q_ref, k_ref, v_ref, sg_ref, sgb_ref, ob_ref, t_ref):
    @pl.when(pl.program_id(1) == 0)
    def _():
        t_ref[...] = s0_ref[0]

    m = mod_ref[0]
    h = _rms(x_ref[...]) * gain_ref[...] * (1.0 + m[1:2]) + m[0:1]
    hb = h.astype(BF16)

    def proj(off, width):
        return _dot(hb, win_ref[:, off:off + width])

    u = proj(OFF_CC, CONV_W) * proj(OFF_CX, CONV_W)
    col = lax.broadcasted_iota(jnp.int32, (TM_IN, CONV_W), 0) % GRID_W
    prev = jnp.where(col == 0, 0.0, pltpu.roll(u, 1, 0))
    nxt = jnp.where(col == GRID_W - 1, 0.0, pltpu.roll(u, TM_IN - 1, 0))
    cw = cw_ref[...]
    conv = cw[0:1] * prev + cw[1:2] * u + cw[2:3] * nxt + cb_ref[...]
    z = (proj(OFF_CB, CONV_W) * conv).astype(BF16)
    yc_ref[...] = (_sigmoid(proj(OFF_GA, D_MODEL)) * _dot(z, wco_ref[...])).astype(BF16)
    sgb_ref[...] = _sigmoid(proj(OFF_GB, D_MODEL)).astype(BF16)

    cos = cos_ref[...]
    sin = sin_ref[...]
    half = RET_DK // 2
    for off, ref, scale in ((OFF_Q, q_ref, 1.0), (OFF_K, k_ref, RET_DK ** -0.5)):
        t = proj(off, RET_QK)
        for hh in range(RET_HEADS):
            lo = hh * RET_DK
            t1 = t[:, lo:lo + half]
            t2 = t[:, lo + half:lo + RET_DK]
            ref[:, lo:lo + half] = ((t1 * cos - t2 * sin) * scale).astype(BF16)
            ref[:, lo + half:lo + RET_DK] = ((t2 * cos + t1 * sin) * scale).astype(BF16)
    v_ref[...] = proj(OFF_V, RET_V).astype(BF16)
    g = proj(OFF_G, RET_V)
    sg_ref[...] = (g * _sigmoid(g)).astype(BF16)

    for hh in range(RET_HEADS):
        qh = q_ref[:, hh * RET_DK:(hh + 1) * RET_DK].astype(F32)
        kh = k_ref[:, hh * RET_DK:(hh + 1) * RET_DK].astype(F32)
        vh = v_ref[:, hh * RET_DV:(hh + 1) * RET_DV]
        t = t_ref[hh]
        qs = (qh * rowb_ref[hh, 0]).astype(BF16)
        ob_ref[:, hh * RET_DV:(hh + 1) * RET_DV] = _dot(qs, t.astype(BF16)).astype(BF16)
        ks = (kh * rowb_ref[hh, 1]).astype(BF16)
        t_ref[hh] = cdt_ref[hh, 1] * t + _dot_tn(ks, vh)


def _inproj(x2, modt, gain, cos, sin, win, wco, cw, cb, s_bw, rowb, cdt, bsz, seq):
    nt = seq // TM_IN

    def row_spec(width):
        return pl.BlockSpec((TM_IN, width), lambda b, j: (b * nt + nt - 1 - j, 0))

    def out(width):
        return jax.ShapeDtypeStruct((bsz * seq, width), BF16)

    rope = pl.BlockSpec((TM_IN, RET_DK // 2), lambda b, j: (nt - 1 - j, 0))
    return pl.pallas_call(
        _inproj_kernel,
        name="inproj",
        grid=(bsz, nt),
        in_specs=[
            row_spec(D_MODEL),
            pl.BlockSpec((1, MOD_ROWS, D_MODEL), lambda b, j: (b, 0, 0)),
            pl.BlockSpec((1, D_MODEL), lambda b, j: (0, 0)),
            rope, rope,
            _const_spec(win.shape),
            _const_spec(wco.shape),
            pl.BlockSpec(cw.shape, lambda b, j: (0, 0)),
            pl.BlockSpec((1, CONV_W), lambda b, j: (0, 0)),
            pl.BlockSpec((1, RET_HEADS, RET_DK, RET_DV), lambda b, j: (b, 0, 0, 0)),
            _const_spec(rowb.shape),
            _const_spec(cdt.shape),
        ],
        out_specs=(row_spec(D_MODEL), row_spec(RET_QK), row_spec(RET_QK), row_spec(RET_V),
                   row_spec(RET_V), row_spec(D_MODEL), row_spec(RET_V)),
        out_shape=(out(D_MODEL), out(RET_QK), out(RET_QK), out(RET_V), out(RET_V), out(D_MODEL),
                   out(RET_V)),
        scratch_shapes=[pltpu.VMEM((RET_HEADS, RET_DK, RET_DV), F32)],
        compiler_params=pltpu.CompilerParams(dimension_semantics=("arbitrary", "arbitrary"),
                                             vmem_limit_bytes=VMEM_LIMIT),
    )(x2, modt, gain, cos, sin, win, wco, cw, cb, s_bw, rowb, cdt)


def _ret_fw_kernel(q_ref, k_ref, v_ref, sg_ref, ob_ref, yc_ref, sgb_ref, x_ref, s0_ref, dmat_ref,
                   rowf_ref, cdt_ref, wro_ref, wo_ref, mod_ref, gpost_ref,
                   x1_ref, s_ref, gat_ref):
    @pl.when(pl.program_id(1) == 0)
    def _():
        s_ref[...] = s0_ref[0]

    for ci in range(TS_RET // RET_C):
        r0 = ci * RET_C
        for h in range(RET_HEADS):
            qb = q_ref[r0:r0 + RET_C, h * RET_DK:(h + 1) * RET_DK]
            kb = k_ref[r0:r0 + RET_C, h * RET_DK:(h + 1) * RET_DK]
            vh = v_ref[r0:r0 + RET_C, h * RET_DV:(h + 1) * RET_DV]
            scores = (_dot_nt(qb, kb) * dmat_ref[h]).astype(BF16)
            s = s_ref[h]
            qs = (qb.astype(F32) * rowf_ref[h, 0]).astype(BF16)
            o = (_dot(scores, vh) + _dot(qs, s.astype(BF16))
                 + ob_ref[r0:r0 + RET_C, h * RET_DV:(h + 1) * RET_DV].astype(F32))
            ks = (kb.astype(F32) * rowf_ref[h, 1]).astype(BF16)
            s_ref[h] = cdt_ref[h, 0] * s + _dot_tn(ks, vh)
            gate = sg_ref[r0:r0 + RET_C, h * RET_DV:(h + 1) * RET_DV].astype(F32)
            gat_ref[r0:r0 + RET_C, h * RET_DV:(h + 1) * RET_DV] = (gate * _rms(o)).astype(BF16)

    y_ret = _dot(gat_ref[...], wro_ref[...])
    merged = yc_ref[...].astype(F32) + sgb_ref[...].astype(F32) * y_ret
    y = _dot(merged.astype(BF16), wo_ref[...])
    x1_ref[...] = x_ref[...] + mod_ref[0][2:3] * (_rms(y) * gpost_ref[...])


def _ret_fw(q, k, v, sg, ob, yc, sgb, x2, s_fw, dmat, rowf, cdt, wro, wo, modt, gpost, bsz, seq):
    ns = seq // TS_RET

    def row_spec(width):
        return pl.BlockSpec((TS_RET, width), lambda b, j: (b * ns + j, 0))

    return pl.pallas_call(
        _ret_fw_kernel,
        name="ret_fw",
        grid=(bsz, ns),
        in_specs=[
            row_spec(RET_QK), row_spec(RET_QK), row_spec(RET_V), row_spec(RET_V), row_spec(RET_V),
            row_spec(D_MODEL), row_spec(D_MODEL), row_spec(D_MODEL),
            pl.BlockSpec((1, RET_HEADS, RET_DK, RET_DV), lambda b, j: (b, 0, 0, 0)),
            _const_spec(dmat.shape), _const_spec(rowf.shape), _const_spec(cdt.shape),
            _const_spec(wro.shape), _const_spec(wo.shape),
            pl.BlockSpec((1, MOD_ROWS, D_MODEL), lambda b, j: (b, 0, 0)),
            pl.BlockSpec((1, D_MODEL), lambda b, j: (0, 0)),
        ],
        out_specs=row_spec(D_MODEL),
        out_shape=jax.ShapeDtypeStruct((bsz * seq, D_MODEL), F32),
        scratch_shapes=[pltpu.VMEM((RET_HEADS, RET_DK, RET_DV), F32),
                        pltpu.VMEM((TS_RET, RET_V), BF16)],
        compiler_params=pltpu.CompilerParams(dimension_semantics=("arbitrary", "arbitrary"),
                                             vmem_limit_bytes=VMEM_LIMIT),
    )(q, k, v, sg, ob, yc, sgb, x2, s_fw, dmat, rowf, cdt, wro, wo, modt, gpost)


def _ffn_kernel(x1_ref, mod_ref, gpre_ref, gpost_ref, wup_ref, cw_ref, cb_ref, wdn_ref,
                out_ref, acc_ref, *, grid_rows):
    tm = grid_rows * FFN_COLS
    x1 = x1_ref[0].reshape(tm, D_MODEL)
    m = mod_ref[0]
    hb = (_rms(x1) * gpre_ref[...] * (1.0 + m[4:5]) + m[3:4]).astype(BF16)

    def conv(u, off, width):
        w = cw_ref[:, off:off + width]
        pad = jnp.zeros((FFN_COLS, width), F32)
        above = jnp.concatenate([pad, u[:tm - FFN_COLS]], axis=0)
        below = jnp.concatenate([u[FFN_COLS:], pad], axis=0)
        return w[0:1] * above + w[1:2] * u + w[2:3] * below + cb_ref[:, off:off + width]

    for c0, width in FFN_BLOCKS:
        g = conv(_dot(hb, wup_ref[:, c0:c0 + width]), c0, width)
        val = conv(_dot(hb, wup_ref[:, D_FF + c0:D_FF + c0 + width]), D_FF + c0, width)
        a = (g * _sigmoid(g) * val).astype(BF16)
        part = _dot(a, wdn_ref[c0:c0 + width, :])
        if c0 == 0:
            acc_ref[...] = part
        else:
            acc_ref[...] += part

    out = x1 + m[5:6] * (_rms(acc_ref[...]) * gpost_ref[...])
    out_ref[0] = out.reshape(grid_rows, FFN_COLS, D_MODEL)


def _ffn(x1g, modt, gpre, gpost, wup, cw, cb, wdn):
    bsz, grid_rows, grid_w, _ = x1g.shape
    strip = pl.BlockSpec((1, grid_rows, FFN_COLS, D_MODEL), lambda b, s: (b, 0, s, 0))
    vec = pl.BlockSpec((1, D_MODEL), lambda b, s: (0, 0))
    return pl.pallas_call(
        functools.partial(_ffn_kernel, grid_rows=grid_rows),
        name="ffn",
        grid=(bsz, grid_w // FFN_COLS),
        in_specs=[
            strip,
            pl.BlockSpec((1, MOD_ROWS, D_MODEL), lambda b, s: (b, 0, 0)),
            vec, vec,
            _const_spec(wup.shape),
            pl.BlockSpec(cw.shape, lambda b, s: (0, 0)),
            pl.BlockSpec((1, 2 * D_FF), lambda b, s: (0, 0)),
            _const_spec(wdn.shape),
        ],
        out_specs=strip,
        out_shape=jax.ShapeDtypeStruct(x1g.shape, F32),
        scratch_shapes=[pltpu.VMEM((grid_rows * FFN_COLS, D_MODEL), F32)],
        compiler_params=pltpu.CompilerParams(dimension_semantics=("arbitrary", "arbitrary"),
                                             vmem_limit_bytes=VMEM_LIMIT),
    )(x1g, modt, gpre, gpost, wup, cw, cb, wdn)


def _rotary_tables(n):
    t = jnp.arange(n)
    row = (t // GRID_W).astype(F32)
    col = (t % GRID_W).astype(F32)
    inv = ROPE_BASE ** (-jnp.arange(ROPE_PAIRS, dtype=F32) / ROPE_PAIRS)
    ang = jnp.concatenate([row[:, None] * inv, col[:, None] * inv], axis=-1)
    return jnp.cos(ang), jnp.sin(ang)


def kernel(x, c, ctx, c_ctx, w_ada, b_ada, norm_mix_pre, norm_mix_post, w_in, conv_w, conv_b,
           w_conv_out, ret_decay_fw, ret_decay_bw, w_ret_out, w_o, norm_ffn_pre, norm_ffn_post,
           w_ffn_up, ffn_conv_w, ffn_conv_b, w_ffn_down):
    bsz, seq, d = x.shape
    depth = w_ada.shape[0]
    assert d == D_MODEL and depth == 1 and bsz + 1 <= MOD_ROWS
    assert seq % TS_RET == 0 and seq % TM_IN == 0 and TM_IN % GRID_W == 0 and GRID_W % FFN_COLS == 0
    lc = ctx.shape[1]
    cos, sin = _rotary_tables(seq)
    x2 = x.reshape(bsz * seq, d)

    cin = jnp.zeros((MOD_ROWS, d), F32).at[:bsz].set(c).at[bsz].set(c_ctx)
    mod_all = _ada(cin, w_ada[0], b_ada).reshape(MOD_ROWS, N_MOD, d)
    pad = ((0, 0), (0, MOD_ROWS - N_MOD), (0, 0))
    modt = jnp.pad(mod_all[:bsz], pad)
    modc = jnp.pad(mod_all[bsz:bsz + 1], pad)[0]

    theta = jnp.concatenate([ret_decay_fw[0], ret_decay_bw[0]]).astype(F32)
    theta = jnp.broadcast_to(theta[:, None], (2 * RET_HEADS, RET_DV))
    dmat, rowf, rowb, cdt, ctxw = _tables(theta, lc)

    win = w_in[0].astype(BF16)
    s_fw, s_bw = _ctx_states(ctx, modc, norm_mix_pre, win[:, OFF_K:OFF_G], ctxw)

    yc, q, k, v, sg, sgb, ob = _inproj(x2, modt, norm_mix_pre, cos, sin, win,
                                       w_conv_out[0].astype(BF16), conv_w[0], conv_b,
                                       s_bw, rowb, cdt, bsz, seq)
    x1 = _ret_fw(q, k, v, sg, ob, yc, sgb, x2, s_fw, dmat, rowf, cdt, w_ret_out[0].astype(BF16),
                 w_o[0].astype(BF16), modt, norm_mix_post, bsz, seq)
    out = _ffn(x1.reshape(bsz, seq // GRID_W, GRID_W, d), modt, norm_ffn_pre, norm_ffn_post,
               w_ffn_up[0].astype(BF16), ffn_conv_w[0], ffn_conv_b, w_ffn_down[0].astype(BF16))
    return out.reshape(bsz, seq, d)
```

```python
import functools

import jax
import jax.numpy as jnp
from jax import lax
from jax.experimental import pallas as pl
from jax.experimental.pallas import tpu as pltpu

F32 = jnp.float32
BF16 = jnp.bfloat16

D_MODEL = 1024
GRID_W = 64
CONV_W = 1024
RET_HEADS = 4
RET_DK = 256
RET_DV = 512
RET_QK = RET_HEADS * RET_DK
RET_V = RET_HEADS * RET_DV
ROPE_PAIRS = RET_DK // 4
ROPE_BASE = 10000.0
D_FF = 2816
N_MOD = 6
EPS = 1e-6

OFF_CX = 0
OFF_CB = OFF_CX + CONV_W
OFF_CC = OFF_CB + CONV_W
OFF_Q = OFF_CC + CONV_W
OFF_K = OFF_Q + RET_QK
OFF_V = OFF_K + RET_QK
OFF_G = OFF_V + RET_V
OFF_GA = OFF_G + RET_V
OFF_GB = OFF_GA + D_MODEL
IN_COLS = OFF_GB + D_MODEL

RET_C = 256
TM_IN = RET_C
TS_RET = 512
FFN_COLS = 8
FFN_BLOCKS = ((0, 512), (512, 512), (1024, 512), (1536, 512), (2048, 512), (2560, 256))
ADA_BN = 1536
MOD_ROWS = 8
VMEM_LIMIT = 60 * 1024 * 1024


def _const_spec(shape):
    zeros = (0,) * len(shape)
    return pl.BlockSpec(shape, lambda *_: zeros, pipeline_mode=pl.Buffered(1))


def _rms(xf):
    return xf * lax.rsqrt(jnp.mean(xf * xf, axis=-1, keepdims=True) + EPS)


def _sigmoid(x):
    return 1.0 / (1.0 + jnp.exp(-x))


def _dot(a, b):
    return jnp.dot(a, b, preferred_element_type=F32)


def _dot_nt(a, b):
    return lax.dot_general(a, b, (((1,), (1,)), ((), ())), preferred_element_type=F32)


def _dot_tn(a, b):
    return lax.dot_general(a, b, (((0,), (0,)), ((), ())), preferred_element_type=F32)


def _ada_kernel(c_ref, w_ref, b_ref, o_ref):
    cv = c_ref[...]
    s = cv * _sigmoid(cv)
    o_ref[...] = _dot(s.astype(BF16), w_ref[...].astype(BF16)) + b_ref[...]


def _ada(cin, w_ada, b_ada):
    ncol = w_ada.shape[1]
    return pl.pallas_call(
        _ada_kernel,
        name="ada",
        grid=(ncol // ADA_BN,),
        in_specs=[
            pl.BlockSpec((MOD_ROWS, D_MODEL), lambda j: (0, 0)),
            pl.BlockSpec((D_MODEL, ADA_BN), lambda j: (0, j)),
            pl.BlockSpec((1, ADA_BN), lambda j: (0, j)),
        ],
        out_specs=pl.BlockSpec((MOD_ROWS, ADA_BN), lambda j: (0, j)),
        out_shape=jax.ShapeDtypeStruct((MOD_ROWS, ncol), F32),
        compiler_params=pltpu.CompilerParams(dimension_semantics=("arbitrary",)),
    )(cin, w_ada, b_ada)


def _tables_kernel(theta_ref, dmat_ref, rowf_ref, rowb_ref, cdt_ref, ctxw_ref, *, ctx_len):
    th = theta_ref[...]
    lg = -(jnp.maximum(-th, 0.0) + jnp.log1p(jnp.exp(-jnp.abs(th))))
    n_i = lax.broadcasted_iota(jnp.int32, (RET_C, RET_C), 0)
    m_i = lax.broadcasted_iota(jnp.int32, (RET_C, RET_C), 1)
    diff = (n_i - m_i).astype(F32)
    idx = lax.broadcasted_iota(jnp.int32, (RET_C, RET_DK), 0).astype(F32)
    midx = lax.broadcasted_iota(jnp.int32, (ctx_len, RET_DK), 0).astype(F32)
    for h in range(RET_HEADS):
        lf = lg[h:h + 1, :]
        lb = lg[RET_HEADS + h:RET_HEADS + h + 1, :]
        lf_c, lb_c = lf[:, :RET_C], lb[:, :RET_C]
        dmat_ref[h] = jnp.exp(jnp.where(diff >= 0.0, diff * lf_c, -diff * lb_c))
        lf_k, lb_k = lf[:, :RET_DK], lb[:, :RET_DK]
        rowf_ref[h, 0] = jnp.exp((idx + 1.0) * lf_k).astype(BF16)
        rowf_ref[h, 1] = jnp.exp((RET_C - 1.0 - idx) * lf_k).astype(BF16)
        rowb_ref[h, 0] = jnp.exp((RET_C - idx) * lb_k).astype(BF16)
        rowb_ref[h, 1] = jnp.exp(idx * lb_k).astype(BF16)
        cdt_ref[h, 0] = jnp.exp(RET_C * lf)
        cdt_ref[h, 1] = jnp.exp(RET_C * lb)
        ctxw_ref[h, 0] = jnp.exp((ctx_len - 1.0 - midx) * lf_k)
        ctxw_ref[h, 1] = jnp.exp(midx * lb_k)


def _tables(theta, ctx_len):
    rows = jax.ShapeDtypeStruct((RET_HEADS, 2, RET_C, RET_DK), BF16)
    return pl.pallas_call(
        functools.partial(_tables_kernel, ctx_len=ctx_len),
        name="tables",
        out_shape=(
            jax.ShapeDtypeStruct((RET_HEADS, RET_C, RET_C), F32),
            rows, rows,
            jax.ShapeDtypeStruct((RET_HEADS, 2, 1, RET_DV), F32),
            jax.ShapeDtypeStruct((RET_HEADS, 2, ctx_len, RET_DK), F32),
        ),
    )(theta)


def _ctx_kernel(ctx_ref, modc_ref, gain_ref, wkv_ref, ctxw_ref, sfw_ref, sbw_ref):
    mc = modc_ref[...]
    hc = _rms(ctx_ref[0]) * gain_ref[...] * (1.0 + mc[1:2]) + mc[0:1]
    kv = _dot(hc.astype(BF16), wkv_ref[...])
    for h in range(RET_HEADS):
        kh = kv[:, h * RET_DK:(h + 1) * RET_DK] * RET_DK ** -0.5
        vh = kv[:, RET_QK + h * RET_DV:RET_QK + (h + 1) * RET_DV].astype(BF16)
        sfw_ref[0, h] = _dot_tn((kh * ctxw_ref[h, 0]).astype(BF16), vh)
        sbw_ref[0, h] = _dot_tn((kh * ctxw_ref[h, 1]).astype(BF16), vh)


def _ctx_states(ctx, modc, gain, wkv, ctxw):
    bsz, lc, _ = ctx.shape
    state = jax.ShapeDtypeStruct((bsz, RET_HEADS, RET_DK, RET_DV), F32)
    sspec = pl.BlockSpec((1, RET_HEADS, RET_DK, RET_DV), lambda b: (b, 0, 0, 0))
    return pl.pallas_call(
        _ctx_kernel,
        name="ctx",
        grid=(bsz,),
        in_specs=[
            pl.BlockSpec((1, lc, D_MODEL), lambda b: (b, 0, 0)),
            pl.BlockSpec((MOD_ROWS, D_MODEL), lambda b: (0, 0)),
            pl.BlockSpec((1, D_MODEL), lambda b: (0, 0)),
            pl.BlockSpec(wkv.shape, lambda b: (0, 0)),
            pl.BlockSpec(ctxw.shape, lambda b: (0, 0, 0, 0)),
        ],
        out_specs=(sspec, sspec),
        out_shape=(state, state),
        compiler_params=pltpu.CompilerParams(dimension_semantics=("arbitrary",),
                                             vmem_limit_bytes=VMEM_LIMIT),
    )(ctx, modc, gain, wkv, ctxw)


def _inproj_kernel(x_ref, mod_ref, gain_ref, cos_ref, sin_ref, win_ref, wco_ref, cw_ref, cb_ref,
                   s0_ref, rowb_ref, cdt_ref,
                   yc_ref, q_ref, k_ref, v_ref, sg_ref, sgb_ref, ob_ref, t_ref):
    @pl.when(pl.program_id(1) == 0)
    def _():
        t_ref[...] = s0_ref[0]

    m = mod_ref[0]
    h = _rms(x_ref[...]) * gain_ref[...] * (1.0 + m[1:2]) + m[0:1]
    hb = h.astype(BF16)

    def proj(off, width):
        return _dot(hb, win_ref[:, off:off + width])

    u = proj(OFF_CC, CONV_W) * proj(OFF_CX, CONV_W)
    col = lax.broadcasted_iota(jnp.int32, (TM_IN, CONV_W), 0) % GRID_W
    prev = jnp.where(col == 0, 0.0, pltpu.roll(u, 1, 0))
    nxt = jnp.where(col == GRID_W - 1, 0.0, pltpu.roll(u, TM_IN - 1, 0))
    cw = cw_ref[...]
    conv = cw[0:1] * prev + cw[1:2] * u + cw[2:3] * nxt + cb_ref[...]
    z = (proj(OFF_CB, CONV_W) * conv).astype(BF16)
    yc_ref[...] = (_sigmoid(proj(OFF_GA, D_MODEL)) * _dot(z, wco_ref[...])).astype(BF16)
    sgb_ref[...] = _sigmoid(proj(OFF_GB, D_MODEL)).astype(BF16)

    cos = cos_ref[...]
    sin = sin_ref[...]
    half = RET_DK // 2
    for off, ref, scale in ((OFF_Q, q_ref, 1.0), (OFF_K, k_ref, RET_DK ** -0.5)):
        t = proj(off, RET_QK)
        for hh in range(RET_HEADS):
            lo = hh * RET_DK
            t1 = t[:, lo:lo + half]
            t2 = t[:, lo + half:lo + RET_DK]
            ref[:, lo:lo + half] = ((t1 * cos - t2 * sin) * scale).astype(BF16)
            ref[:, lo + half:lo + RET_DK] = ((t2 * cos + t1 * sin) * scale).astype(BF16)
    v_ref[...] = proj(OFF_V, RET_V).astype(BF16)
    g = proj(OFF_G, RET_V)
    sg_ref[...] = (g * _sigmoid(g)).astype(BF16)

    for hh in range(RET_HEADS):
        qs = q_ref[:, hh * RET_DK:(hh + 1) * RET_DK] * rowb_ref[hh, 0]
        ks = k_ref[:, hh * RET_DK:(hh + 1) * RET_DK] * rowb_ref[hh, 1]
        vh = v_ref[:, hh * RET_DV:(hh + 1) * RET_DV]
        t = t_ref[hh]
        ob_ref[:, hh * RET_DV:(hh + 1) * RET_DV] = _dot(qs, t.astype(BF16)).astype(BF16)
        t_ref[hh] = cdt_ref[hh, 1] * t + _dot_tn(ks, vh)


def _inproj(x2, modt, gain, cos, sin, win, wco, cw, cb, s_bw, rowb, cdt, bsz, seq):
    nt = seq // TM_IN

    def row_spec(width):
        return pl.BlockSpec((TM_IN, width), lambda b, j: (b * nt + nt - 1 - j, 0))

    def out(width):
        return jax.ShapeDtypeStruct((bsz * seq, width), BF16)

    rope = pl.BlockSpec((TM_IN, RET_DK // 2), lambda b, j: (nt - 1 - j, 0))
    return pl.pallas_call(
        _inproj_kernel,
        name="inproj",
        grid=(bsz, nt),
        in_specs=[
            row_spec(D_MODEL),
            pl.BlockSpec((1, MOD_ROWS, D_MODEL), lambda b, j: (b, 0, 0)),
            pl.BlockSpec((1, D_MODEL), lambda b, j: (0, 0)),
            rope, rope,
            _const_spec(win.shape),
            _const_spec(wco.shape),
            pl.BlockSpec(cw.shape, lambda b, j: (0, 0)),
            pl.BlockSpec((1, CONV_W), lambda b, j: (0, 0)),
            pl.BlockSpec((1, RET_HEADS, RET_DK, RET_DV), lambda b, j: (b, 0, 0, 0)),
            _const_spec(rowb.shape),
            _const_spec(cdt.shape),
        ],
        out_specs=(row_spec(D_MODEL), row_spec(RET_QK), row_spec(RET_QK), row_spec(RET_V),
                   row_spec(RET_V), row_spec(D_MODEL), row_spec(RET_V)),
        out_shape=(out(D_MODEL), out(RET_QK), out(RET_QK), out(RET_V), out(RET_V), out(D_MODEL),
                   out(RET_V)),
        scratch_shapes=[pltpu.VMEM((RET_HEADS, RET_DK, RET_DV), F32)],
        compiler_params=pltpu.CompilerParams(dimension_semantics=("arbitrary", "arbitrary"),
                                             vmem_limit_bytes=VMEM_LIMIT),
    )(x2, modt, gain, cos, sin, win, wco, cw, cb, s_bw, rowb, cdt)


def _ret_fw_kernel(q_ref, k_ref, v_ref, sg_ref, ob_ref, yc_ref, sgb_ref, x_ref, s0_ref, dmat_ref,
                   rowf_ref, cdt_ref, wro_ref, wo_ref, mod_ref, gpost_ref,
                   x1_ref, s_ref, gat_ref):
    @pl.when(pl.program_id(1) == 0)
    def _():
        s_ref[...] = s0_ref[0]

    for ci in range(TS_RET // RET_C):
        r0 = ci * RET_C
        for h in range(RET_HEADS):
            qb = q_ref[r0:r0 + RET_C, h * RET_DK:(h + 1) * RET_DK]
            kb = k_ref[r0:r0 + RET_C, h * RET_DK:(h + 1) * RET_DK]
            vh = v_ref[r0:r0 + RET_C, h * RET_DV:(h + 1) * RET_DV]
            scores = (_dot_nt(qb, kb) * dmat_ref[h]).astype(BF16)
            s = s_ref[h]
            o = (_dot(scores, vh) + _dot(qb * rowf_ref[h, 0], s.astype(BF16))
                 + ob_ref[r0:r0 + RET_C, h * RET_DV:(h + 1) * RET_DV].astype(F32))
            s_ref[h] = cdt_ref[h, 0] * s + _dot_tn(kb * rowf_ref[h, 1], vh)
            gate = sg_ref[r0:r0 + RET_C, h * RET_DV:(h + 1) * RET_DV].astype(F32)
            gat_ref[r0:r0 + RET_C, h * RET_DV:(h + 1) * RET_DV] = (gate * _rms(o)).astype(BF16)

    y_ret = _dot(gat_ref[...], wro_ref[...])
    merged = yc_ref[...].astype(F32) + sgb_ref[...].astype(F32) * y_ret
    y = _dot(merged.astype(BF16), wo_ref[...])
    x1_ref[...] = x_ref[...] + mod_ref[0][2:3] * (_rms(y) * gpost_ref[...])


def _ret_fw(q, k, v, sg, ob, yc, sgb, x2, s_fw, dmat, rowf, cdt, wro, wo, modt, gpost, bsz, seq):
    ns = seq // TS_RET

    def row_spec(width):
        return pl.BlockSpec((TS_RET, width), lambda b, j: (b * ns + j, 0))

    return pl.pallas_call(
        _ret_fw_kernel,
        name="ret_fw",
        grid=(bsz, ns),
        in_specs=[
            row_spec(RET_QK), row_spec(RET_QK), row_spec(RET_V), row_spec(RET_V), row_spec(RET_V),
            row_spec(D_MODEL), row_spec(D_MODEL), row_spec(D_MODEL),
            pl.BlockSpec((1, RET_HEADS, RET_DK, RET_DV), lambda b, j: (b, 0, 0, 0)),
            _const_spec(dmat.shape), _const_spec(rowf.shape), _const_spec(cdt.shape),
            _const_spec(wro.shape), _const_spec(wo.shape),
            pl.BlockSpec((1, MOD_ROWS, D_MODEL), lambda b, j: (b, 0, 0)),
            pl.BlockSpec((1, D_MODEL), lambda b, j: (0, 0)),
        ],
        out_specs=row_spec(D_MODEL),
        out_shape=jax.ShapeDtypeStruct((bsz * seq, D_MODEL), F32),
        scratch_shapes=[pltpu.VMEM((RET_HEADS, RET_DK, RET_DV), F32),
                        pltpu.VMEM((TS_RET, RET_V), BF16)],
        compiler_params=pltpu.CompilerParams(dimension_semantics=("arbitrary", "arbitrary"),
                                             vmem_limit_bytes=VMEM_LIMIT),
    )(q, k, v, sg, ob, yc, sgb, x2, s_fw, dmat, rowf, cdt, wro, wo, modt, gpost)


def _ffn_kernel(x1_ref, mod_ref, gpre_ref, gpost_ref, wup_ref, cw_ref, cb_ref, wdn_ref,
                out_ref, acc_ref, *, grid_rows):
    tm = grid_rows * FFN_COLS
    x1 = x1_ref[0].reshape(tm, D_MODEL)
    m = mod_ref[0]
    hb = (_rms(x1) * gpre_ref[...] * (1.0 + m[4:5]) + m[3:4]).astype(BF16)

    def conv(u, off, width):
        w = cw_ref[:, off:off + width]
        pad = jnp.zeros((FFN_COLS, width), F32)
        above = jnp.concatenate([pad, u[:tm - FFN_COLS]], axis=0)
        below = jnp.concatenate([u[FFN_COLS:], pad], axis=0)
        return w[0:1] * above + w[1:2] * u + w[2:3] * below + cb_ref[:, off:off + width]

    def up(c0, width):
        return (_dot(hb, wup_ref[:, c0:c0 + width]),
                _dot(hb, wup_ref[:, D_FF + c0:D_FF + c0 + width]))

    ahead = up(*FFN_BLOCKS[0])
    for j, (c0, width) in enumerate(FFN_BLOCKS):
        ug, uv = ahead
        if j + 1 < len(FFN_BLOCKS):
            ahead = up(*FFN_BLOCKS[j + 1])
        g = conv(ug, c0, width)
        val = conv(uv, D_FF + c0, width)
        a = (g * _sigmoid(g) * val).astype(BF16)
        part = _dot(a, wdn_ref[c0:c0 + width, :])
        if c0 == 0:
            acc_ref[...] = part
        else:
            acc_ref[...] += part

    out = x1 + m[5:6] * (_rms(acc_ref[...]) * gpost_ref[...])
    out_ref[0] = out.reshape(grid_rows, FFN_COLS, D_MODEL)


def _ffn(x1g, modt, gpre, gpost, wup, cw, cb, wdn):
    bsz, grid_rows, grid_w, _ = x1g.shape
    strip = pl.BlockSpec((1, grid_rows, FFN_COLS, D_MODEL), lambda b, s: (b, 0, s, 0))
    vec = pl.BlockSpec((1, D_MODEL), lambda b, s: (0, 0))
    return pl.pallas_call(
        functools.partial(_ffn_kernel, grid_rows=grid_rows),
        name="ffn",
        grid=(bsz, grid_w // FFN_COLS),
        in_specs=[
            strip,
            pl.BlockSpec((1, MOD_ROWS, D_MODEL), lambda b, s: (b, 0, 0)),
            vec, vec,
            _const_spec(wup.shape),
            pl.BlockSpec(cw.shape, lambda b, s: (0, 0)),
            pl.BlockSpec((1, 2 * D_FF), lambda b, s: (0, 0)),
            _const_spec(wdn.shape),
        ],
        out_specs=strip,
        out_shape=jax.ShapeDtypeStruct(x1g.shape, F32),
        scratch_shapes=[pltpu.VMEM((grid_rows * FFN_COLS, D_MODEL), F32)],
        compiler_params=pltpu.CompilerParams(dimension_semantics=("arbitrary", "arbitrary"),
                                             vmem_limit_bytes=VMEM_LIMIT),
    )(x1g, modt, gpre, gpost, wup, cw, cb, wdn)


def _rotary_tables(n):
    t = jnp.arange(n)
    row = (t // GRID_W).astype(F32)
    col = (t % GRID_W).astype(F32)
    inv = ROPE_BASE ** (-jnp.arange(ROPE_PAIRS, dtype=F32) / ROPE_PAIRS)
    ang = jnp.concatenate([row[:, None] * inv, col[:, None] * inv], axis=-1)
    return jnp.cos(ang), jnp.sin(ang)


def kernel(x, c, ctx, c_ctx, w_ada, b_ada, norm_mix_pre, norm_mix_post, w_in, conv_w, conv_b,
           w_conv_out, ret_decay_fw, ret_decay_bw, w_ret_out, w_o, norm_ffn_pre, norm_ffn_post,
           w_ffn_up, ffn_conv_w, ffn_conv_b, w_ffn_down):
    bsz, seq, d = x.shape
    depth = w_ada.shape[0]
    assert d == D_MODEL and depth == 1 and bsz + 1 <= MOD_ROWS
    assert seq % TS_RET == 0 and seq % TM_IN == 0 and TM_IN % GRID_W == 0 and GRID_W % FFN_COLS == 0
    lc = ctx.shape[1]
    cos, sin = _rotary_tables(seq)
    x2 = x.reshape(bsz * seq, d)

    cin = jnp.zeros((MOD_ROWS, d), F32).at[:bsz].set(c).at[bsz].set(c_ctx)
    mod_all = _ada(cin, w_ada[0], b_ada).reshape(MOD_ROWS, N_MOD, d)
    pad = ((0, 0), (0, MOD_ROWS - N_MOD), (0, 0))
    modt = jnp.pad(mod_all[:bsz], pad)
    modc = jnp.pad(mod_all[bsz:bsz + 1], pad)[0]

    theta = jnp.concatenate([ret_decay_fw[0], ret_decay_bw[0]]).astype(F32)
    theta = jnp.broadcast_to(theta[:, None], (2 * RET_HEADS, RET_DV))
    dmat, rowf, rowb, cdt, ctxw = _tables(theta, lc)

    win = w_in[0].astype(BF16)
    s_fw, s_bw = _ctx_states(ctx, modc, norm_mix_pre, win[:, OFF_K:OFF_G], ctxw)

    yc, q, k, v, sg, sgb, ob = _inproj(x2, modt, norm_mix_pre, cos, sin, win,
                                       w_conv_out[0].astype(BF16), conv_w[0], conv_b,
                                       s_bw, rowb, cdt, bsz, seq)
    x1 = _ret_fw(q, k, v, sg, ob, yc, sgb, x2, s_fw, dmat, rowf, cdt, w_ret_out[0].astype(BF16),
                 w_o[0].astype(BF16), modt, norm_mix_post, bsz, seq)
    out = _ffn(x1.reshape(bsz, seq // GRID_W, GRID_W, d), modt, norm_ffn_pre, norm_ffn_post,
               w_ffn_up[0].astype(BF16), ffn_conv_w[0], ffn_conv_b, w_ffn_down[0].astype(BF16))
    return out.reshape(bsz, seq, d)
```

```python
import functools

import jax
import jax.numpy as jnp
import numpy as np
from jax import lax
from jax.experimental import pallas as pl
from jax.experimental.pallas import tpu as pltpu

F32 = jnp.float32
BF16 = jnp.bfloat16

D_MODEL = 1024
GRID_W = 64
CONV_W = 1024
RET_HEADS = 4
RET_DK = 256
RET_DV = 512
RET_QK = RET_HEADS * RET_DK
RET_V = RET_HEADS * RET_DV
ROPE_PAIRS = RET_DK // 4
ROPE_BASE = 10000.0
D_FF = 2816
N_MOD = 6
EPS = 1e-6

OFF_CX = 0
OFF_CB = OFF_CX + CONV_W
OFF_CC = OFF_CB + CONV_W
OFF_Q = OFF_CC + CONV_W
OFF_K = OFF_Q + RET_QK
OFF_V = OFF_K + RET_QK
OFF_G = OFF_V + RET_V
OFF_GA = OFF_G + RET_V
OFF_GB = OFF_GA + D_MODEL
IN_COLS = OFF_GB + D_MODEL

RET_C = 256
TM_IN = RET_C
TS_RET = 512
FFN_COLS = 8
FFN_BLOCKS = ((0, 1024), (1024, 1024), (2048, 768))
FFN_EPI_PIECES = 4
ADA_BN = 1536
MOD_ROWS = 8
VMEM_LIMIT = 60 * 1024 * 1024


def _const_spec(shape):
    zeros = (0,) * len(shape)
    return pl.BlockSpec(shape, lambda *_: zeros, pipeline_mode=pl.Buffered(1))


def _rms(xf):
    return xf * lax.rsqrt(jnp.mean(xf * xf, axis=-1, keepdims=True) + EPS)


def _sigmoid(x):
    return 1.0 / (1.0 + jnp.exp(-x))


def _dot(a, b):
    return jnp.dot(a, b, preferred_element_type=F32)


def _dot_nt(a, b):
    return lax.dot_general(a, b, (((1,), (1,)), ((), ())), preferred_element_type=F32)


def _dot_tn(a, b):
    return lax.dot_general(a, b, (((0,), (0,)), ((), ())), preferred_element_type=F32)


def _ada_kernel(c_ref, w_ref, b_ref, o_ref):
    cv = c_ref[...]
    s = cv * _sigmoid(cv)
    o_ref[...] = _dot(s.astype(BF16), w_ref[...].astype(BF16)) + b_ref[...]


def _ada(cin, w_ada, b_ada):
    ncol = w_ada.shape[1]
    return pl.pallas_call(
        _ada_kernel,
        name="ada",
        grid=(ncol // ADA_BN,),
        in_specs=[
            pl.BlockSpec((MOD_ROWS, D_MODEL), lambda j: (0, 0)),
            pl.BlockSpec((D_MODEL, ADA_BN), lambda j: (0, j)),
            pl.BlockSpec((1, ADA_BN), lambda j: (0, j)),
        ],
        out_specs=pl.BlockSpec((MOD_ROWS, ADA_BN), lambda j: (0, j)),
        out_shape=jax.ShapeDtypeStruct((MOD_ROWS, ncol), F32),
        compiler_params=pltpu.CompilerParams(dimension_semantics=("arbitrary",)),
    )(cin, w_ada, b_ada)


def _tables_kernel(theta_ref, dmat_ref, rowf_ref, rowb_ref, cdt_ref, ctxw_ref, *, ctx_len):
    th = theta_ref[...]
    lg = -(jnp.maximum(-th, 0.0) + jnp.log1p(jnp.exp(-jnp.abs(th))))
    n_i = lax.broadcasted_iota(jnp.int32, (RET_C, RET_C), 0)
    m_i = lax.broadcasted_iota(jnp.int32, (RET_C, RET_C), 1)
    diff = (n_i - m_i).astype(F32)
    idx = lax.broadcasted_iota(jnp.int32, (RET_C, RET_DK), 0).astype(F32)
    midx = lax.broadcasted_iota(jnp.int32, (ctx_len, RET_DK), 0).astype(F32)
    for h in range(RET_HEADS):
        lf = lg[h:h + 1, :]
        lb = lg[RET_HEADS + h:RET_HEADS + h + 1, :]
        lf_c, lb_c = lf[:, :RET_C], lb[:, :RET_C]
        dmat_ref[h] = jnp.exp(jnp.where(diff >= 0.0, diff * lf_c, -diff * lb_c))
        lf_k, lb_k = lf[:, :RET_DK], lb[:, :RET_DK]
        rowf_ref[h, 0] = jnp.exp((idx + 1.0) * lf_k).astype(BF16)
        rowf_ref[h, 1] = jnp.exp((RET_C - 1.0 - idx) * lf_k).astype(BF16)
        rowb_ref[h, 0] = jnp.exp((RET_C - idx) * lb_k).astype(BF16)
        rowb_ref[h, 1] = jnp.exp(idx * lb_k).astype(BF16)
        cdt_ref[h, 0] = jnp.exp(RET_C * lf)
        cdt_ref[h, 1] = jnp.exp(RET_C * lb)
        ctxw_ref[h, 0] = jnp.exp((ctx_len - 1.0 - midx) * lf_k)
        ctxw_ref[h, 1] = jnp.exp(midx * lb_k)


def _tables(theta, ctx_len):
    rows = jax.ShapeDtypeStruct((RET_HEADS, 2, RET_C, RET_DK), BF16)
    return pl.pallas_call(
        functools.partial(_tables_kernel, ctx_len=ctx_len),
        name="tables",
        out_shape=(
            jax.ShapeDtypeStruct((RET_HEADS, RET_C, RET_C), F32),
            rows, rows,
            jax.ShapeDtypeStruct((RET_HEADS, 2, 1, RET_DV), F32),
            jax.ShapeDtypeStruct((RET_HEADS, 2, ctx_len, RET_DK), F32),
        ),
    )(theta)


def _ctx_kernel(ctx_ref, modc_ref, gain_ref, wk_ref, wv0_ref, wv1_ref, ctxw_ref, sfw_ref, sbw_ref):
    mc = modc_ref[...]
    hc = (_rms(ctx_ref[0]) * gain_ref[...] * (1.0 + mc[1:2]) + mc[0:1]).astype(BF16)
    kv = jnp.concatenate([_dot(hc, w_ref[...].astype(BF16)) for w_ref in (wk_ref, wv0_ref, wv1_ref)],
                         axis=1)
    for h in range(RET_HEADS):
        kh = kv[:, h * RET_DK:(h + 1) * RET_DK] * RET_DK ** -0.5
        vh = kv[:, RET_QK + h * RET_DV:RET_QK + (h + 1) * RET_DV].astype(BF16)
        sfw_ref[0, h] = _dot_tn((kh * ctxw_ref[h, 0]).astype(BF16), vh)
        sbw_ref[0, h] = _dot_tn((kh * ctxw_ref[h, 1]).astype(BF16), vh)


def _ctx_states(ctx, modc, gain, w_in, ctxw):
    bsz, lc, _ = ctx.shape
    assert OFF_K % RET_QK == 0 and OFF_V % RET_QK == 0 and RET_V == 2 * RET_QK

    def wcols(off):
        return pl.BlockSpec((D_MODEL, RET_QK), lambda b: (0, off // RET_QK))

    state = jax.ShapeDtypeStruct((bsz, RET_HEADS, RET_DK, RET_DV), F32)
    sspec = pl.BlockSpec((1, RET_HEADS, RET_DK, RET_DV), lambda b: (b, 0, 0, 0))
    return pl.pallas_call(
        _ctx_kernel,
        name="ctx",
        grid=(bsz,),
        in_specs=[
            pl.BlockSpec((1, lc, D_MODEL), lambda b: (b, 0, 0)),
            pl.BlockSpec((MOD_ROWS, D_MODEL), lambda b: (0, 0)),
            pl.BlockSpec((1, D_MODEL), lambda b: (0, 0)),
            wcols(OFF_K), wcols(OFF_V), wcols(OFF_V + RET_QK),
            pl.BlockSpec(ctxw.shape, lambda b: (0, 0, 0, 0)),
        ],
        out_specs=(sspec, sspec),
        out_shape=(state, state),
        compiler_params=pltpu.CompilerParams(dimension_semantics=("arbitrary",),
                                             vmem_limit_bytes=VMEM_LIMIT),
    )(ctx, modc, gain, w_in, w_in, w_in, ctxw)


def _inproj_kernel(x_ref, mod_ref, gain_ref, cos_ref, sin_ref, win_ref, wco_ref, cw_ref, cb_ref,
                   s0_ref, rowb_ref, cdt_ref,
                   yc_ref, q_ref, k_ref, v_ref, sg_ref, sgb_ref, ob_ref, t_ref):
    @pl.when(pl.program_id(1) == 0)
    def _():
        t_ref[...] = s0_ref[0]

    m = mod_ref[0]
    h = _rms(x_ref[...]) * gain_ref[...] * (1.0 + m[1:2]) + m[0:1]
    hb = h.astype(BF16)

    def proj(off, width):
        return _dot(hb, win_ref[:, off:off + width])

    u = proj(OFF_CC, CONV_W) * proj(OFF_CX, CONV_W)
    col = lax.broadcasted_iota(jnp.int32, (TM_IN, CONV_W), 0) % GRID_W
    prev = jnp.where(col == 0, 0.0, pltpu.roll(u, 1, 0))
    nxt = jnp.where(col == GRID_W - 1, 0.0, pltpu.roll(u, TM_IN - 1, 0))
    cw = cw_ref[...]
    conv = cw[0:1] * prev + cw[1:2] * u + cw[2:3] * nxt + cb_ref[...]
    z = (proj(OFF_CB, CONV_W) * conv).astype(BF16)
    yc_ref[...] = (_sigmoid(proj(OFF_GA, D_MODEL)) * _dot(z, wco_ref[...])).astype(BF16)
    sgb_ref[...] = _sigmoid(proj(OFF_GB, D_MODEL)).astype(BF16)

    cos = cos_ref[...]
    sin = sin_ref[...]
    half = RET_DK // 2
    for off, ref, scale in ((OFF_Q, q_ref, 1.0), (OFF_K, k_ref, RET_DK ** -0.5)):
        t = proj(off, RET_QK)
        for hh in range(RET_HEADS):
            lo = hh * RET_DK
            t1 = t[:, lo:lo + half]
            t2 = t[:, lo + half:lo + RET_DK]
            ref[:, lo:lo + half] = ((t1 * cos - t2 * sin) * scale).astype(BF16)
            ref[:, lo + half:lo + RET_DK] = ((t2 * cos + t1 * sin) * scale).astype(BF16)
    v_ref[...] = proj(OFF_V, RET_V).astype(BF16)
    g = proj(OFF_G, RET_V)
    sg_ref[...] = (g * _sigmoid(g)).astype(BF16)

    for hh in range(RET_HEADS):
        qs = q_ref[:, hh * RET_DK:(hh + 1) * RET_DK] * rowb_ref[hh, 0]
        ks = k_ref[:, hh * RET_DK:(hh + 1) * RET_DK] * rowb_ref[hh, 1]
        vh = v_ref[:, hh * RET_DV:(hh + 1) * RET_DV]
        t = t_ref[hh]
        ob_ref[:, hh * RET_DV:(hh + 1) * RET_DV] = _dot(qs, t.astype(BF16)).astype(BF16)
        t_ref[hh] = cdt_ref[hh, 1] * t + _dot_tn(ks, vh)


def _inproj(x2, modt, gain, cos, sin, win, wco, cw, cb, s_bw, rowb, cdt, bsz, seq):
    nt = seq // TM_IN

    def row_spec(width):
        return pl.BlockSpec((TM_IN, width), lambda b, j: (b * nt + nt - 1 - j, 0))

    def out(width):
        return jax.ShapeDtypeStruct((bsz * seq, width), BF16)

    rope = pl.BlockSpec((TM_IN, RET_DK // 2), lambda b, j: (nt - 1 - j, 0))
    return pl.pallas_call(
        _inproj_kernel,
        name="inproj",
        grid=(bsz, nt),
        in_specs=[
            row_spec(D_MODEL),
            pl.BlockSpec((1, MOD_ROWS, D_MODEL), lambda b, j: (b, 0, 0)),
            pl.BlockSpec((1, D_MODEL), lambda b, j: (0, 0)),
            rope, rope,
            _const_spec(win.shape),
            _const_spec(wco.shape),
            pl.BlockSpec(cw.shape, lambda b, j: (0, 0)),
            pl.BlockSpec((1, CONV_W), lambda b, j: (0, 0)),
            pl.BlockSpec((1, RET_HEADS, RET_DK, RET_DV), lambda b, j: (b, 0, 0, 0)),
            _const_spec(rowb.shape),
            _const_spec(cdt.shape),
        ],
        out_specs=(row_spec(D_MODEL), row_spec(RET_QK), row_spec(RET_QK), row_spec(RET_V),
                   row_spec(RET_V), row_spec(D_MODEL), row_spec(RET_V)),
        out_shape=(out(D_MODEL), out(RET_QK), out(RET_QK), out(RET_V), out(RET_V), out(D_MODEL),
                   out(RET_V)),
        scratch_shapes=[pltpu.VMEM((RET_HEADS, RET_DK, RET_DV), F32)],
        compiler_params=pltpu.CompilerParams(dimension_semantics=("arbitrary", "arbitrary"),
                                             vmem_limit_bytes=VMEM_LIMIT),
    )(x2, modt, gain, cos, sin, win, wco, cw, cb, s_bw, rowb, cdt)


def _ret_fw_kernel(q_ref, k_ref, v_ref, sg_ref, ob_ref, yc_ref, sgb_ref, x_ref, s0_ref, dmat_ref,
                   rowf_ref, cdt_ref, wro_ref, wo_ref, mod_ref, gpost_ref,
                   x1_ref, s_ref, gat_ref):
    @pl.when(pl.program_id(1) == 0)
    def _():
        s_ref[...] = s0_ref[0]

    for ci in range(TS_RET // RET_C):
        r0 = ci * RET_C
        for h in range(RET_HEADS):
            qb = q_ref[r0:r0 + RET_C, h * RET_DK:(h + 1) * RET_DK]
            kb = k_ref[r0:r0 + RET_C, h * RET_DK:(h + 1) * RET_DK]
            vh = v_ref[r0:r0 + RET_C, h * RET_DV:(h + 1) * RET_DV]
            scores = (_dot_nt(qb, kb) * dmat_ref[h]).astype(BF16)
            s = s_ref[h]
            o = (_dot(scores, vh) + _dot(qb * rowf_ref[h, 0], s.astype(BF16))
                 + ob_ref[r0:r0 + RET_C, h * RET_DV:(h + 1) * RET_DV].astype(F32))
            s_ref[h] = cdt_ref[h, 0] * s + _dot_tn(kb * rowf_ref[h, 1], vh)
            gate = sg_ref[r0:r0 + RET_C, h * RET_DV:(h + 1) * RET_DV].astype(F32)
            gat_ref[r0:r0 + RET_C, h * RET_DV:(h + 1) * RET_DV] = (gate * _rms(o)).astype(BF16)

    y_ret = _dot(gat_ref[...], wro_ref[...])
    merged = yc_ref[...].astype(F32) + sgb_ref[...].astype(F32) * y_ret
    y = _dot(merged.astype(BF16), wo_ref[...])
    x1_ref[...] = x_ref[...] + mod_ref[0][2:3] * (_rms(y) * gpost_ref[...])


def _ret_fw(q, k, v, sg, ob, yc, sgb, x2, s_fw, dmat, rowf, cdt, wro, wo, modt, gpost, bsz, seq):
    ns = seq // TS_RET

    def row_spec(width):
        return pl.BlockSpec((TS_RET, width), lambda b, j: (b * ns + j, 0))

    return pl.pallas_call(
        _ret_fw_kernel,
        name="ret_fw",
        grid=(bsz, ns),
        in_specs=[
            row_spec(RET_QK), row_spec(RET_QK), row_spec(RET_V), row_spec(RET_V), row_spec(RET_V),
            row_spec(D_MODEL), row_spec(D_MODEL), row_spec(D_MODEL),
            pl.BlockSpec((1, RET_HEADS, RET_DK, RET_DV), lambda b, j: (b, 0, 0, 0)),
            _const_spec(dmat.shape), _const_spec(rowf.shape), _const_spec(cdt.shape),
            _const_spec(wro.shape), _const_spec(wo.shape),
            pl.BlockSpec((1, MOD_ROWS, D_MODEL), lambda b, j: (b, 0, 0)),
            pl.BlockSpec((1, D_MODEL), lambda b, j: (0, 0)),
        ],
        out_specs=row_spec(D_MODEL),
        out_shape=jax.ShapeDtypeStruct((bsz * seq, D_MODEL), F32),
        scratch_shapes=[pltpu.VMEM((RET_HEADS, RET_DK, RET_DV), F32),
                        pltpu.VMEM((TS_RET, RET_V), BF16)],
        compiler_params=pltpu.CompilerParams(dimension_semantics=("arbitrary", "arbitrary"),
                                             vmem_limit_bytes=VMEM_LIMIT),
    )(q, k, v, sg, ob, yc, sgb, x2, s_fw, dmat, rowf, cdt, wro, wo, modt, gpost)


def _ffn_kernel(x1_ref, mod_ref, gpre_ref, gpost_ref, wup_ref, cw_ref, cb_ref, wdn_ref,
                out_ref, acc_ref, *, grid_rows):
    tm = grid_rows * FFN_COLS
    x1 = x1_ref[0].reshape(tm, D_MODEL)
    m = mod_ref[0]
    hb = (_rms(x1) * gpre_ref[...] * (1.0 + m[4:5]) + m[3:4]).astype(BF16)

    def conv(u, off, width):
        w = cw_ref[:, off:off + width]
        pad = jnp.zeros((FFN_COLS, width), F32)
        above = jnp.concatenate([pad, u[:tm - FFN_COLS]], axis=0)
        below = jnp.concatenate([u[FFN_COLS:], pad], axis=0)
        return w[0:1] * above + w[1:2] * u + w[2:3] * below + cb_ref[:, off:off + width]

    def up(c0, width):
        return (_dot(hb, wup_ref[:, c0:c0 + width]),
                _dot(hb, wup_ref[:, D_FF + c0:D_FF + c0 + width]))

    ahead = up(*FFN_BLOCKS[0])
    for j, (c0, width) in enumerate(FFN_BLOCKS):
        ug, uv = ahead
        if j + 1 < len(FFN_BLOCKS):
            ahead = up(*FFN_BLOCKS[j + 1])
        g = conv(ug, c0, width)
        val = conv(uv, D_FF + c0, width)
        a = (g * _sigmoid(g) * val).astype(BF16)
        if j == 0:
            acc_ref[...] = _dot(a, wdn_ref[c0:c0 + width, :])
        elif j + 1 < len(FFN_BLOCKS):
            acc_ref[...] += _dot(a, wdn_ref[c0:c0 + width, :])
        else:
            gr = grid_rows // FFN_EPI_PIECES
            for p in range(FFN_EPI_PIECES):
                rows = slice(p * gr * FFN_COLS, (p + 1) * gr * FFN_COLS)
                f = acc_ref[rows, :] + _dot(a[rows, :], wdn_ref[c0:c0 + width, :])
                res = x1[rows, :] + m[5:6] * (_rms(f) * gpost_ref[...])
                out_ref[0, p * gr:(p + 1) * gr] = res.reshape(gr, FFN_COLS, D_MODEL)


def _ffn(x1g, modt, gpre, gpost, wup, cw, cb, wdn):
    bsz, grid_rows, grid_w, _ = x1g.shape
    strip = pl.BlockSpec((1, grid_rows, FFN_COLS, D_MODEL), lambda b, s: (b, 0, s, 0))
    vec = pl.BlockSpec((1, D_MODEL), lambda b, s: (0, 0))
    return pl.pallas_call(
        functools.partial(_ffn_kernel, grid_rows=grid_rows),
        name="ffn",
        grid=(bsz, grid_w // FFN_COLS),
        in_specs=[
            strip,
            pl.BlockSpec((1, MOD_ROWS, D_MODEL), lambda b, s: (b, 0, 0)),
            vec, vec,
            _const_spec(wup.shape),
            pl.BlockSpec(cw.shape, lambda b, s: (0, 0)),
            pl.BlockSpec((1, 2 * D_FF), lambda b, s: (0, 0)),
            _const_spec(wdn.shape),
        ],
        out_specs=strip,
        out_shape=jax.ShapeDtypeStruct(x1g.shape, F32),
        scratch_shapes=[pltpu.VMEM((grid_rows * FFN_COLS, D_MODEL), F32)],
        compiler_params=pltpu.CompilerParams(dimension_semantics=("arbitrary", "arbitrary"),
                                             vmem_limit_bytes=VMEM_LIMIT),
    )(x1g, modt, gpre, gpost, wup, cw, cb, wdn)


def _rotary_tables(n):
    t = np.arange(n)
    row = (t // GRID_W).astype(np.float64)
    col = (t % GRID_W).astype(np.float64)
    inv = ROPE_BASE ** (-np.arange(ROPE_PAIRS, dtype=np.float64) / ROPE_PAIRS)
    ang = np.concatenate([row[:, None] * inv, col[:, None] * inv], axis=-1)
    return jnp.asarray(np.cos(ang), F32), jnp.asarray(np.sin(ang), F32)


def kernel(x, c, ctx, c_ctx, w_ada, b_ada, norm_mix_pre, norm_mix_post, w_in, conv_w, conv_b,
           w_conv_out, ret_decay_fw, ret_decay_bw, w_ret_out, w_o, norm_ffn_pre, norm_ffn_post,
           w_ffn_up, ffn_conv_w, ffn_conv_b, w_ffn_down):
    bsz, seq, d = x.shape
    depth = w_ada.shape[0]
    assert d == D_MODEL and depth == 1 and bsz + 1 <= MOD_ROWS
    assert seq % TS_RET == 0 and seq % TM_IN == 0 and TM_IN % GRID_W == 0 and GRID_W % FFN_COLS == 0
    lc = ctx.shape[1]
    cos, sin = _rotary_tables(seq)
    x2 = x.reshape(bsz * seq, d)

    cin = jnp.zeros((MOD_ROWS, d), F32).at[:bsz].set(c).at[bsz].set(c_ctx)
    mod_all = _ada(cin, w_ada[0], b_ada).reshape(MOD_ROWS, N_MOD, d)
    pad = ((0, 0), (0, MOD_ROWS - N_MOD), (0, 0))
    modt = jnp.pad(mod_all[:bsz], pad)
    modc = jnp.pad(mod_all[bsz:bsz + 1], pad)[0]

    theta = jnp.concatenate([ret_decay_fw[0], ret_decay_bw[0]]).astype(F32)
    theta = jnp.broadcast_to(theta[:, None], (2 * RET_HEADS, RET_DV))
    dmat, rowf, rowb, cdt, ctxw = _tables(theta, lc)

    s_fw, s_bw = _ctx_states(ctx, modc, norm_mix_pre, w_in[0], ctxw)
    win = w_in[0].astype(BF16)

    yc, q, k, v, sg, sgb, ob = _inproj(x2, modt, norm_mix_pre, cos, sin, win,
                                       w_conv_out[0].astype(BF16), conv_w[0], conv_b,
                                       s_bw, rowb, cdt, bsz, seq)
    x1 = _ret_fw(q, k, v, sg, ob, yc, sgb, x2, s_fw, dmat, rowf, cdt, w_ret_out[0].astype(BF16),
                 w_o[0].astype(BF16), modt, norm_mix_post, bsz, seq)
    out = _ffn(x1.reshape(bsz, seq // GRID_W, GRID_W, d), modt, norm_ffn_pre, norm_ffn_post,
               w_ffn_up[0].astype(BF16), ffn_conv_w[0], ffn_conv_b, w_ffn_down[0].astype(BF16))
    return out.reshape(bsz, seq, d)
```

```python
import functools

import jax
import jax.numpy as jnp
import numpy as np
from jax import lax
from jax.experimental import pallas as pl
from jax.experimental.pallas import tpu as pltpu

F32 = jnp.float32
BF16 = jnp.bfloat16

D_MODEL = 1024
GRID_W = 64
CONV_W = 1024
RET_HEADS = 4
RET_DK = 256
RET_DV = 512
RET_QK = RET_HEADS * RET_DK
RET_V = RET_HEADS * RET_DV
ROPE_PAIRS = RET_DK // 4
ROPE_BASE = 10000.0
D_FF = 2816
N_MOD = 6
EPS = 1e-6

OFF_CX = 0
OFF_CB = OFF_CX + CONV_W
OFF_CC = OFF_CB + CONV_W
OFF_Q = OFF_CC + CONV_W
OFF_K = OFF_Q + RET_QK
OFF_V = OFF_K + RET_QK
OFF_G = OFF_V + RET_V
OFF_GA = OFF_G + RET_V
OFF_GB = OFF_GA + D_MODEL
IN_COLS = OFF_GB + D_MODEL

RET_C = 256
TM_IN = RET_C
TS_RET = 512
FFN_COLS = 8
FFN_BLOCKS = ((0, 1024), (1024, 1024), (2048, 768))
FFN_EPI_PIECES = 4
IN_W_CHUNKS = 22
CO_W_CHUNKS = 8
RET_W_CHUNKS = 8
FFN_W_CHUNKS = 11
ADA_BN = 1536
MOD_ROWS = 8
VMEM_LIMIT = 60 * 1024 * 1024


def _const_spec(shape):
    zeros = (0,) * len(shape)
    return pl.BlockSpec(shape, lambda *_: zeros, pipeline_mode=pl.Buffered(1))


def _chunk_spec(shape, n_chunks, axis):
    block = list(shape)
    block[axis] //= n_chunks
    assert block[axis] * n_chunks == shape[axis]

    def index_map(i):
        c = jnp.minimum(i, n_chunks - 1)
        return (c, 0) if axis == 0 else (0, c)

    return pl.BlockSpec(tuple(block), index_map)


def _cast_chunks(step, weights):
    for c in range(max(n for _, _, _, n in weights)):
        @pl.when(step == c)
        def _():
            for src_ref, dst_ref, axis, n in weights:
                if c < n:
                    size = src_ref.shape[axis]
                    if axis == 0:
                        dst_ref[c * size:(c + 1) * size, :] = src_ref[...].astype(BF16)
                    else:
                        dst_ref[:, c * size:(c + 1) * size] = src_ref[...].astype(BF16)


def _rms(xf):
    return xf * lax.rsqrt(jnp.mean(xf * xf, axis=-1, keepdims=True) + EPS)


def _sigmoid(x):
    return 1.0 / (1.0 + jnp.exp(-x))


def _dot(a, b):
    return jnp.dot(a, b, preferred_element_type=F32)


def _dot_nt(a, b):
    return lax.dot_general(a, b, (((1,), (1,)), ((), ())), preferred_element_type=F32)


def _dot_tn(a, b):
    return lax.dot_general(a, b, (((0,), (0,)), ((), ())), preferred_element_type=F32)


def _ada_kernel(c_ref, w_ref, b_ref, o_ref):
    cv = c_ref[...]
    s = cv * _sigmoid(cv)
    o_ref[...] = _dot(s.astype(BF16), w_ref[...].astype(BF16)) + b_ref[...]


def _ada(cin, w_ada, b_ada):
    ncol = w_ada.shape[1]
    return pl.pallas_call(
        _ada_kernel,
        name="ada",
        grid=(ncol // ADA_BN,),
        in_specs=[
            pl.BlockSpec((MOD_ROWS, D_MODEL), lambda j: (0, 0)),
            pl.BlockSpec((D_MODEL, ADA_BN), lambda j: (0, j)),
            pl.BlockSpec((1, ADA_BN), lambda j: (0, j)),
        ],
        out_specs=pl.BlockSpec((MOD_ROWS, ADA_BN), lambda j: (0, j)),
        out_shape=jax.ShapeDtypeStruct((MOD_ROWS, ncol), F32),
        compiler_params=pltpu.CompilerParams(dimension_semantics=("arbitrary",)),
    )(cin, w_ada, b_ada)


def _tables_kernel(theta_ref, dmat_ref, rowf_ref, rowb_ref, cdt_ref, ctxw_ref, *, ctx_len):
    th = theta_ref[...]
    lg = -(jnp.maximum(-th, 0.0) + jnp.log1p(jnp.exp(-jnp.abs(th))))
    n_i = lax.broadcasted_iota(jnp.int32, (RET_C, RET_C), 0)
    m_i = lax.broadcasted_iota(jnp.int32, (RET_C, RET_C), 1)
    diff = (n_i - m_i).astype(F32)
    idx = lax.broadcasted_iota(jnp.int32, (RET_C, RET_DK), 0).astype(F32)
    midx = lax.broadcasted_iota(jnp.int32, (ctx_len, RET_DK), 0).astype(F32)
    for h in range(RET_HEADS):
        lf = lg[h:h + 1, :]
        lb = lg[RET_HEADS + h:RET_HEADS + h + 1, :]
        lf_c, lb_c = lf[:, :RET_C], lb[:, :RET_C]
        dmat_ref[h] = jnp.exp(jnp.where(diff >= 0.0, diff * lf_c, -diff * lb_c))
        lf_k, lb_k = lf[:, :RET_DK], lb[:, :RET_DK]
        rowf_ref[h, 0] = jnp.exp((idx + 1.0) * lf_k).astype(BF16)
        rowf_ref[h, 1] = jnp.exp((RET_C - 1.0 - idx) * lf_k).astype(BF16)
        rowb_ref[h, 0] = jnp.exp((RET_C - idx) * lb_k).astype(BF16)
        rowb_ref[h, 1] = jnp.exp(idx * lb_k).astype(BF16)
        cdt_ref[h, 0] = jnp.exp(RET_C * lf)
        cdt_ref[h, 1] = jnp.exp(RET_C * lb)
        ctxw_ref[h, 0] = jnp.exp((ctx_len - 1.0 - midx) * lf_k)
        ctxw_ref[h, 1] = jnp.exp(midx * lb_k)


def _tables(theta, ctx_len):
    rows = jax.ShapeDtypeStruct((RET_HEADS, 2, RET_C, RET_DK), BF16)
    return pl.pallas_call(
        functools.partial(_tables_kernel, ctx_len=ctx_len),
        name="tables",
        out_shape=(
            jax.ShapeDtypeStruct((RET_HEADS, RET_C, RET_C), F32),
            rows, rows,
            jax.ShapeDtypeStruct((RET_HEADS, 2, 1, RET_DV), F32),
            jax.ShapeDtypeStruct((RET_HEADS, 2, ctx_len, RET_DK), F32),
        ),
    )(theta)


def _ctx_kernel(ctx_ref, modc_ref, gain_ref, wk_ref, wv0_ref, wv1_ref, ctxw_ref, sfw_ref, sbw_ref):
    mc = modc_ref[...]
    hc = (_rms(ctx_ref[0]) * gain_ref[...] * (1.0 + mc[1:2]) + mc[0:1]).astype(BF16)
    kv = jnp.concatenate([_dot(hc, w_ref[...].astype(BF16)) for w_ref in (wk_ref, wv0_ref, wv1_ref)],
                         axis=1)
    for h in range(RET_HEADS):
        kh = kv[:, h * RET_DK:(h + 1) * RET_DK] * RET_DK ** -0.5
        vh = kv[:, RET_QK + h * RET_DV:RET_QK + (h + 1) * RET_DV].astype(BF16)
        sfw_ref[0, h] = _dot_tn((kh * ctxw_ref[h, 0]).astype(BF16), vh)
        sbw_ref[0, h] = _dot_tn((kh * ctxw_ref[h, 1]).astype(BF16), vh)


def _ctx_states(ctx, modc, gain, w_in, ctxw):
    bsz, lc, _ = ctx.shape
    assert OFF_K % RET_QK == 0 and OFF_V % RET_QK == 0 and RET_V == 2 * RET_QK

    def wcols(off):
        return pl.BlockSpec((D_MODEL, RET_QK), lambda b: (0, off // RET_QK))

    state = jax.ShapeDtypeStruct((bsz, RET_HEADS, RET_DK, RET_DV), F32)
    sspec = pl.BlockSpec((1, RET_HEADS, RET_DK, RET_DV), lambda b: (b, 0, 0, 0))
    return pl.pallas_call(
        _ctx_kernel,
        name="ctx",
        grid=(bsz,),
        in_specs=[
            pl.BlockSpec((1, lc, D_MODEL), lambda b: (b, 0, 0)),
            pl.BlockSpec((MOD_ROWS, D_MODEL), lambda b: (0, 0)),
            pl.BlockSpec((1, D_MODEL), lambda b: (0, 0)),
            wcols(OFF_K), wcols(OFF_V), wcols(OFF_V + RET_QK),
            pl.BlockSpec(ctxw.shape, lambda b: (0, 0, 0, 0)),
        ],
        out_specs=(sspec, sspec),
        out_shape=(state, state),
        compiler_params=pltpu.CompilerParams(dimension_semantics=("arbitrary",),
                                             vmem_limit_bytes=VMEM_LIMIT),
    )(ctx, modc, gain, w_in, w_in, w_in, ctxw)


def _inproj_kernel(x_ref, mod_ref, gain_ref, cos_ref, sin_ref, win32_ref, wco32_ref, cw_ref, cb_ref,
                   s0_ref, rowb_ref, cdt_ref,
                   yc_ref, q_ref, k_ref, v_ref, sg_ref, sgb_ref, ob_ref, t_ref, win_ref, wco_ref,
                   *, tiles_per_seq):
    step = pl.program_id(0)
    _cast_chunks(step, ((win32_ref, win_ref, 1, IN_W_CHUNKS), (wco32_ref, wco_ref, 0, CO_W_CHUNKS)))

    @pl.when(step >= IN_W_CHUNKS)
    def _():
        _inproj_tile(x_ref, mod_ref, gain_ref, cos_ref, sin_ref, win_ref, wco_ref, cw_ref, cb_ref,
                     s0_ref, rowb_ref, cdt_ref, yc_ref, q_ref, k_ref, v_ref, sg_ref, sgb_ref, ob_ref,
                     t_ref, (step - IN_W_CHUNKS) % tiles_per_seq == 0)


def _inproj_tile(x_ref, mod_ref, gain_ref, cos_ref, sin_ref, win_ref, wco_ref, cw_ref, cb_ref,
                 s0_ref, rowb_ref, cdt_ref,
                 yc_ref, q_ref, k_ref, v_ref, sg_ref, sgb_ref, ob_ref, t_ref, first_tile):
    @pl.when(first_tile)
    def _():
        t_ref[...] = s0_ref[0]

    m = mod_ref[0]
    h = _rms(x_ref[...]) * gain_ref[...] * (1.0 + m[1:2]) + m[0:1]
    hb = h.astype(BF16)

    def proj(off, width):
        return _dot(hb, win_ref[:, off:off + width])

    u = proj(OFF_CC, CONV_W) * proj(OFF_CX, CONV_W)
    col = lax.broadcasted_iota(jnp.int32, (TM_IN, CONV_W), 0) % GRID_W
    prev = jnp.where(col == 0, 0.0, pltpu.roll(u, 1, 0))
    nxt = jnp.where(col == GRID_W - 1, 0.0, pltpu.roll(u, TM_IN - 1, 0))
    cw = cw_ref[...]
    conv = cw[0:1] * prev + cw[1:2] * u + cw[2:3] * nxt + cb_ref[...]
    z = (proj(OFF_CB, CONV_W) * conv).astype(BF16)
    yc_ref[...] = (_sigmoid(proj(OFF_GA, D_MODEL)) * _dot(z, wco_ref[...])).astype(BF16)
    sgb_ref[...] = _sigmoid(proj(OFF_GB, D_MODEL)).astype(BF16)

    cos = cos_ref[...]
    sin = sin_ref[...]
    half = RET_DK // 2
    for off, ref, scale in ((OFF_Q, q_ref, 1.0), (OFF_K, k_ref, RET_DK ** -0.5)):
        t = proj(off, RET_QK)
        for hh in range(RET_HEADS):
            lo = hh * RET_DK
            t1 = t[:, lo:lo + half]
            t2 = t[:, lo + half:lo + RET_DK]
            ref[:, lo:lo + half] = ((t1 * cos - t2 * sin) * scale).astype(BF16)
            ref[:, lo + half:lo + RET_DK] = ((t2 * cos + t1 * sin) * scale).astype(BF16)
    v_ref[...] = proj(OFF_V, RET_V).astype(BF16)
    g = proj(OFF_G, RET_V)
    sg_ref[...] = (g * _sigmoid(g)).astype(BF16)

    for hh in range(RET_HEADS):
        qs = q_ref[:, hh * RET_DK:(hh + 1) * RET_DK] * rowb_ref[hh, 0]
        ks = k_ref[:, hh * RET_DK:(hh + 1) * RET_DK] * rowb_ref[hh, 1]
        vh = v_ref[:, hh * RET_DV:(hh + 1) * RET_DV]
        t = t_ref[hh]
        ob_ref[:, hh * RET_DV:(hh + 1) * RET_DV] = _dot(qs, t.astype(BF16)).astype(BF16)
        t_ref[hh] = cdt_ref[hh, 1] * t + _dot_tn(ks, vh)


def _inproj(x2, modt, gain, cos, sin, win, wco, cw, cb, s_bw, rowb, cdt, bsz, seq):
    nt = seq // TM_IN

    def tile(i):
        t = jnp.maximum(i - IN_W_CHUNKS, 0)
        return t // nt, nt - 1 - t % nt

    def row_spec(width):
        return pl.BlockSpec((TM_IN, width), lambda i: (tile(i)[0] * nt + tile(i)[1], 0))

    def out(width):
        return jax.ShapeDtypeStruct((bsz * seq, width), BF16)

    rope = pl.BlockSpec((TM_IN, RET_DK // 2), lambda i: (tile(i)[1], 0))
    return pl.pallas_call(
        functools.partial(_inproj_kernel, tiles_per_seq=nt),
        name="inproj",
        grid=(IN_W_CHUNKS + bsz * nt,),
        in_specs=[
            row_spec(D_MODEL),
            pl.BlockSpec((1, MOD_ROWS, D_MODEL), lambda i: (tile(i)[0], 0, 0)),
            pl.BlockSpec((1, D_MODEL), lambda i: (0, 0)),
            rope, rope,
            _chunk_spec(win.shape, IN_W_CHUNKS, 1),
            _chunk_spec(wco.shape, CO_W_CHUNKS, 0),
            pl.BlockSpec(cw.shape, lambda i: (0, 0)),
            pl.BlockSpec((1, CONV_W), lambda i: (0, 0)),
            pl.BlockSpec((1, RET_HEADS, RET_DK, RET_DV), lambda i: (tile(i)[0], 0, 0, 0)),
            _const_spec(rowb.shape),
            _const_spec(cdt.shape),
        ],
        out_specs=(row_spec(D_MODEL), row_spec(RET_QK), row_spec(RET_QK), row_spec(RET_V),
                   row_spec(RET_V), row_spec(D_MODEL), row_spec(RET_V)),
        out_shape=(out(D_MODEL), out(RET_QK), out(RET_QK), out(RET_V), out(RET_V), out(D_MODEL),
                   out(RET_V)),
        scratch_shapes=[pltpu.VMEM((RET_HEADS, RET_DK, RET_DV), F32),
                        pltpu.VMEM(win.shape, BF16), pltpu.VMEM(wco.shape, BF16)],
        compiler_params=pltpu.CompilerParams(dimension_semantics=("arbitrary",),
                                             vmem_limit_bytes=VMEM_LIMIT),
    )(x2, modt, gain, cos, sin, win, wco, cw, cb, s_bw, rowb, cdt)


def _ret_fw_kernel(q_ref, k_ref, v_ref, sg_ref, ob_ref, yc_ref, sgb_ref, x_ref, s0_ref, dmat_ref,
                   rowf_ref, cdt_ref, wro32_ref, wo32_ref, mod_ref, gpost_ref,
                   x1_ref, s_ref, gat_ref, wro_ref, wo_ref, *, steps_per_seq):
    step = pl.program_id(0)
    _cast_chunks(step, ((wro32_ref, wro_ref, 0, RET_W_CHUNKS), (wo32_ref, wo_ref, 0, RET_W_CHUNKS)))

    @pl.when(step >= RET_W_CHUNKS)
    def _():
        _ret_fw_tile(q_ref, k_ref, v_ref, sg_ref, ob_ref, yc_ref, sgb_ref, x_ref, s0_ref, dmat_ref,
                     rowf_ref, cdt_ref, wro_ref, wo_ref, mod_ref, gpost_ref, x1_ref, s_ref, gat_ref,
                     (step - RET_W_CHUNKS) % steps_per_seq == 0)


def _ret_fw_tile(q_ref, k_ref, v_ref, sg_ref, ob_ref, yc_ref, sgb_ref, x_ref, s0_ref, dmat_ref,
                 rowf_ref, cdt_ref, wro_ref, wo_ref, mod_ref, gpost_ref,
                 x1_ref, s_ref, gat_ref, first_tile):
    @pl.when(first_tile)
    def _():
        s_ref[...] = s0_ref[0]

    for ci in range(TS_RET // RET_C):
        r0 = ci * RET_C
        for h in range(RET_HEADS):
            qb = q_ref[r0:r0 + RET_C, h * RET_DK:(h + 1) * RET_DK]
            kb = k_ref[r0:r0 + RET_C, h * RET_DK:(h + 1) * RET_DK]
            vh = v_ref[r0:r0 + RET_C, h * RET_DV:(h + 1) * RET_DV]
            scores = (_dot_nt(qb, kb) * dmat_ref[h]).astype(BF16)
            s = s_ref[h]
            o = (_dot(scores, vh) + _dot(qb * rowf_ref[h, 0], s.astype(BF16))
                 + ob_ref[r0:r0 + RET_C, h * RET_DV:(h + 1) * RET_DV].astype(F32))
            s_ref[h] = cdt_ref[h, 0] * s + _dot_tn(kb * rowf_ref[h, 1], vh)
            gate = sg_ref[r0:r0 + RET_C, h * RET_DV:(h + 1) * RET_DV].astype(F32)
            gat_ref[r0:r0 + RET_C, h * RET_DV:(h + 1) * RET_DV] = (gate * _rms(o)).astype(BF16)

    y_ret = _dot(gat_ref[...], wro_ref[...])
    merged = yc_ref[...].astype(F32) + sgb_ref[...].astype(F32) * y_ret
    y = _dot(merged.astype(BF16), wo_ref[...])
    x1_ref[...] = x_ref[...] + mod_ref[0][2:3] * (_rms(y) * gpost_ref[...])


def _ret_fw(q, k, v, sg, ob, yc, sgb, x2, s_fw, dmat, rowf, cdt, wro, wo, modt, gpost, bsz, seq):
    ns = seq // TS_RET

    def tile(i):
        return jnp.maximum(i - RET_W_CHUNKS, 0)

    def row_spec(width):
        return pl.BlockSpec((TS_RET, width), lambda i: (tile(i), 0))

    return pl.pallas_call(
        functools.partial(_ret_fw_kernel, steps_per_seq=ns),
        name="ret_fw",
        grid=(RET_W_CHUNKS + bsz * ns,),
        in_specs=[
            row_spec(RET_QK), row_spec(RET_QK), row_spec(RET_V), row_spec(RET_V), row_spec(RET_V),
            row_spec(D_MODEL), row_spec(D_MODEL), row_spec(D_MODEL),
            pl.BlockSpec((1, RET_HEADS, RET_DK, RET_DV), lambda i: (tile(i) // ns, 0, 0, 0)),
            _const_spec(dmat.shape), _const_spec(rowf.shape), _const_spec(cdt.shape),
            _chunk_spec(wro.shape, RET_W_CHUNKS, 0), _chunk_spec(wo.shape, RET_W_CHUNKS, 0),
            pl.BlockSpec((1, MOD_ROWS, D_MODEL), lambda i: (tile(i) // ns, 0, 0)),
            pl.BlockSpec((1, D_MODEL), lambda i: (0, 0)),
        ],
        out_specs=row_spec(D_MODEL),
        out_shape=jax.ShapeDtypeStruct((bsz * seq, D_MODEL), F32),
        scratch_shapes=[pltpu.VMEM((RET_HEADS, RET_DK, RET_DV), F32),
                        pltpu.VMEM((TS_RET, RET_V), BF16),
                        pltpu.VMEM(wro.shape, BF16), pltpu.VMEM(wo.shape, BF16)],
        compiler_params=pltpu.CompilerParams(dimension_semantics=("arbitrary",),
                                             vmem_limit_bytes=VMEM_LIMIT),
    )(q, k, v, sg, ob, yc, sgb, x2, s_fw, dmat, rowf, cdt, wro, wo, modt, gpost)


def _ffn_kernel(x1_ref, mod_ref, gpre_ref, gpost_ref, wup32_ref, cw_ref, cb_ref, wdn32_ref,
                out_ref, acc_ref, wup_ref, wdn_ref, *, grid_rows):
    step = pl.program_id(0)
    _cast_chunks(step, ((wup32_ref, wup_ref, 1, FFN_W_CHUNKS), (wdn32_ref, wdn_ref, 0, FFN_W_CHUNKS)))

    @pl.when(step >= FFN_W_CHUNKS)
    def _():
        _ffn_strip(x1_ref, mod_ref, gpre_ref, gpost_ref, wup_ref, cw_ref, cb_ref, wdn_ref,
                   out_ref, acc_ref, grid_rows)


def _ffn_strip(x1_ref, mod_ref, gpre_ref, gpost_ref, wup_ref, cw_ref, cb_ref, wdn_ref,
               out_ref, acc_ref, grid_rows):
    tm = grid_rows * FFN_COLS
    x1 = x1_ref[0].reshape(tm, D_MODEL)
    m = mod_ref[0]
    hb = (_rms(x1) * gpre_ref[...] * (1.0 + m[4:5]) + m[3:4]).astype(BF16)

    def conv(u, off, width):
        w = cw_ref[:, off:off + width]
        pad = jnp.zeros((FFN_COLS, width), F32)
        above = jnp.concatenate([pad, u[:tm - FFN_COLS]], axis=0)
        below = jnp.concatenate([u[FFN_COLS:], pad], axis=0)
        return w[0:1] * above + w[1:2] * u + w[2:3] * below + cb_ref[:, off:off + width]

    def up(c0, width):
        return (_dot(hb, wup_ref[:, c0:c0 + width]),
                _dot(hb, wup_ref[:, D_FF + c0:D_FF + c0 + width]))

    ahead = up(*FFN_BLOCKS[0])
    for j, (c0, width) in enumerate(FFN_BLOCKS):
        ug, uv = ahead
        if j + 1 < len(FFN_BLOCKS):
            ahead = up(*FFN_BLOCKS[j + 1])
        g = conv(ug, c0, width)
        val = conv(uv, D_FF + c0, width)
        a = (g * _sigmoid(g) * val).astype(BF16)
        if j == 0:
            acc_ref[...] = _dot(a, wdn_ref[c0:c0 + width, :])
        elif j + 1 < len(FFN_BLOCKS):
            acc_ref[...] += _dot(a, wdn_ref[c0:c0 + width, :])
        else:
            gr = grid_rows // FFN_EPI_PIECES
            for p in range(FFN_EPI_PIECES):
                rows = slice(p * gr * FFN_COLS, (p + 1) * gr * FFN_COLS)
                f = acc_ref[rows, :] + _dot(a[rows, :], wdn_ref[c0:c0 + width, :])
                res = x1[rows, :] + m[5:6] * (_rms(f) * gpost_ref[...])
                out_ref[0, p * gr:(p + 1) * gr] = res.reshape(gr, FFN_COLS, D_MODEL)


def _ffn(x1g, modt, gpre, gpost, wup, cw, cb, wdn):
    bsz, grid_rows, grid_w, _ = x1g.shape
    strips = grid_w // FFN_COLS

    def tile(i):
        return jnp.maximum(i - FFN_W_CHUNKS, 0)

    strip = pl.BlockSpec((1, grid_rows, FFN_COLS, D_MODEL),
                         lambda i: (tile(i) // strips, 0, tile(i) % strips, 0))
    vec = pl.BlockSpec((1, D_MODEL), lambda i: (0, 0))
    return pl.pallas_call(
        functools.partial(_ffn_kernel, grid_rows=grid_rows),
        name="ffn",
        grid=(FFN_W_CHUNKS + bsz * strips,),
        in_specs=[
            strip,
            pl.BlockSpec((1, MOD_ROWS, D_MODEL), lambda i: (tile(i) // strips, 0, 0)),
            vec, vec,
            _chunk_spec(wup.shape, FFN_W_CHUNKS, 1),
            pl.BlockSpec(cw.shape, lambda i: (0, 0)),
            pl.BlockSpec((1, 2 * D_FF), lambda i: (0, 0)),
            _chunk_spec(wdn.shape, FFN_W_CHUNKS, 0),
        ],
        out_specs=strip,
        out_shape=jax.ShapeDtypeStruct(x1g.shape, F32),
        scratch_shapes=[pltpu.VMEM((grid_rows * FFN_COLS, D_MODEL), F32),
                        pltpu.VMEM(wup.shape, BF16), pltpu.VMEM(wdn.shape, BF16)],
        compiler_params=pltpu.CompilerParams(dimension_semantics=("arbitrary",),
                                             vmem_limit_bytes=VMEM_LIMIT),
    )(x1g, modt, gpre, gpost, wup, cw, cb, wdn)


def _rotary_tables(n):
    t = np.arange(n)
    row = (t // GRID_W).astype(np.float64)
    col = (t % GRID_W).astype(np.float64)
    inv = ROPE_BASE ** (-np.arange(ROPE_PAIRS, dtype=np.float64) / ROPE_PAIRS)
    ang = np.concatenate([row[:, None] * inv, col[:, None] * inv], axis=-1)
    return jnp.asarray(np.cos(ang), F32), jnp.asarray(np.sin(ang), F32)


def kernel(x, c, ctx, c_ctx, w_ada, b_ada, norm_mix_pre, norm_mix_post, w_in, conv_w, conv_b,
           w_conv_out, ret_decay_fw, ret_decay_bw, w_ret_out, w_o, norm_ffn_pre, norm_ffn_post,
           w_ffn_up, ffn_conv_w, ffn_conv_b, w_ffn_down):
    bsz, seq, d = x.shape
    depth = w_ada.shape[0]
    assert d == D_MODEL and depth == 1 and bsz + 1 <= MOD_ROWS
    assert seq % TS_RET == 0 and seq % TM_IN == 0 and TM_IN % GRID_W == 0 and GRID_W % FFN_COLS == 0
    lc = ctx.shape[1]
    cos, sin = _rotary_tables(seq)
    x2 = x.reshape(bsz * seq, d)

    cin = jnp.zeros((MOD_ROWS, d), F32).at[:bsz].set(c).at[bsz].set(c_ctx)
    mod_all = _ada(cin, w_ada[0], b_ada).reshape(MOD_ROWS, N_MOD, d)
    pad = ((0, 0), (0, MOD_ROWS - N_MOD), (0, 0))
    modt = jnp.pad(mod_all[:bsz], pad)
    modc = jnp.pad(mod_all[bsz:bsz + 1], pad)[0]

    theta = jnp.concatenate([ret_decay_fw[0], ret_decay_bw[0]]).astype(F32)
    theta = jnp.broadcast_to(theta[:, None], (2 * RET_HEADS, RET_DV))
    dmat, rowf, rowb, cdt, ctxw = _tables(theta, lc)

    s_fw, s_bw = _ctx_states(ctx, modc, norm_mix_pre, w_in[0], ctxw)

    yc, q, k, v, sg, sgb, ob = _inproj(x2, modt, norm_mix_pre, cos, sin, w_in[0], w_conv_out[0],
                                       conv_w[0], conv_b, s_bw, rowb, cdt, bsz, seq)
    x1 = _ret_fw(q, k, v, sg, ob, yc, sgb, x2, s_fw, dmat, rowf, cdt, w_ret_out[0], w_o[0],
                 modt, norm_mix_post, bsz, seq)
    out = _ffn(x1.reshape(bsz, seq // GRID_W, GRID_W, d), modt, norm_ffn_pre, norm_ffn_post,
               w_ffn_up[0], ffn_conv_w[0], ffn_conv_b, w_ffn_down[0])
    return out.reshape(bsz, seq, d)
```

```python
import functools

import jax
import jax.numpy as jnp
import numpy as np
from jax import lax
from jax.experimental import pallas as pl
from jax.experimental.pallas import tpu as pltpu

F32 = jnp.float32
BF16 = jnp.bfloat16

D_MODEL = 1024
GRID_W = 64
CONV_W = 1024
RET_HEADS = 4
RET_DK = 256
RET_DV = 512
RET_QK = RET_HEADS * RET_DK
RET_V = RET_HEADS * RET_DV
ROPE_PAIRS = RET_DK // 4
ROPE_BASE = 10000.0
D_FF = 2816
N_MOD = 6
EPS = 1e-6

OFF_CX = 0
OFF_CB = OFF_CX + CONV_W
OFF_CC = OFF_CB + CONV_W
OFF_Q = OFF_CC + CONV_W
OFF_K = OFF_Q + RET_QK
OFF_V = OFF_K + RET_QK
OFF_G = OFF_V + RET_V
OFF_GA = OFF_G + RET_V
OFF_GB = OFF_GA + D_MODEL
IN_COLS = OFF_GB + D_MODEL

RET_C = 256
TM_IN = RET_C
TS_RET = 512
RET_EPI_PIECES = 2
FFN_COLS = 8
FFN_BLOCKS = ((0, 1024), (1024, 1024), (2048, 768))
FFN_EPI_PIECES = 4
IN_W_CHUNKS = 11
CO_W_CHUNKS = 8
RET_W_CHUNKS = 4
FFN_W_CHUNKS = 11
ADA_BN = 1536
MOD_ROWS = 8
VMEM_LIMIT = 60 * 1024 * 1024


def _const_spec(shape):
    zeros = (0,) * len(shape)
    return pl.BlockSpec(shape, lambda *_: zeros, pipeline_mode=pl.Buffered(1))


def _chunk_spec(shape, n_chunks, axis):
    block = list(shape)
    block[axis] //= n_chunks
    assert block[axis] * n_chunks == shape[axis]

    def index_map(i):
        c = jnp.minimum(i, n_chunks - 1)
        return (c, 0) if axis == 0 else (0, c)

    return pl.BlockSpec(tuple(block), index_map)


def _cast_chunks(step, weights):
    n_steps = max(n for _, _, _, n in weights)

    @pl.when(step < n_steps)
    def _():
        for c in range(n_steps):
            @pl.when(step == c)
            def _():
                for src_ref, dst_ref, axis, n in weights:
                    if c < n:
                        size = src_ref.shape[axis]
                        if axis == 0:
                            dst_ref[c * size:(c + 1) * size, :] = src_ref[...].astype(BF16)
                        else:
                            dst_ref[:, c * size:(c + 1) * size] = src_ref[...].astype(BF16)


def _rms(xf):
    return xf * lax.rsqrt(jnp.mean(xf * xf, axis=-1, keepdims=True) + EPS)


def _sigmoid(x):
    return 1.0 / (1.0 + jnp.exp(-x))


def _dot(a, b):
    return jnp.dot(a, b, preferred_element_type=F32)


def _dot_nt(a, b):
    return lax.dot_general(a, b, (((1,), (1,)), ((), ())), preferred_element_type=F32)


def _dot_tn(a, b):
    return lax.dot_general(a, b, (((0,), (0,)), ((), ())), preferred_element_type=F32)


def _ada_kernel(c_ref, cctx_ref, w_ref, b_ref, o_ref):
    bsz = c_ref.shape[0]
    cv = jnp.concatenate([c_ref[...], cctx_ref[...], jnp.zeros((MOD_ROWS - bsz - 1, D_MODEL), F32)], axis=0)
    s = cv * _sigmoid(cv)
    o_ref[...] = _dot(s.astype(BF16), w_ref[...].astype(BF16)) + b_ref[...]


def _ada(c, c_ctx, w_ada, b_ada):
    ncol = w_ada.shape[1]
    return pl.pallas_call(
        _ada_kernel,
        name="ada",
        grid=(ncol // ADA_BN,),
        in_specs=[
            pl.BlockSpec(c.shape, lambda j: (0, 0)),
            pl.BlockSpec((1, D_MODEL), lambda j: (0, 0)),
            pl.BlockSpec((D_MODEL, ADA_BN), lambda j: (0, j)),
            pl.BlockSpec((1, ADA_BN), lambda j: (0, j)),
        ],
        out_specs=pl.BlockSpec((MOD_ROWS, ADA_BN), lambda j: (0, j)),
        out_shape=jax.ShapeDtypeStruct((MOD_ROWS, ncol), F32),
        compiler_params=pltpu.CompilerParams(dimension_semantics=("arbitrary",)),
    )(c, c_ctx.reshape(1, D_MODEL), w_ada, b_ada)


def _tables_kernel(theta_ref, dmat_ref, rowf_ref, rowb_ref, cdt_ref, ctxw_ref, *, ctx_len):
    th = theta_ref[...]
    lg = -(jnp.maximum(-th, 0.0) + jnp.log1p(jnp.exp(-jnp.abs(th))))
    n_i = lax.broadcasted_iota(jnp.int32, (RET_C, RET_C), 0)
    m_i = lax.broadcasted_iota(jnp.int32, (RET_C, RET_C), 1)
    diff = (n_i - m_i).astype(F32)
    idx = lax.broadcasted_iota(jnp.int32, (RET_C, RET_DK), 0).astype(F32)
    midx = lax.broadcasted_iota(jnp.int32, (ctx_len, RET_DK), 0).astype(F32)
    for h in range(RET_HEADS):
        lf = lg[h:h + 1, :]
        lb = lg[RET_HEADS + h:RET_HEADS + h + 1, :]
        lf_c, lb_c = lf[:, :RET_C], lb[:, :RET_C]
        dmat_ref[h] = jnp.exp(jnp.where(diff >= 0.0, diff * lf_c, -diff * lb_c))
        lf_k, lb_k = lf[:, :RET_DK], lb[:, :RET_DK]
        rowf_ref[h, 0] = jnp.exp((idx + 1.0) * lf_k).astype(BF16)
        rowf_ref[h, 1] = jnp.exp((RET_C - 1.0 - idx) * lf_k).astype(BF16)
        rowb_ref[h, 0] = jnp.exp((RET_C - idx) * lb_k).astype(BF16)
        rowb_ref[h, 1] = jnp.exp(idx * lb_k).astype(BF16)
        cdt_ref[h, 0] = jnp.exp(RET_C * lf)
        cdt_ref[h, 1] = jnp.exp(RET_C * lb)
        ctxw_ref[h, 0] = jnp.exp((ctx_len - 1.0 - midx) * lf_k)
        ctxw_ref[h, 1] = jnp.exp(midx * lb_k)


def _tables(theta, ctx_len):
    rows = jax.ShapeDtypeStruct((RET_HEADS, 2, RET_C, RET_DK), BF16)
    return pl.pallas_call(
        functools.partial(_tables_kernel, ctx_len=ctx_len),
        name="tables",
        out_shape=(
            jax.ShapeDtypeStruct((RET_HEADS, RET_C, RET_C), F32),
            rows, rows,
            jax.ShapeDtypeStruct((RET_HEADS, 2, 1, RET_DV), F32),
            jax.ShapeDtypeStruct((RET_HEADS, 2, ctx_len, RET_DK), F32),
        ),
    )(theta)


def _ctx_kernel(ctx_ref, modc_ref, gain_ref, wk_ref, wv0_ref, wv1_ref, ctxw_ref, sfw_ref, sbw_ref):
    mc = modc_ref[0]
    hc = (_rms(ctx_ref[0]) * gain_ref[...] * (1.0 + mc[1:2]) + mc[0:1]).astype(BF16)
    kv = jnp.concatenate([_dot(hc, w_ref[...].astype(BF16)) for w_ref in (wk_ref, wv0_ref, wv1_ref)],
                         axis=1)
    for h in range(RET_HEADS):
        kh = kv[:, h * RET_DK:(h + 1) * RET_DK] * RET_DK ** -0.5
        vh = kv[:, RET_QK + h * RET_DV:RET_QK + (h + 1) * RET_DV].astype(BF16)
        sfw_ref[0, h] = _dot_tn((kh * ctxw_ref[h, 0]).astype(BF16), vh)
        sbw_ref[0, h] = _dot_tn((kh * ctxw_ref[h, 1]).astype(BF16), vh)


def _ctx_states(ctx, modt, gain, w_in, ctxw):
    bsz, lc, _ = ctx.shape
    assert OFF_K % RET_QK == 0 and OFF_V % RET_QK == 0 and RET_V == 2 * RET_QK

    def wcols(off):
        return pl.BlockSpec((D_MODEL, RET_QK), lambda b: (0, off // RET_QK))

    state = jax.ShapeDtypeStruct((bsz, RET_HEADS, RET_DK, RET_DV), F32)
    sspec = pl.BlockSpec((1, RET_HEADS, RET_DK, RET_DV), lambda b: (b, 0, 0, 0))
    return pl.pallas_call(
        _ctx_kernel,
        name="ctx",
        grid=(bsz,),
        in_specs=[
            pl.BlockSpec((1, lc, D_MODEL), lambda b: (b, 0, 0)),
            pl.BlockSpec((1, N_MOD, D_MODEL), lambda b: (bsz, 0, 0)),
            pl.BlockSpec((1, D_MODEL), lambda b: (0, 0)),
            wcols(OFF_K), wcols(OFF_V), wcols(OFF_V + RET_QK),
            pl.BlockSpec(ctxw.shape, lambda b: (0, 0, 0, 0)),
        ],
        out_specs=(sspec, sspec),
        out_shape=(state, state),
        compiler_params=pltpu.CompilerParams(dimension_semantics=("arbitrary",),
                                             vmem_limit_bytes=VMEM_LIMIT),
    )(ctx, modt, gain, w_in, w_in, w_in, ctxw)


def _inproj_kernel(x_ref, mod_ref, gain_ref, cos_ref, sin_ref, win32_ref, wco32_ref, cw_ref, cb_ref,
                   s0_ref, rowb_ref, cdt_ref,
                   yc_ref, q_ref, k_ref, v_ref, sg_ref, sgb_ref, ob_ref, t_ref, win_ref, wco_ref,
                   *, tiles_per_seq):
    step = pl.program_id(0)
    _cast_chunks(step, ((win32_ref, win_ref, 1, IN_W_CHUNKS), (wco32_ref, wco_ref, 0, CO_W_CHUNKS)))

    @pl.when(step >= IN_W_CHUNKS)
    def _():
        _inproj_tile(x_ref, mod_ref, gain_ref, cos_ref, sin_ref, win_ref, wco_ref, cw_ref, cb_ref,
                     s0_ref, rowb_ref, cdt_ref, yc_ref, q_ref, k_ref, v_ref, sg_ref, sgb_ref, ob_ref,
                     t_ref, (step - IN_W_CHUNKS) % tiles_per_seq == 0)


def _inproj_tile(x_ref, mod_ref, gain_ref, cos_ref, sin_ref, win_ref, wco_ref, cw_ref, cb_ref,
                 s0_ref, rowb_ref, cdt_ref,
                 yc_ref, q_ref, k_ref, v_ref, sg_ref, sgb_ref, ob_ref, t_ref, first_tile):
    @pl.when(first_tile)
    def _():
        t_ref[...] = s0_ref[0]

    m = mod_ref[0]
    h = _rms(x_ref[...]) * gain_ref[...] * (1.0 + m[1:2]) + m[0:1]
    hb = h.astype(BF16)

    def proj(off, width):
        return _dot(hb, win_ref[:, off:off + width])

    u = proj(OFF_CC, CONV_W) * proj(OFF_CX, CONV_W)
    col = lax.broadcasted_iota(jnp.int32, (TM_IN, CONV_W), 0) % GRID_W
    prev = jnp.where(col == 0, 0.0, pltpu.roll(u, 1, 0))
    nxt = jnp.where(col == GRID_W - 1, 0.0, pltpu.roll(u, TM_IN - 1, 0))
    cw = cw_ref[...]
    conv = cw[0:1] * prev + cw[1:2] * u + cw[2:3] * nxt + cb_ref[...]
    z = (proj(OFF_CB, CONV_W) * conv).astype(BF16)
    yc_ref[...] = (_sigmoid(proj(OFF_GA, D_MODEL)) * _dot(z, wco_ref[...])).astype(BF16)
    sgb_ref[...] = _sigmoid(proj(OFF_GB, D_MODEL)).astype(BF16)

    cos = cos_ref[...]
    sin = sin_ref[...]
    half = RET_DK // 2
    for off, ref, scale in ((OFF_Q, q_ref, 1.0), (OFF_K, k_ref, RET_DK ** -0.5)):
        t = proj(off, RET_QK)
        for hh in range(RET_HEADS):
            lo = hh * RET_DK
            t1 = t[:, lo:lo + half]
            t2 = t[:, lo + half:lo + RET_DK]
            ref[:, lo:lo + half] = ((t1 * cos - t2 * sin) * scale).astype(BF16)
            ref[:, lo + half:lo + RET_DK] = ((t2 * cos + t1 * sin) * scale).astype(BF16)
    v_ref[...] = proj(OFF_V, RET_V).astype(BF16)
    g = proj(OFF_G, RET_V)
    sg_ref[...] = (g * _sigmoid(g)).astype(BF16)

    for hh in range(RET_HEADS):
        qs = q_ref[:, hh * RET_DK:(hh + 1) * RET_DK] * rowb_ref[hh, 0]
        ks = k_ref[:, hh * RET_DK:(hh + 1) * RET_DK] * rowb_ref[hh, 1]
        vh = v_ref[:, hh * RET_DV:(hh + 1) * RET_DV]
        t = t_ref[hh]
        ob_ref[:, hh * RET_DV:(hh + 1) * RET_DV] = _dot(qs, t.astype(BF16)).astype(BF16)
        t_ref[hh] = cdt_ref[hh, 1] * t + _dot_tn(ks, vh)


def _inproj(x2, modt, gain, cos, sin, win, wco, cw, cb, s_bw, rowb, cdt, bsz, seq):
    nt = seq // TM_IN

    def tile(i):
        t = jnp.maximum(i - IN_W_CHUNKS, 0)
        return t // nt, nt - 1 - t % nt

    def row_spec(width):
        return pl.BlockSpec((TM_IN, width), lambda i: (tile(i)[0] * nt + tile(i)[1], 0))

    def out(width):
        return jax.ShapeDtypeStruct((bsz * seq, width), BF16)

    rope = pl.BlockSpec((TM_IN, RET_DK // 2), lambda i: (tile(i)[1], 0))
    return pl.pallas_call(
        functools.partial(_inproj_kernel, tiles_per_seq=nt),
        name="inproj",
        grid=(IN_W_CHUNKS + bsz * nt,),
        in_specs=[
            row_spec(D_MODEL),
            pl.BlockSpec((1, N_MOD, D_MODEL), lambda i: (tile(i)[0], 0, 0)),
            pl.BlockSpec((1, D_MODEL), lambda i: (0, 0)),
            rope, rope,
            _chunk_spec(win.shape, IN_W_CHUNKS, 1),
            _chunk_spec(wco.shape, CO_W_CHUNKS, 0),
            pl.BlockSpec(cw.shape, lambda i: (0, 0)),
            pl.BlockSpec((1, CONV_W), lambda i: (0, 0)),
            pl.BlockSpec((1, RET_HEADS, RET_DK, RET_DV), lambda i: (tile(i)[0], 0, 0, 0)),
            _const_spec(rowb.shape),
            _const_spec(cdt.shape),
        ],
        out_specs=(row_spec(D_MODEL), row_spec(RET_QK), row_spec(RET_QK), row_spec(RET_V),
                   row_spec(RET_V), row_spec(D_MODEL), row_spec(RET_V)),
        out_shape=(out(D_MODEL), out(RET_QK), out(RET_QK), out(RET_V), out(RET_V), out(D_MODEL),
                   out(RET_V)),
        scratch_shapes=[pltpu.VMEM((RET_HEADS, RET_DK, RET_DV), F32),
                        pltpu.VMEM(win.shape, BF16), pltpu.VMEM(wco.shape, BF16)],
        compiler_params=pltpu.CompilerParams(dimension_semantics=("arbitrary",),
                                             vmem_limit_bytes=VMEM_LIMIT),
    )(x2, modt, gain, cos, sin, win, wco, cw, cb, s_bw, rowb, cdt)


def _ret_fw_kernel(q_ref, k_ref, v_ref, sg_ref, ob_ref, yc_ref, sgb_ref, x_ref, s0_ref, dmat_ref,
                   rowf_ref, cdt_ref, wro32_ref, wo32_ref, mod_ref, gpost_ref,
                   x1_ref, s_ref, gat_ref, wro_ref, wo_ref, *, steps_per_seq):
    step = pl.program_id(0)
    _cast_chunks(step, ((wro32_ref, wro_ref, 0, RET_W_CHUNKS), (wo32_ref, wo_ref, 0, RET_W_CHUNKS)))

    @pl.when(step >= RET_W_CHUNKS)
    def _():
        _ret_fw_tile(q_ref, k_ref, v_ref, sg_ref, ob_ref, yc_ref, sgb_ref, x_ref, s0_ref, dmat_ref,
                     rowf_ref, cdt_ref, wro_ref, wo_ref, mod_ref, gpost_ref, x1_ref, s_ref, gat_ref,
                     (step - RET_W_CHUNKS) % steps_per_seq == 0)


def _ret_fw_tile(q_ref, k_ref, v_ref, sg_ref, ob_ref, yc_ref, sgb_ref, x_ref, s0_ref, dmat_ref,
                 rowf_ref, cdt_ref, wro_ref, wo_ref, mod_ref, gpost_ref,
                 x1_ref, s_ref, gat_ref, first_tile):
    @pl.when(first_tile)
    def _():
        s_ref[...] = s0_ref[0]

    for ci in range(TS_RET // RET_C):
        r0 = ci * RET_C
        for h in range(RET_HEADS):
            qb = q_ref[r0:r0 + RET_C, h * RET_DK:(h + 1) * RET_DK]
            kb = k_ref[r0:r0 + RET_C, h * RET_DK:(h + 1) * RET_DK]
            vh = v_ref[r0:r0 + RET_C, h * RET_DV:(h + 1) * RET_DV]
            scores = (_dot_nt(qb, kb) * dmat_ref[h]).astype(BF16)
            s = s_ref[h]
            o = (_dot(scores, vh) + _dot(qb * rowf_ref[h, 0], s.astype(BF16))
                 + ob_ref[r0:r0 + RET_C, h * RET_DV:(h + 1) * RET_DV].astype(F32))
            s_ref[h] = cdt_ref[h, 0] * s + _dot_tn(kb * rowf_ref[h, 1], vh)
            gate = sg_ref[r0:r0 + RET_C, h * RET_DV:(h + 1) * RET_DV].astype(F32)
            gat_ref[r0:r0 + RET_C, h * RET_DV:(h + 1) * RET_DV] = (gate * _rms(o)).astype(BF16)

    pr = TS_RET // RET_EPI_PIECES
    for p in range(RET_EPI_PIECES):
        rows = slice(p * pr, (p + 1) * pr)
        y_ret = _dot(gat_ref[rows, :], wro_ref[...])
        merged = yc_ref[rows, :].astype(F32) + sgb_ref[rows, :].astype(F32) * y_ret
        y = _dot(merged.astype(BF16), wo_ref[...])
        x1_ref[rows, :] = x_ref[rows, :] + mod_ref[0][2:3] * (_rms(y) * gpost_ref[...])


def _ret_fw(q, k, v, sg, ob, yc, sgb, x2, s_fw, dmat, rowf, cdt, wro, wo, modt, gpost, bsz, seq):
    ns = seq // TS_RET

    def tile(i):
        return jnp.maximum(i - RET_W_CHUNKS, 0)

    def row_spec(width):
        return pl.BlockSpec((TS_RET, width), lambda i: (tile(i), 0))

    return pl.pallas_call(
        functools.partial(_ret_fw_kernel, steps_per_seq=ns),
        name="ret_fw",
        grid=(RET_W_CHUNKS + bsz * ns,),
        in_specs=[
            row_spec(RET_QK), row_spec(RET_QK), row_spec(RET_V), row_spec(RET_V), row_spec(RET_V),
            row_spec(D_MODEL), row_spec(D_MODEL), row_spec(D_MODEL),
            pl.BlockSpec((1, RET_HEADS, RET_DK, RET_DV), lambda i: (tile(i) // ns, 0, 0, 0)),
            _const_spec(dmat.shape), _const_spec(rowf.shape), _const_spec(cdt.shape),
            _chunk_spec(wro.shape, RET_W_CHUNKS, 0), _chunk_spec(wo.shape, RET_W_CHUNKS, 0),
            pl.BlockSpec((1, N_MOD, D_MODEL), lambda i: (tile(i) // ns, 0, 0)),
            pl.BlockSpec((1, D_MODEL), lambda i: (0, 0)),
        ],
        out_specs=row_spec(D_MODEL),
        out_shape=jax.ShapeDtypeStruct((bsz * seq, D_MODEL), F32),
        scratch_shapes=[pltpu.VMEM((RET_HEADS, RET_DK, RET_DV), F32),
                        pltpu.VMEM((TS_RET, RET_V), BF16),
                        pltpu.VMEM(wro.shape, BF16), pltpu.VMEM(wo.shape, BF16)],
        compiler_params=pltpu.CompilerParams(dimension_semantics=("arbitrary",),
                                             vmem_limit_bytes=VMEM_LIMIT),
    )(q, k, v, sg, ob, yc, sgb, x2, s_fw, dmat, rowf, cdt, wro, wo, modt, gpost)


def _ffn_kernel(x1_ref, mod_ref, gpre_ref, gpost_ref, wup32_ref, cw_ref, cb_ref, wdn32_ref,
                out_ref, acc_ref, wup_ref, wdn_ref, *, grid_rows):
    step = pl.program_id(0)
    _cast_chunks(step, ((wup32_ref, wup_ref, 1, FFN_W_CHUNKS), (wdn32_ref, wdn_ref, 0, FFN_W_CHUNKS)))

    @pl.when(step >= FFN_W_CHUNKS)
    def _():
        _ffn_strip(x1_ref, mod_ref, gpre_ref, gpost_ref, wup_ref, cw_ref, cb_ref, wdn_ref,
                   out_ref, acc_ref, grid_rows)


def _ffn_strip(x1_ref, mod_ref, gpre_ref, gpost_ref, wup_ref, cw_ref, cb_ref, wdn_ref,
               out_ref, acc_ref, grid_rows):
    tm = grid_rows * FFN_COLS
    x1 = x1_ref[0].reshape(tm, D_MODEL)
    m = mod_ref[0]
    hb = (_rms(x1) * gpre_ref[...] * (1.0 + m[4:5]) + m[3:4]).astype(BF16)

    def conv(u, off, width):
        w = cw_ref[:, off:off + width]
        pad = jnp.zeros((FFN_COLS, width), F32)
        above = jnp.concatenate([pad, u[:tm - FFN_COLS]], axis=0)
        below = jnp.concatenate([u[FFN_COLS:], pad], axis=0)
        return w[0:1] * above + w[1:2] * u + w[2:3] * below + cb_ref[:, off:off + width]

    def up(c0, width):
        return (_dot(hb, wup_ref[:, c0:c0 + width]),
                _dot(hb, wup_ref[:, D_FF + c0:D_FF + c0 + width]))

    ahead = up(*FFN_BLOCKS[0])
    for j, (c0, width) in enumerate(FFN_BLOCKS):
        ug, uv = ahead
        if j + 1 < len(FFN_BLOCKS):
            ahead = up(*FFN_BLOCKS[j + 1])
        g = conv(ug, c0, width)
        val = conv(uv, D_FF + c0, width)
        a = (g * _sigmoid(g) * val).astype(BF16)
        if j == 0:
            acc_ref[...] = _dot(a, wdn_ref[c0:c0 + width, :])
        elif j + 1 < len(FFN_BLOCKS):
            acc_ref[...] += _dot(a, wdn_ref[c0:c0 + width, :])
        else:
            gr = grid_rows // FFN_EPI_PIECES
            for p in range(FFN_EPI_PIECES):
                rows = slice(p * gr * FFN_COLS, (p + 1) * gr * FFN_COLS)
                f = acc_ref[rows, :] + _dot(a[rows, :], wdn_ref[c0:c0 + width, :])
                res = x1[rows, :] + m[5:6] * (_rms(f) * gpost_ref[...])
                out_ref[0, p * gr:(p + 1) * gr] = res.reshape(gr, FFN_COLS, D_MODEL)


def _ffn(x1g, modt, gpre, gpost, wup, cw, cb, wdn):
    bsz, grid_rows, grid_w, _ = x1g.shape
    strips = grid_w // FFN_COLS

    def tile(i):
        return jnp.maximum(i - FFN_W_CHUNKS, 0)

    strip = pl.BlockSpec((1, grid_rows, FFN_COLS, D_MODEL),
                         lambda i: (tile(i) // strips, 0, tile(i) % strips, 0))
    vec = pl.BlockSpec((1, D_MODEL), lambda i: (0, 0))
    return pl.pallas_call(
        functools.partial(_ffn_kernel, grid_rows=grid_rows),
        name="ffn",
        grid=(FFN_W_CHUNKS + bsz * strips,),
        in_specs=[
            strip,
            pl.BlockSpec((1, N_MOD, D_MODEL), lambda i: (tile(i) // strips, 0, 0)),
            vec, vec,
            _chunk_spec(wup.shape, FFN_W_CHUNKS, 1),
            pl.BlockSpec(cw.shape, lambda i: (0, 0)),
            pl.BlockSpec((1, 2 * D_FF), lambda i: (0, 0)),
            _chunk_spec(wdn.shape, FFN_W_CHUNKS, 0),
        ],
        out_specs=strip,
        out_shape=jax.ShapeDtypeStruct(x1g.shape, F32),
        scratch_shapes=[pltpu.VMEM((grid_rows * FFN_COLS, D_MODEL), F32),
                        pltpu.VMEM(wup.shape, BF16), pltpu.VMEM(wdn.shape, BF16)],
        compiler_params=pltpu.CompilerParams(dimension_semantics=("arbitrary",),
                                             vmem_limit_bytes=VMEM_LIMIT),
    )(x1g, modt, gpre, gpost, wup, cw, cb, wdn)


def _rotary_tables(n):
    t = np.arange(n)
    row = (t // GRID_W).astype(np.float64)
    col = (t % GRID_W).astype(np.float64)
    inv = ROPE_BASE ** (-np.arange(ROPE_PAIRS, dtype=np.float64) / ROPE_PAIRS)
    ang = np.concatenate([row[:, None] * inv, col[:, None] * inv], axis=-1)
    return jnp.asarray(np.cos(ang), F32), jnp.asarray(np.sin(ang), F32)


def kernel(x, c, ctx, c_ctx, w_ada, b_ada, norm_mix_pre, norm_mix_post, w_in, conv_w, conv_b,
           w_conv_out, ret_decay_fw, ret_decay_bw, w_ret_out, w_o, norm_ffn_pre, norm_ffn_post,
           w_ffn_up, ffn_conv_w, ffn_conv_b, w_ffn_down):
    bsz, seq, d = x.shape
    depth = w_ada.shape[0]
    assert d == D_MODEL and depth == 1 and bsz + 1 <= MOD_ROWS
    assert seq % TS_RET == 0 and seq % TM_IN == 0 and TM_IN % GRID_W == 0 and GRID_W % FFN_COLS == 0
    lc = ctx.shape[1]
    cos, sin = _rotary_tables(seq)
    x2 = x.reshape(bsz * seq, d)

    modt = _ada(c, c_ctx, w_ada[0], b_ada).reshape(MOD_ROWS, N_MOD, d)

    theta = jnp.concatenate([ret_decay_fw[0], ret_decay_bw[0]]).astype(F32)
    theta = jnp.broadcast_to(theta[:, None], (2 * RET_HEADS, RET_DV))
    dmat, rowf, rowb, cdt, ctxw = _tables(theta, lc)

    s_fw, s_bw = _ctx_states(ctx, modt, norm_mix_pre, w_in[0], ctxw)

    yc, q, k, v, sg, sgb, ob = _inproj(x2, modt, norm_mix_pre, cos, sin, w_in[0], w_conv_out[0],
                                       conv_w[0], conv_b, s_bw, rowb, cdt, bsz, seq)
    x1 = _ret_fw(q, k, v, sg, ob, yc, sgb, x2, s_fw, dmat, rowf, cdt, w_ret_out[0], w_o[0],
                 modt, norm_mix_post, bsz, seq)
    out = _ffn(x1.reshape(bsz, seq // GRID_W, GRID_W, d), modt, norm_ffn_pre, norm_ffn_post,
               w_ffn_up[0], ffn_conv_w[0], ffn_conv_b, w_ffn_down[0])
    return out.reshape(bsz, seq, d)
```

```python
import functools

import jax
import jax.numpy as jnp
import numpy as np
from jax import lax
from jax.experimental import pallas as pl
from jax.experimental.pallas import tpu as pltpu

F32 = jnp.float32
BF16 = jnp.bfloat16

D_MODEL = 1024
GRID_W = 64
CONV_W = 1024
RET_HEADS = 4
RET_DK = 256
RET_DV = 512
RET_QK = RET_HEADS * RET_DK
RET_V = RET_HEADS * RET_DV
ROPE_PAIRS = RET_DK // 4
ROPE_BASE = 10000.0
D_FF = 2816
N_MOD = 6
EPS = 1e-6

OFF_CX = 0
OFF_CB = OFF_CX + CONV_W
OFF_CC = OFF_CB + CONV_W
OFF_Q = OFF_CC + CONV_W
OFF_K = OFF_Q + RET_QK
OFF_V = OFF_K + RET_QK
OFF_G = OFF_V + RET_V
OFF_GA = OFF_G + RET_V
OFF_GB = OFF_GA + D_MODEL
IN_COLS = OFF_GB + D_MODEL

RET_C = 256
TM_IN = RET_C
TS_RET = 512
RET_EPI_PIECES = 2
FFN_COLS = 8
FFN_BLOCKS = ((0, 1024), (1024, 1024), (2048, 768))
FFN_EPI_PIECES = 4
IN_W_CHUNKS = 11
CO_W_CHUNKS = 8
FFN_DN_PASS_STEPS = 16
ADA_BN = 1536
MOD_ROWS = 8
VMEM_LIMIT = 60 * 1024 * 1024


def _const_spec(shape):
    zeros = (0,) * len(shape)
    return pl.BlockSpec(shape, lambda *_: zeros, pipeline_mode=pl.Buffered(1))


def _chunk_spec(shape, n_chunks, axis):
    block = list(shape)
    block[axis] //= n_chunks
    assert block[axis] * n_chunks == shape[axis]

    def index_map(i):
        c = jnp.minimum(i, n_chunks - 1)
        return (c, 0) if axis == 0 else (0, c)

    return pl.BlockSpec(tuple(block), index_map)


def _pass_spec(shape, n_steps, first_step=0):
    rows = shape[0] // n_steps
    assert rows * n_steps == shape[0] and rows % 16 == 0
    return pl.BlockSpec((rows, shape[1]), lambda i: (jnp.clip(i - first_step, 0, n_steps - 1), 0))


def _cast_chunks(step, weights):
    n_steps = max(n for _, _, _, n in weights)

    @pl.when(step < n_steps)
    def _():
        for c in range(n_steps):
            @pl.when(step == c)
            def _():
                for src_ref, dst_ref, axis, n in weights:
                    if c < n:
                        size = src_ref.shape[axis]
                        if axis == 0:
                            dst_ref[c * size:(c + 1) * size, :] = src_ref[...].astype(BF16)
                        else:
                            dst_ref[:, c * size:(c + 1) * size] = src_ref[...].astype(BF16)


def _rms(xf):
    return xf * lax.rsqrt(jnp.mean(xf * xf, axis=-1, keepdims=True) + EPS)


def _sigmoid(x):
    return 1.0 / (1.0 + jnp.exp(-x))


def _dot(a, b):
    return jnp.dot(a, b, preferred_element_type=F32)


def _dot_nt(a, b):
    return lax.dot_general(a, b, (((1,), (1,)), ((), ())), preferred_element_type=F32)


def _dot_tn(a, b):
    return lax.dot_general(a, b, (((0,), (0,)), ((), ())), preferred_element_type=F32)


def _ada_kernel(c_ref, cctx_ref, w_ref, b_ref, o_ref):
    bsz = c_ref.shape[0]
    cv = jnp.concatenate([c_ref[...], cctx_ref[...], jnp.zeros((MOD_ROWS - bsz - 1, D_MODEL), F32)], axis=0)
    s = cv * _sigmoid(cv)
    o_ref[...] = _dot(s.astype(BF16), w_ref[...].astype(BF16)) + b_ref[...]


def _ada(c, c_ctx, w_ada, b_ada):
    ncol = w_ada.shape[1]
    return pl.pallas_call(
        _ada_kernel,
        name="ada",
        grid=(ncol // ADA_BN,),
        in_specs=[
            pl.BlockSpec(c.shape, lambda j: (0, 0)),
            pl.BlockSpec((1, D_MODEL), lambda j: (0, 0)),
            pl.BlockSpec((D_MODEL, ADA_BN), lambda j: (0, j)),
            pl.BlockSpec((1, ADA_BN), lambda j: (0, j)),
        ],
        out_specs=pl.BlockSpec((MOD_ROWS, ADA_BN), lambda j: (0, j)),
        out_shape=jax.ShapeDtypeStruct((MOD_ROWS, ncol), F32),
        compiler_params=pltpu.CompilerParams(dimension_semantics=("arbitrary",)),
    )(c, c_ctx.reshape(1, D_MODEL), w_ada, b_ada)


def _tables_kernel(theta_ref, dmat_ref, rowf_ref, rowb_ref, cdt_ref, ctxw_ref, *, ctx_len):
    th = theta_ref[...]
    lg = -(jnp.maximum(-th, 0.0) + jnp.log1p(jnp.exp(-jnp.abs(th))))
    n_i = lax.broadcasted_iota(jnp.int32, (RET_C, RET_C), 0)
    m_i = lax.broadcasted_iota(jnp.int32, (RET_C, RET_C), 1)
    diff = (n_i - m_i).astype(F32)
    idx = lax.broadcasted_iota(jnp.int32, (RET_C, RET_DK), 0).astype(F32)
    midx = lax.broadcasted_iota(jnp.int32, (ctx_len, RET_DK), 0).astype(F32)
    for h in range(RET_HEADS):
        lf = lg[h:h + 1, :]
        lb = lg[RET_HEADS + h:RET_HEADS + h + 1, :]
        lf_c, lb_c = lf[:, :RET_C], lb[:, :RET_C]
        dmat_ref[h] = jnp.exp(jnp.where(diff >= 0.0, diff * lf_c, -diff * lb_c))
        lf_k, lb_k = lf[:, :RET_DK], lb[:, :RET_DK]
        rowf_ref[h, 0] = jnp.exp((idx + 1.0) * lf_k).astype(BF16)
        rowf_ref[h, 1] = jnp.exp((RET_C - 1.0 - idx) * lf_k).astype(BF16)
        rowb_ref[h, 0] = jnp.exp((RET_C - idx) * lb_k).astype(BF16)
        rowb_ref[h, 1] = jnp.exp(idx * lb_k).astype(BF16)
        cdt_ref[h, 0] = jnp.exp(RET_C * lf)
        cdt_ref[h, 1] = jnp.exp(RET_C * lb)
        ctxw_ref[h, 0] = jnp.exp((ctx_len - 1.0 - midx) * lf_k)
        ctxw_ref[h, 1] = jnp.exp(midx * lb_k)


def _tables(theta, ctx_len):
    rows = jax.ShapeDtypeStruct((RET_HEADS, 2, RET_C, RET_DK), BF16)
    return pl.pallas_call(
        functools.partial(_tables_kernel, ctx_len=ctx_len),
        name="tables",
        out_shape=(
            jax.ShapeDtypeStruct((RET_HEADS, RET_C, RET_C), F32),
            rows, rows,
            jax.ShapeDtypeStruct((RET_HEADS, 2, 1, RET_DV), F32),
            jax.ShapeDtypeStruct((RET_HEADS, 2, ctx_len, RET_DK), F32),
        ),
    )(theta)


def _ctx_kernel(ctx_ref, modc_ref, gain_ref, wk_ref, wv0_ref, wv1_ref, ctxw_ref, sfw_ref, sbw_ref):
    mc = modc_ref[0]
    hc = (_rms(ctx_ref[0]) * gain_ref[...] * (1.0 + mc[1:2]) + mc[0:1]).astype(BF16)
    kv = jnp.concatenate([_dot(hc, w_ref[...].astype(BF16)) for w_ref in (wk_ref, wv0_ref, wv1_ref)],
                         axis=1)
    for h in range(RET_HEADS):
        kh = kv[:, h * RET_DK:(h + 1) * RET_DK] * RET_DK ** -0.5
        vh = kv[:, RET_QK + h * RET_DV:RET_QK + (h + 1) * RET_DV].astype(BF16)
        sfw_ref[0, h] = _dot_tn((kh * ctxw_ref[h, 0]).astype(BF16), vh)
        sbw_ref[0, h] = _dot_tn((kh * ctxw_ref[h, 1]).astype(BF16), vh)


def _ctx_states(ctx, modt, gain, w_in, ctxw):
    bsz, lc, _ = ctx.shape
    assert OFF_K % RET_QK == 0 and OFF_V % RET_QK == 0 and RET_V == 2 * RET_QK

    def wcols(off):
        return pl.BlockSpec((D_MODEL, RET_QK), lambda b: (0, off // RET_QK))

    state = jax.ShapeDtypeStruct((bsz, RET_HEADS, RET_DK, RET_DV), F32)
    sspec = pl.BlockSpec((1, RET_HEADS, RET_DK, RET_DV), lambda b: (b, 0, 0, 0))
    return pl.pallas_call(
        _ctx_kernel,
        name="ctx",
        grid=(bsz,),
        in_specs=[
            pl.BlockSpec((1, lc, D_MODEL), lambda b: (b, 0, 0)),
            pl.BlockSpec((1, N_MOD, D_MODEL), lambda b: (bsz, 0, 0)),
            pl.BlockSpec((1, D_MODEL), lambda b: (0, 0)),
            wcols(OFF_K), wcols(OFF_V), wcols(OFF_V + RET_QK),
            pl.BlockSpec(ctxw.shape, lambda b: (0, 0, 0, 0)),
        ],
        out_specs=(sspec, sspec),
        out_shape=(state, state),
        compiler_params=pltpu.CompilerParams(dimension_semantics=("arbitrary",),
                                             vmem_limit_bytes=VMEM_LIMIT),
    )(ctx, modt, gain, w_in, w_in, w_in, ctxw)


def _inproj_kernel(x_ref, mod_ref, gain_ref, cos_ref, sin_ref, win32_ref, wco32_ref, cw_ref, cb_ref,
                   s0_ref, rowb_ref, cdt_ref, wro32_ref, wo32_ref,
                   yc_ref, q_ref, k_ref, v_ref, sg_ref, sgb_ref, ob_ref, wro_out_ref, wo_out_ref,
                   t_ref, win_ref, wco_ref, *, tiles_per_seq):
    step = pl.program_id(0)
    _cast_chunks(step, ((win32_ref, win_ref, 1, IN_W_CHUNKS), (wco32_ref, wco_ref, 0, CO_W_CHUNKS)))

    @pl.when(step >= IN_W_CHUNKS)
    def _():
        wro_out_ref[...] = wro32_ref[...].astype(BF16)
        wo_out_ref[...] = wo32_ref[...].astype(BF16)
        _inproj_tile(x_ref, mod_ref, gain_ref, cos_ref, sin_ref, win_ref, wco_ref, cw_ref, cb_ref,
                     s0_ref, rowb_ref, cdt_ref, yc_ref, q_ref, k_ref, v_ref, sg_ref, sgb_ref, ob_ref,
                     t_ref, (step - IN_W_CHUNKS) % tiles_per_seq == 0)


def _inproj_tile(x_ref, mod_ref, gain_ref, cos_ref, sin_ref, win_ref, wco_ref, cw_ref, cb_ref,
                 s0_ref, rowb_ref, cdt_ref,
                 yc_ref, q_ref, k_ref, v_ref, sg_ref, sgb_ref, ob_ref, t_ref, first_tile):
    @pl.when(first_tile)
    def _():
        t_ref[...] = s0_ref[0]

    m = mod_ref[0]
    h = _rms(x_ref[...]) * gain_ref[...] * (1.0 + m[1:2]) + m[0:1]
    hb = h.astype(BF16)

    def proj(off, width):
        return _dot(hb, win_ref[:, off:off + width])

    u = proj(OFF_CC, CONV_W) * proj(OFF_CX, CONV_W)
    col = lax.broadcasted_iota(jnp.int32, (TM_IN, CONV_W), 0) % GRID_W
    prev = jnp.where(col == 0, 0.0, pltpu.roll(u, 1, 0))
    nxt = jnp.where(col == GRID_W - 1, 0.0, pltpu.roll(u, TM_IN - 1, 0))
    cw = cw_ref[...]
    conv = cw[0:1] * prev + cw[1:2] * u + cw[2:3] * nxt + cb_ref[...]
    z = (proj(OFF_CB, CONV_W) * conv).astype(BF16)
    yc_ref[...] = (_sigmoid(proj(OFF_GA, D_MODEL)) * _dot(z, wco_ref[...])).astype(BF16)
    sgb_ref[...] = _sigmoid(proj(OFF_GB, D_MODEL)).astype(BF16)

    cos = cos_ref[...]
    sin = sin_ref[...]
    half = RET_DK // 2
    for off, ref, scale in ((OFF_Q, q_ref, 1.0), (OFF_K, k_ref, RET_DK ** -0.5)):
        t = proj(off, RET_QK)
        for hh in range(RET_HEADS):
            lo = hh * RET_DK
            t1 = t[:, lo:lo + half]
            t2 = t[:, lo + half:lo + RET_DK]
            ref[:, lo:lo + half] = ((t1 * cos - t2 * sin) * scale).astype(BF16)
            ref[:, lo + half:lo + RET_DK] = ((t2 * cos + t1 * sin) * scale).astype(BF16)
    v_ref[...] = proj(OFF_V, RET_V).astype(BF16)
    g = proj(OFF_G, RET_V)
    sg_ref[...] = (g * _sigmoid(g)).astype(BF16)

    for hh in range(RET_HEADS):
        qs = q_ref[:, hh * RET_DK:(hh + 1) * RET_DK] * rowb_ref[hh, 0]
        ks = k_ref[:, hh * RET_DK:(hh + 1) * RET_DK] * rowb_ref[hh, 1]
        vh = v_ref[:, hh * RET_DV:(hh + 1) * RET_DV]
        t = t_ref[hh]
        ob_ref[:, hh * RET_DV:(hh + 1) * RET_DV] = _dot(qs, t.astype(BF16)).astype(BF16)
        t_ref[hh] = cdt_ref[hh, 1] * t + _dot_tn(ks, vh)


def _inproj(x2, modt, gain, cos, sin, win, wco, cw, cb, s_bw, rowb, cdt, wro, wo, bsz, seq):
    nt = seq // TM_IN
    passed = [_pass_spec(w.shape, bsz * nt, IN_W_CHUNKS) for w in (wro, wo)]

    def tile(i):
        t = jnp.maximum(i - IN_W_CHUNKS, 0)
        return t // nt, nt - 1 - t % nt

    def row_spec(width):
        return pl.BlockSpec((TM_IN, width), lambda i: (tile(i)[0] * nt + tile(i)[1], 0))

    def out(width):
        return jax.ShapeDtypeStruct((bsz * seq, width), BF16)

    rope = pl.BlockSpec((TM_IN, RET_DK // 2), lambda i: (tile(i)[1], 0))
    return pl.pallas_call(
        functools.partial(_inproj_kernel, tiles_per_seq=nt),
        name="inproj",
        grid=(IN_W_CHUNKS + bsz * nt,),
        in_specs=[
            row_spec(D_MODEL),
            pl.BlockSpec((1, N_MOD, D_MODEL), lambda i: (tile(i)[0], 0, 0)),
            pl.BlockSpec((1, D_MODEL), lambda i: (0, 0)),
            rope, rope,
            _chunk_spec(win.shape, IN_W_CHUNKS, 1),
            _chunk_spec(wco.shape, CO_W_CHUNKS, 0),
            pl.BlockSpec(cw.shape, lambda i: (0, 0)),
            pl.BlockSpec((1, CONV_W), lambda i: (0, 0)),
            pl.BlockSpec((1, RET_HEADS, RET_DK, RET_DV), lambda i: (tile(i)[0], 0, 0, 0)),
            _const_spec(rowb.shape),
            _const_spec(cdt.shape),
            *passed,
        ],
        out_specs=(row_spec(D_MODEL), row_spec(RET_QK), row_spec(RET_QK), row_spec(RET_V),
                   row_spec(RET_V), row_spec(D_MODEL), row_spec(RET_V), *passed),
        out_shape=(out(D_MODEL), out(RET_QK), out(RET_QK), out(RET_V), out(RET_V), out(D_MODEL),
                   out(RET_V), jax.ShapeDtypeStruct(wro.shape, BF16), jax.ShapeDtypeStruct(wo.shape, BF16)),
        scratch_shapes=[pltpu.VMEM((RET_HEADS, RET_DK, RET_DV), F32),
                        pltpu.VMEM(win.shape, BF16), pltpu.VMEM(wco.shape, BF16)],
        compiler_params=pltpu.CompilerParams(dimension_semantics=("arbitrary",),
                                             vmem_limit_bytes=VMEM_LIMIT),
    )(x2, modt, gain, cos, sin, win, wco, cw, cb, s_bw, rowb, cdt, wro, wo)


def _ret_fw_kernel(q_ref, k_ref, v_ref, sg_ref, ob_ref, yc_ref, sgb_ref, x_ref, s0_ref, dmat_ref,
                   rowf_ref, cdt_ref, wro_ref, wo_ref, mod_ref, gpost_ref, wup32_ref, wdn32_ref,
                   x1_ref, wup_out_ref, wdn_out_ref, s_ref, gat_ref, *, steps_per_seq):
    wup_out_ref[...] = wup32_ref[...].astype(BF16)
    wdn_out_ref[...] = wdn32_ref[...].astype(BF16)
    _ret_fw_tile(q_ref, k_ref, v_ref, sg_ref, ob_ref, yc_ref, sgb_ref, x_ref, s0_ref, dmat_ref,
                 rowf_ref, cdt_ref, wro_ref, wo_ref, mod_ref, gpost_ref, x1_ref, s_ref, gat_ref,
                 pl.program_id(0) % steps_per_seq == 0)


def _ret_fw_tile(q_ref, k_ref, v_ref, sg_ref, ob_ref, yc_ref, sgb_ref, x_ref, s0_ref, dmat_ref,
                 rowf_ref, cdt_ref, wro_ref, wo_ref, mod_ref, gpost_ref,
                 x1_ref, s_ref, gat_ref, first_tile):
    @pl.when(first_tile)
    def _():
        s_ref[...] = s0_ref[0]

    for ci in range(TS_RET // RET_C):
        r0 = ci * RET_C
        for h in range(RET_HEADS):
            qb = q_ref[r0:r0 + RET_C, h * RET_DK:(h + 1) * RET_DK]
            kb = k_ref[r0:r0 + RET_C, h * RET_DK:(h + 1) * RET_DK]
            vh = v_ref[r0:r0 + RET_C, h * RET_DV:(h + 1) * RET_DV]
            scores = (_dot_nt(qb, kb) * dmat_ref[h]).astype(BF16)
            s = s_ref[h]
            o = (_dot(scores, vh) + _dot(qb * rowf_ref[h, 0], s.astype(BF16))
                 + ob_ref[r0:r0 + RET_C, h * RET_DV:(h + 1) * RET_DV].astype(F32))
            s_ref[h] = cdt_ref[h, 0] * s + _dot_tn(kb * rowf_ref[h, 1], vh)
            gate = sg_ref[r0:r0 + RET_C, h * RET_DV:(h + 1) * RET_DV].astype(F32)
            gat_ref[r0:r0 + RET_C, h * RET_DV:(h + 1) * RET_DV] = (gate * _rms(o)).astype(BF16)

    pr = TS_RET // RET_EPI_PIECES
    for p in range(RET_EPI_PIECES):
        rows = slice(p * pr, (p + 1) * pr)
        y_ret = _dot(gat_ref[rows, :], wro_ref[...])
        merged = yc_ref[rows, :].astype(F32) + sgb_ref[rows, :].astype(F32) * y_ret
        y = _dot(merged.astype(BF16), wo_ref[...])
        x1_ref[rows, :] = x_ref[rows, :] + mod_ref[0][2:3] * (_rms(y) * gpost_ref[...])


def _ret_fw(q, k, v, sg, ob, yc, sgb, x2, s_fw, dmat, rowf, cdt, wro, wo, modt, gpost, wup, wdn,
            bsz, seq):
    ns = seq // TS_RET
    passed = [_pass_spec(wup.shape, bsz * ns), _pass_spec(wdn.shape, FFN_DN_PASS_STEPS)]

    def row_spec(width):
        return pl.BlockSpec((TS_RET, width), lambda i: (i, 0))

    return pl.pallas_call(
        functools.partial(_ret_fw_kernel, steps_per_seq=ns),
        name="ret_fw",
        grid=(bsz * ns,),
        in_specs=[
            row_spec(RET_QK), row_spec(RET_QK), row_spec(RET_V), row_spec(RET_V), row_spec(RET_V),
            row_spec(D_MODEL), row_spec(D_MODEL), row_spec(D_MODEL),
            pl.BlockSpec((1, RET_HEADS, RET_DK, RET_DV), lambda i: (i // ns, 0, 0, 0)),
            _const_spec(dmat.shape), _const_spec(rowf.shape), _const_spec(cdt.shape),
            _const_spec(wro.shape), _const_spec(wo.shape),
            pl.BlockSpec((1, N_MOD, D_MODEL), lambda i: (i // ns, 0, 0)),
            pl.BlockSpec((1, D_MODEL), lambda i: (0, 0)),
            *passed,
        ],
        out_specs=(row_spec(D_MODEL), *passed),
        out_shape=(jax.ShapeDtypeStruct((bsz * seq, D_MODEL), F32),
                   jax.ShapeDtypeStruct(wup.shape, BF16), jax.ShapeDtypeStruct(wdn.shape, BF16)),
        scratch_shapes=[pltpu.VMEM((RET_HEADS, RET_DK, RET_DV), F32),
                        pltpu.VMEM((TS_RET, RET_V), BF16)],
        compiler_params=pltpu.CompilerParams(dimension_semantics=("arbitrary",),
                                             vmem_limit_bytes=VMEM_LIMIT),
    )(q, k, v, sg, ob, yc, sgb, x2, s_fw, dmat, rowf, cdt, wro, wo, modt, gpost, wup, wdn)


def _ffn_kernel(x1_ref, mod_ref, gpre_ref, gpost_ref, wup_ref, cw_ref, cb_ref, wdn_ref,
                out_ref, acc_ref, *, grid_rows):
    tm = grid_rows * FFN_COLS
    x1 = x1_ref[0].reshape(tm, D_MODEL)
    m = mod_ref[0]
    hb = (_rms(x1) * gpre_ref[...] * (1.0 + m[4:5]) + m[3:4]).astype(BF16)

    def conv(u, off, width):
        w = cw_ref[:, off:off + width]
        pad = jnp.zeros((FFN_COLS, width), F32)
        above = jnp.concatenate([pad, u[:tm - FFN_COLS]], axis=0)
        below = jnp.concatenate([u[FFN_COLS:], pad], axis=0)
        return w[0:1] * above + w[1:2] * u + w[2:3] * below + cb_ref[:, off:off + width]

    def up(c0, width):
        return (_dot(hb, wup_ref[:, c0:c0 + width]),
                _dot(hb, wup_ref[:, D_FF + c0:D_FF + c0 + width]))

    ahead = up(*FFN_BLOCKS[0])
    for j, (c0, width) in enumerate(FFN_BLOCKS):
        ug, uv = ahead
        if j + 1 < len(FFN_BLOCKS):
            ahead = up(*FFN_BLOCKS[j + 1])
        g = conv(ug, c0, width)
        val = conv(uv, D_FF + c0, width)
        a = (g * _sigmoid(g) * val).astype(BF16)
        if j == 0:
            acc_ref[...] = _dot(a, wdn_ref[c0:c0 + width, :])
        elif j + 1 < len(FFN_BLOCKS):
            acc_ref[...] += _dot(a, wdn_ref[c0:c0 + width, :])
        else:
            gr = grid_rows // FFN_EPI_PIECES
            for p in range(FFN_EPI_PIECES):
                rows = slice(p * gr * FFN_COLS, (p + 1) * gr * FFN_COLS)
                f = acc_ref[rows, :] + _dot(a[rows, :], wdn_ref[c0:c0 + width, :])
                res = x1[rows, :] + m[5:6] * (_rms(f) * gpost_ref[...])
                out_ref[0, p * gr:(p + 1) * gr] = res.reshape(gr, FFN_COLS, D_MODEL)


def _ffn(x1g, modt, gpre, gpost, wup, cw, cb, wdn):
    bsz, grid_rows, grid_w, _ = x1g.shape
    strips = grid_w // FFN_COLS
    strip = pl.BlockSpec((1, grid_rows, FFN_COLS, D_MODEL), lambda i: (i // strips, 0, i % strips, 0))
    vec = pl.BlockSpec((1, D_MODEL), lambda i: (0, 0))
    return pl.pallas_call(
        functools.partial(_ffn_kernel, grid_rows=grid_rows),
        name="ffn",
        grid=(bsz * strips,),
        in_specs=[
            strip,
            pl.BlockSpec((1, N_MOD, D_MODEL), lambda i: (i // strips, 0, 0)),
            vec, vec,
            _const_spec(wup.shape),
            pl.BlockSpec(cw.shape, lambda i: (0, 0)),
            pl.BlockSpec((1, 2 * D_FF), lambda i: (0, 0)),
            _const_spec(wdn.shape),
        ],
        out_specs=strip,
        out_shape=jax.ShapeDtypeStruct(x1g.shape, F32),
        scratch_shapes=[pltpu.VMEM((grid_rows * FFN_COLS, D_MODEL), F32)],
        compiler_params=pltpu.CompilerParams(dimension_semantics=("arbitrary",),
                                             vmem_limit_bytes=VMEM_LIMIT),
    )(x1g, modt, gpre, gpost, wup, cw, cb, wdn)


def _rotary_tables(n):
    t = np.arange(n)
    row = (t // GRID_W).astype(np.float64)
    col = (t % GRID_W).astype(np.float64)
    inv = ROPE_BASE ** (-np.arange(ROPE_PAIRS, dtype=np.float64) / ROPE_PAIRS)
    ang = np.concatenate([row[:, None] * inv, col[:, None] * inv], axis=-1)
    return jnp.asarray(np.cos(ang), F32), jnp.asarray(np.sin(ang), F32)


def kernel(x, c, ctx, c_ctx, w_ada, b_ada, norm_mix_pre, norm_mix_post, w_in, conv_w, conv_b,
           w_conv_out, ret_decay_fw, ret_decay_bw, w_ret_out, w_o, norm_ffn_pre, norm_ffn_post,
           w_ffn_up, ffn_conv_w, ffn_conv_b, w_ffn_down):
    bsz, seq, d = x.shape
    depth = w_ada.shape[0]
    assert d == D_MODEL and depth == 1 and bsz + 1 <= MOD_ROWS
    assert seq % TS_RET == 0 and seq % TM_IN == 0 and TM_IN % GRID_W == 0 and GRID_W % FFN_COLS == 0
    lc = ctx.shape[1]
    cos, sin = _rotary_tables(seq)
    x2 = x.reshape(bsz * seq, d)

    modt = _ada(c, c_ctx, w_ada[0], b_ada).reshape(MOD_ROWS, N_MOD, d)

    theta = jnp.concatenate([ret_decay_fw[0], ret_decay_bw[0]]).astype(F32)
    theta = jnp.broadcast_to(theta[:, None], (2 * RET_HEADS, RET_DV))
    dmat, rowf, rowb, cdt, ctxw = _tables(theta, lc)

    s_fw, s_bw = _ctx_states(ctx, modt, norm_mix_pre, w_in[0], ctxw)

    yc, q, k, v, sg, sgb, ob, wro, wo = _inproj(x2, modt, norm_mix_pre, cos, sin, w_in[0], w_conv_out[0],
                                                conv_w[0], conv_b, s_bw, rowb, cdt, w_ret_out[0], w_o[0],
                                                bsz, seq)
    x1, wup, wdn = _ret_fw(q, k, v, sg, ob, yc, sgb, x2, s_fw, dmat, rowf, cdt, wro, wo, modt,
                           norm_mix_post, w_ffn_up[0], w_ffn_down[0], bsz, seq)
    out = _ffn(x1.reshape(bsz, seq // GRID_W, GRID_W, d), modt, norm_ffn_pre, norm_ffn_post,
               wup, ffn_conv_w[0], ffn_conv_b, wdn)
    return out.reshape(bsz, seq, d)
```

```python
import functools

import jax
import jax.numpy as jnp
import numpy as np
from jax import lax
from jax.experimental import pallas as pl
from jax.experimental.pallas import tpu as pltpu

F32 = jnp.float32
BF16 = jnp.bfloat16

D_MODEL = 1024
GRID_W = 64
CONV_W = 1024
RET_HEADS = 4
RET_DK = 256
RET_DV = 512
RET_QK = RET_HEADS * RET_DK
RET_V = RET_HEADS * RET_DV
ROPE_PAIRS = RET_DK // 4
ROPE_BASE = 10000.0
D_FF = 2816
N_MOD = 6
EPS = 1e-6

OFF_CX = 0
OFF_CB = OFF_CX + CONV_W
OFF_CC = OFF_CB + CONV_W
OFF_Q = OFF_CC + CONV_W
OFF_K = OFF_Q + RET_QK
OFF_V = OFF_K + RET_QK
OFF_G = OFF_V + RET_V
OFF_GA = OFF_G + RET_V
OFF_GB = OFF_GA + D_MODEL
IN_COLS = OFF_GB + D_MODEL

RET_C = 256
TM_IN = RET_C
TS_RET = 512
RET_EPI_PIECES = 2
FFN_COLS = 8
FFN_BLOCKS = ((0, 1024), (1024, 1024), (2048, 768))
FFN_EPI_PIECES = 4
IN_W_CHUNKS = 11
CO_W_CHUNKS = 8
RET_W_CHUNKS = 4
FFN_DN_PASS_STEPS = 16
ADA_BN = 1536
MOD_ROWS = 8
VMEM_LIMIT = 60 * 1024 * 1024


def _const_spec(shape):
    zeros = (0,) * len(shape)
    return pl.BlockSpec(shape, lambda *_: zeros, pipeline_mode=pl.Buffered(1))


def _chunk_spec(shape, n_chunks, axis):
    block = list(shape)
    block[axis] //= n_chunks
    assert block[axis] * n_chunks == shape[axis]

    def index_map(i):
        c = jnp.minimum(i, n_chunks - 1)
        return (c, 0) if axis == 0 else (0, c)

    return pl.BlockSpec(tuple(block), index_map)


def _pass_spec(shape, n_steps, first_step=0):
    rows = shape[0] // n_steps
    assert rows * n_steps == shape[0] and rows % 16 == 0
    return pl.BlockSpec((rows, shape[1]), lambda i: (jnp.clip(i - first_step, 0, n_steps - 1), 0))


def _cast_chunks(step, weights):
    n_steps = max(n for _, _, _, n in weights)

    @pl.when(step < n_steps)
    def _():
        for c in range(n_steps):
            @pl.when(step == c)
            def _():
                for src_ref, dst_ref, axis, n in weights:
                    if c < n:
                        size = src_ref.shape[axis]
                        if axis == 0:
                            dst_ref[c * size:(c + 1) * size, :] = src_ref[...].astype(BF16)
                        else:
                            dst_ref[:, c * size:(c + 1) * size] = src_ref[...].astype(BF16)


def _rms(xf):
    return xf * lax.rsqrt(jnp.mean(xf * xf, axis=-1, keepdims=True) + EPS)


def _sigmoid(x):
    return 1.0 / (1.0 + jnp.exp(-x))


def _dot(a, b):
    return jnp.dot(a, b, preferred_element_type=F32)


def _dot_nt(a, b):
    return lax.dot_general(a, b, (((1,), (1,)), ((), ())), preferred_element_type=F32)


def _dot_tn(a, b):
    return lax.dot_general(a, b, (((0,), (0,)), ((), ())), preferred_element_type=F32)


def _ada_kernel(c_ref, cctx_ref, w_ref, b_ref, o_ref):
    bsz = c_ref.shape[0]
    cv = jnp.concatenate([c_ref[...], cctx_ref[...], jnp.zeros((MOD_ROWS - bsz - 1, D_MODEL), F32)], axis=0)
    s = cv * _sigmoid(cv)
    o_ref[...] = _dot(s.astype(BF16), w_ref[...].astype(BF16)) + b_ref[...]


def _ada(c, c_ctx, w_ada, b_ada):
    ncol = w_ada.shape[1]
    return pl.pallas_call(
        _ada_kernel,
        name="ada",
        grid=(ncol // ADA_BN,),
        in_specs=[
            pl.BlockSpec(c.shape, lambda j: (0, 0)),
            pl.BlockSpec((1, D_MODEL), lambda j: (0, 0)),
            pl.BlockSpec((D_MODEL, ADA_BN), lambda j: (0, j)),
            pl.BlockSpec((1, ADA_BN), lambda j: (0, j)),
        ],
        out_specs=pl.BlockSpec((MOD_ROWS, ADA_BN), lambda j: (0, j)),
        out_shape=jax.ShapeDtypeStruct((MOD_ROWS, ncol), F32),
        compiler_params=pltpu.CompilerParams(dimension_semantics=("arbitrary",)),
    )(c, c_ctx.reshape(1, D_MODEL), w_ada, b_ada)


def _tables_kernel(theta_ref, dmat_ref, rowf_ref, rowb_ref, cdt_ref, ctxw_ref, *, ctx_len):
    th = theta_ref[...]
    lg = -(jnp.maximum(-th, 0.0) + jnp.log1p(jnp.exp(-jnp.abs(th))))
    n_i = lax.broadcasted_iota(jnp.int32, (RET_C, RET_C), 0)
    m_i = lax.broadcasted_iota(jnp.int32, (RET_C, RET_C), 1)
    diff = (n_i - m_i).astype(F32)
    idx = lax.broadcasted_iota(jnp.int32, (RET_C, RET_DK), 0).astype(F32)
    midx = lax.broadcasted_iota(jnp.int32, (ctx_len, RET_DK), 0).astype(F32)
    for h in range(RET_HEADS):
        lf = lg[h:h + 1, :]
        lb = lg[RET_HEADS + h:RET_HEADS + h + 1, :]
        lf_c, lb_c = lf[:, :RET_C], lb[:, :RET_C]
        dmat_ref[h] = jnp.exp(jnp.where(diff >= 0.0, diff * lf_c, -diff * lb_c))
        lf_k, lb_k = lf[:, :RET_DK], lb[:, :RET_DK]
        rowf_ref[h, 0] = jnp.exp((idx + 1.0) * lf_k).astype(BF16)
        rowf_ref[h, 1] = jnp.exp((RET_C - 1.0 - idx) * lf_k).astype(BF16)
        rowb_ref[h, 0] = jnp.exp((RET_C - idx) * lb_k).astype(BF16)
        rowb_ref[h, 1] = jnp.exp(idx * lb_k).astype(BF16)
        cdt_ref[h, 0] = jnp.exp(RET_C * lf)
        cdt_ref[h, 1] = jnp.exp(RET_C * lb)
        ctxw_ref[h, 0] = jnp.exp((ctx_len - 1.0 - midx) * lf_k)
        ctxw_ref[h, 1] = jnp.exp(midx * lb_k)


def _tables(theta, ctx_len):
    rows = jax.ShapeDtypeStruct((RET_HEADS, 2, RET_C, RET_DK), BF16)
    return pl.pallas_call(
        functools.partial(_tables_kernel, ctx_len=ctx_len),
        name="tables",
        out_shape=(
            jax.ShapeDtypeStruct((RET_HEADS, RET_C, RET_C), F32),
            rows, rows,
            jax.ShapeDtypeStruct((RET_HEADS, 2, 1, RET_DV), F32),
            jax.ShapeDtypeStruct((RET_HEADS, 2, ctx_len, RET_DK), F32),
        ),
    )(theta)


def _ctx_kernel(ctx_ref, modc_ref, gain_ref, w_ref, ctxw_ref, sfw_ref, sbw_ref, hc_ref, kc_ref):
    j = pl.program_id(0)
    bsz, lc, _ = ctx_ref.shape

    @pl.when(j == 0)
    def _():
        mc = modc_ref[0]
        xc = ctx_ref[...].reshape(bsz * lc, D_MODEL)
        hc_ref[...] = (_rms(xc) * gain_ref[...] * (1.0 + mc[1:2]) + mc[0:1]).astype(BF16)

    p = _dot(hc_ref[...], w_ref[...].astype(BF16))

    @pl.when(j == 0)
    def _():
        kc_ref[...] = p * RET_DK ** -0.5

    heads_per_step = RET_QK // RET_DV
    for step in range(1, RET_HEADS // heads_per_step + 1):
        @pl.when(j == step)
        def _():
            for hh in range(heads_per_step):
                h = (step - 1) * heads_per_step + hh
                for b in range(bsz):
                    rows = slice(b * lc, (b + 1) * lc)
                    kh = kc_ref[rows, h * RET_DK:(h + 1) * RET_DK]
                    vh = p[rows, hh * RET_DV:(hh + 1) * RET_DV].astype(BF16)
                    sfw_ref[b, hh] = _dot_tn((kh * ctxw_ref[h, 0]).astype(BF16), vh)
                    sbw_ref[b, hh] = _dot_tn((kh * ctxw_ref[h, 1]).astype(BF16), vh)


def _ctx_states(ctx, modt, gain, w_in, ctxw):
    bsz, lc, _ = ctx.shape
    assert OFF_K % RET_QK == 0 and OFF_V == OFF_K + RET_QK and RET_QK % RET_DV == 0
    heads_per_step = RET_QK // RET_DV
    n_steps = 1 + RET_HEADS // heads_per_step
    state = jax.ShapeDtypeStruct((bsz, RET_HEADS, RET_DK, RET_DV), F32)
    sspec = pl.BlockSpec((bsz, heads_per_step, RET_DK, RET_DV), lambda j: (0, jnp.maximum(j - 1, 0), 0, 0))
    return pl.pallas_call(
        _ctx_kernel,
        name="ctx",
        grid=(n_steps,),
        in_specs=[
            pl.BlockSpec(ctx.shape, lambda j: (0, 0, 0)),
            pl.BlockSpec((1, N_MOD, D_MODEL), lambda j: (bsz, 0, 0)),
            pl.BlockSpec((1, D_MODEL), lambda j: (0, 0)),
            pl.BlockSpec((D_MODEL, RET_QK), lambda j: (0, OFF_K // RET_QK + j)),
            pl.BlockSpec(ctxw.shape, lambda j: (0, 0, 0, 0)),
        ],
        out_specs=(sspec, sspec),
        out_shape=(state, state),
        scratch_shapes=[pltpu.VMEM((bsz * lc, D_MODEL), BF16), pltpu.VMEM((bsz * lc, RET_QK), F32)],
        compiler_params=pltpu.CompilerParams(dimension_semantics=("arbitrary",),
                                             vmem_limit_bytes=VMEM_LIMIT),
    )(ctx, modt, gain, w_in, ctxw)


def _inproj_kernel(x_ref, mod_ref, gain_ref, cos_ref, sin_ref, win32_ref, wco32_ref, cw_ref, cb_ref,
                   s0_ref, rowb_ref, cdt_ref,
                   yc_ref, q_ref, k_ref, v_ref, sg_ref, sgb_ref, ob_ref, t_ref, win_ref, wco_ref,
                   *, tiles_per_seq):
    step = pl.program_id(0)
    _cast_chunks(step, ((win32_ref, win_ref, 1, IN_W_CHUNKS), (wco32_ref, wco_ref, 0, CO_W_CHUNKS)))

    @pl.when(step >= IN_W_CHUNKS)
    def _():
        _inproj_tile(x_ref, mod_ref, gain_ref, cos_ref, sin_ref, win_ref, wco_ref, cw_ref, cb_ref,
                     s0_ref, rowb_ref, cdt_ref, yc_ref, q_ref, k_ref, v_ref, sg_ref, sgb_ref, ob_ref,
                     t_ref, (step - IN_W_CHUNKS) % tiles_per_seq == 0)


def _inproj_tile(x_ref, mod_ref, gain_ref, cos_ref, sin_ref, win_ref, wco_ref, cw_ref, cb_ref,
                 s0_ref, rowb_ref, cdt_ref,
                 yc_ref, q_ref, k_ref, v_ref, sg_ref, sgb_ref, ob_ref, t_ref, first_tile):
    @pl.when(first_tile)
    def _():
        t_ref[...] = s0_ref[0]

    m = mod_ref[0]
    h = _rms(x_ref[...]) * gain_ref[...] * (1.0 + m[1:2]) + m[0:1]
    hb = h.astype(BF16)

    def proj(off, width):
        return _dot(hb, win_ref[:, off:off + width])

    u = proj(OFF_CC, CONV_W) * proj(OFF_CX, CONV_W)
    col = lax.broadcasted_iota(jnp.int32, (TM_IN, CONV_W), 0) % GRID_W
    prev = jnp.where(col == 0, 0.0, pltpu.roll(u, 1, 0))
    nxt = jnp.where(col == GRID_W - 1, 0.0, pltpu.roll(u, TM_IN - 1, 0))
    cw = cw_ref[...]
    conv = cw[0:1] * prev + cw[1:2] * u + cw[2:3] * nxt + cb_ref[...]
    z = (proj(OFF_CB, CONV_W) * conv).astype(BF16)
    yc_ref[...] = (_sigmoid(proj(OFF_GA, D_MODEL)) * _dot(z, wco_ref[...])).astype(BF16)
    sgb_ref[...] = _sigmoid(proj(OFF_GB, D_MODEL)).astype(BF16)

    cos = cos_ref[...]
    sin = sin_ref[...]
    half = RET_DK // 2
    for off, ref, scale in ((OFF_Q, q_ref, 1.0), (OFF_K, k_ref, RET_DK ** -0.5)):
        t = proj(off, RET_QK)
        for hh in range(RET_HEADS):
            lo = hh * RET_DK
            t1 = t[:, lo:lo + half]
            t2 = t[:, lo + half:lo + RET_DK]
            ref[:, lo:lo + half] = ((t1 * cos - t2 * sin) * scale).astype(BF16)
            ref[:, lo + half:lo + RET_DK] = ((t2 * cos + t1 * sin) * scale).astype(BF16)
    v_ref[...] = proj(OFF_V, RET_V).astype(BF16)
    g = proj(OFF_G, RET_V)
    sg_ref[...] = (g * _sigmoid(g)).astype(BF16)

    for hh in range(RET_HEADS):
        qs = q_ref[:, hh * RET_DK:(hh + 1) * RET_DK] * rowb_ref[hh, 0]
        ks = k_ref[:, hh * RET_DK:(hh + 1) * RET_DK] * rowb_ref[hh, 1]
        vh = v_ref[:, hh * RET_DV:(hh + 1) * RET_DV]
        t = t_ref[hh]
        ob_ref[:, hh * RET_DV:(hh + 1) * RET_DV] = _dot(qs, t.astype(BF16)).astype(BF16)
        t_ref[hh] = cdt_ref[hh, 1] * t + _dot_tn(ks, vh)


def _inproj(x2, modt, gain, cos, sin, win, wco, cw, cb, s_bw, rowb, cdt, bsz, seq):
    nt = seq // TM_IN

    def tile(i):
        t = jnp.maximum(i - IN_W_CHUNKS, 0)
        return t // nt, nt - 1 - t % nt

    def row_spec(width):
        return pl.BlockSpec((TM_IN, width), lambda i: (tile(i)[0] * nt + tile(i)[1], 0))

    def out(width):
        return jax.ShapeDtypeStruct((bsz * seq, width), BF16)

    rope = pl.BlockSpec((TM_IN, RET_DK // 2), lambda i: (tile(i)[1], 0))
    return pl.pallas_call(
        functools.partial(_inproj_kernel, tiles_per_seq=nt),
        name="inproj",
        grid=(IN_W_CHUNKS + bsz * nt,),
        in_specs=[
            row_spec(D_MODEL),
            pl.BlockSpec((1, N_MOD, D_MODEL), lambda i: (tile(i)[0], 0, 0)),
            pl.BlockSpec((1, D_MODEL), lambda i: (0, 0)),
            rope, rope,
            _chunk_spec(win.shape, IN_W_CHUNKS, 1),
            _chunk_spec(wco.shape, CO_W_CHUNKS, 0),
            pl.BlockSpec(cw.shape, lambda i: (0, 0)),
            pl.BlockSpec((1, CONV_W), lambda i: (0, 0)),
            pl.BlockSpec((1, RET_HEADS, RET_DK, RET_DV), lambda i: (tile(i)[0], 0, 0, 0)),
            _const_spec(rowb.shape),
            _const_spec(cdt.shape),
        ],
        out_specs=(row_spec(D_MODEL), row_spec(RET_QK), row_spec(RET_QK), row_spec(RET_V),
                   row_spec(RET_V), row_spec(D_MODEL), row_spec(RET_V)),
        out_shape=(out(D_MODEL), out(RET_QK), out(RET_QK), out(RET_V), out(RET_V), out(D_MODEL),
                   out(RET_V)),
        scratch_shapes=[pltpu.VMEM((RET_HEADS, RET_DK, RET_DV), F32),
                        pltpu.VMEM(win.shape, BF16), pltpu.VMEM(wco.shape, BF16)],
        compiler_params=pltpu.CompilerParams(dimension_semantics=("arbitrary",),
                                             vmem_limit_bytes=VMEM_LIMIT),
    )(x2, modt, gain, cos, sin, win, wco, cw, cb, s_bw, rowb, cdt)


def _ret_fw_kernel(q_ref, k_ref, v_ref, sg_ref, ob_ref, yc_ref, sgb_ref, x_ref, s0_ref, dmat_ref,
                   rowf_ref, cdt_ref, wro32_ref, wo32_ref, mod_ref, gpost_ref, wup32_ref, wdn32_ref,
                   x1_ref, wup_out_ref, wdn_out_ref, s_ref, gat_ref, wro_ref, wo_ref, *, steps_per_seq):
    step = pl.program_id(0)
    _cast_chunks(step, ((wro32_ref, wro_ref, 0, RET_W_CHUNKS), (wo32_ref, wo_ref, 0, RET_W_CHUNKS)))

    @pl.when(step >= RET_W_CHUNKS)
    def _():
        wup_out_ref[...] = wup32_ref[...].astype(BF16)
        wdn_out_ref[...] = wdn32_ref[...].astype(BF16)
        _ret_fw_tile(q_ref, k_ref, v_ref, sg_ref, ob_ref, yc_ref, sgb_ref, x_ref, s0_ref, dmat_ref,
                     rowf_ref, cdt_ref, wro_ref, wo_ref, mod_ref, gpost_ref, x1_ref, s_ref, gat_ref,
                     (step - RET_W_CHUNKS) % steps_per_seq == 0)


def _ret_fw_tile(q_ref, k_ref, v_ref, sg_ref, ob_ref, yc_ref, sgb_ref, x_ref, s0_ref, dmat_ref,
                 rowf_ref, cdt_ref, wro_ref, wo_ref, mod_ref, gpost_ref,
                 x1_ref, s_ref, gat_ref, first_tile):
    @pl.when(first_tile)
    def _():
        s_ref[...] = s0_ref[0]

    for ci in range(TS_RET // RET_C):
        r0 = ci * RET_C
        for h in range(RET_HEADS):
            qb = q_ref[r0:r0 + RET_C, h * RET_DK:(h + 1) * RET_DK]
            kb = k_ref[r0:r0 + RET_C, h * RET_DK:(h + 1) * RET_DK]
            vh = v_ref[r0:r0 + RET_C, h * RET_DV:(h + 1) * RET_DV]
            scores = (_dot_nt(qb, kb) * dmat_ref[h]).astype(BF16)
            s = s_ref[h]
            o = (_dot(scores, vh) + _dot(qb * rowf_ref[h, 0], s.astype(BF16))
                 + ob_ref[r0:r0 + RET_C, h * RET_DV:(h + 1) * RET_DV].astype(F32))
            s_ref[h] = cdt_ref[h, 0] * s + _dot_tn(kb * rowf_ref[h, 1], vh)
            gate = sg_ref[r0:r0 + RET_C, h * RET_DV:(h + 1) * RET_DV].astype(F32)
            gat_ref[r0:r0 + RET_C, h * RET_DV:(h + 1) * RET_DV] = (gate * _rms(o)).astype(BF16)

    pr = TS_RET // RET_EPI_PIECES
    for p in range(RET_EPI_PIECES):
        rows = slice(p * pr, (p + 1) * pr)
        y_ret = _dot(gat_ref[rows, :], wro_ref[...])
        merged = yc_ref[rows, :].astype(F32) + sgb_ref[rows, :].astype(F32) * y_ret
        y = _dot(merged.astype(BF16), wo_ref[...])
        x1_ref[rows, :] = x_ref[rows, :] + mod_ref[0][2:3] * (_rms(y) * gpost_ref[...])


def _ret_fw(q, k, v, sg, ob, yc, sgb, x2, s_fw, dmat, rowf, cdt, wro, wo, modt, gpost, wup, wdn,
            bsz, seq):
    ns = seq // TS_RET
    passed = [_pass_spec(wup.shape, bsz * ns, RET_W_CHUNKS),
              _pass_spec(wdn.shape, FFN_DN_PASS_STEPS, RET_W_CHUNKS)]

    def tile(i):
        return jnp.maximum(i - RET_W_CHUNKS, 0)

    def row_spec(width):
        return pl.BlockSpec((TS_RET, width), lambda i: (tile(i), 0))

    return pl.pallas_call(
        functools.partial(_ret_fw_kernel, steps_per_seq=ns),
        name="ret_fw",
        grid=(RET_W_CHUNKS + bsz * ns,),
        in_specs=[
            row_spec(RET_QK), row_spec(RET_QK), row_spec(RET_V), row_spec(RET_V), row_spec(RET_V),
            row_spec(D_MODEL), row_spec(D_MODEL), row_spec(D_MODEL),
            pl.BlockSpec((1, RET_HEADS, RET_DK, RET_DV), lambda i: (tile(i) // ns, 0, 0, 0)),
            _const_spec(dmat.shape), _const_spec(rowf.shape), _const_spec(cdt.shape),
            _chunk_spec(wro.shape, RET_W_CHUNKS, 0), _chunk_spec(wo.shape, RET_W_CHUNKS, 0),
            pl.BlockSpec((1, N_MOD, D_MODEL), lambda i: (tile(i) // ns, 0, 0)),
            pl.BlockSpec((1, D_MODEL), lambda i: (0, 0)),
            *passed,
        ],
        out_specs=(row_spec(D_MODEL), *passed),
        out_shape=(jax.ShapeDtypeStruct((bsz * seq, D_MODEL), F32),
                   jax.ShapeDtypeStruct(wup.shape, BF16), jax.ShapeDtypeStruct(wdn.shape, BF16)),
        scratch_shapes=[pltpu.VMEM((RET_HEADS, RET_DK, RET_DV), F32),
                        pltpu.VMEM((TS_RET, RET_V), BF16),
                        pltpu.VMEM(wro.shape, BF16), pltpu.VMEM(wo.shape, BF16)],
        compiler_params=pltpu.CompilerParams(dimension_semantics=("arbitrary",),
                                             vmem_limit_bytes=VMEM_LIMIT),
    )(q, k, v, sg, ob, yc, sgb, x2, s_fw, dmat, rowf, cdt, wro, wo, modt, gpost, wup, wdn)


def _ffn_kernel(x1_ref, mod_ref, gpre_ref, gpost_ref, wup_ref, cw_ref, cb_ref, wdn_ref,
                out_ref, acc_ref, *, grid_rows):
    tm = grid_rows * FFN_COLS
    x1 = x1_ref[0].reshape(tm, D_MODEL)
    m = mod_ref[0]
    hb = (_rms(x1) * gpre_ref[...] * (1.0 + m[4:5]) + m[3:4]).astype(BF16)

    def conv(u, off, width):
        w = cw_ref[:, off:off + width]
        pad = jnp.zeros((FFN_COLS, width), F32)
        above = jnp.concatenate([pad, u[:tm - FFN_COLS]], axis=0)
        below = jnp.concatenate([u[FFN_COLS:], pad], axis=0)
        return w[0:1] * above + w[1:2] * u + w[2:3] * below + cb_ref[:, off:off + width]

    def up(c0, width):
        return (_dot(hb, wup_ref[:, c0:c0 + width]),
                _dot(hb, wup_ref[:, D_FF + c0:D_FF + c0 + width]))

    ahead = up(*FFN_BLOCKS[0])
    for j, (c0, width) in enumerate(FFN_BLOCKS):
        ug, uv = ahead
        if j + 1 < len(FFN_BLOCKS):
            ahead = up(*FFN_BLOCKS[j + 1])
        g = conv(ug, c0, width)
        val = conv(uv, D_FF + c0, width)
        a = (g * _sigmoid(g) * val).astype(BF16)
        if j == 0:
            acc_ref[...] = _dot(a, wdn_ref[c0:c0 + width, :])
        elif j + 1 < len(FFN_BLOCKS):
            acc_ref[...] += _dot(a, wdn_ref[c0:c0 + width, :])
        else:
            gr = grid_rows // FFN_EPI_PIECES
            for p in range(FFN_EPI_PIECES):
                rows = slice(p * gr * FFN_COLS, (p + 1) * gr * FFN_COLS)
                f = acc_ref[rows, :] + _dot(a[rows, :], wdn_ref[c0:c0 + width, :])
                res = x1[rows, :] + m[5:6] * (_rms(f) * gpost_ref[...])
                out_ref[0, p * gr:(p + 1) * gr] = res.reshape(gr, FFN_COLS, D_MODEL)


def _ffn(x1g, modt, gpre, gpost, wup, cw, cb, wdn):
    bsz, grid_rows, grid_w, _ = x1g.shape
    strips = grid_w // FFN_COLS
    strip = pl.BlockSpec((1, grid_rows, FFN_COLS, D_MODEL), lambda i: (i // strips, 0, i % strips, 0))
    vec = pl.BlockSpec((1, D_MODEL), lambda i: (0, 0))
    return pl.pallas_call(
        functools.partial(_ffn_kernel, grid_rows=grid_rows),
        name="ffn",
        grid=(bsz * strips,),
        in_specs=[
            strip,
            pl.BlockSpec((1, N_MOD, D_MODEL), lambda i: (i // strips, 0, 0)),
            vec, vec,
            _const_spec(wup.shape),
            pl.BlockSpec(cw.shape, lambda i: (0, 0)),
            pl.BlockSpec((1, 2 * D_FF), lambda i: (0, 0)),
            _const_spec(wdn.shape),
        ],
        out_specs=strip,
        out_shape=jax.ShapeDtypeStruct(x1g.shape, F32),
        scratch_shapes=[pltpu.VMEM((grid_rows * FFN_COLS, D_MODEL), F32)],
        compiler_params=pltpu.CompilerParams(dimension_semantics=("arbitrary",),
                                             vmem_limit_bytes=VMEM_LIMIT),
    )(x1g, modt, gpre, gpost, wup, cw, cb, wdn)


def _rotary_tables(n):
    t = np.arange(n)
    row = (t // GRID_W).astype(np.float64)
    col = (t % GRID_W).astype(np.float64)
    inv = ROPE_BASE ** (-np.arange(ROPE_PAIRS, dtype=np.float64) / ROPE_PAIRS)
    ang = np.concatenate([row[:, None] * inv, col[:, None] * inv], axis=-1)
    return jnp.asarray(np.cos(ang), F32), jnp.asarray(np.sin(ang), F32)


def kernel(x, c, ctx, c_ctx, w_ada, b_ada, norm_mix_pre, norm_mix_post, w_in, conv_w, conv_b,
           w_conv_out, ret_decay_fw, ret_decay_bw, w_ret_out, w_o, norm_ffn_pre, norm_ffn_post,
           w_ffn_up, ffn_conv_w, ffn_conv_b, w_ffn_down):
    bsz, seq, d = x.shape
    depth = w_ada.shape[0]
    assert d == D_MODEL and depth == 1 and bsz + 1 <= MOD_ROWS
    assert seq % TS_RET == 0 and seq % TM_IN == 0 and TM_IN % GRID_W == 0 and GRID_W % FFN_COLS == 0
    lc = ctx.shape[1]
    cos, sin = _rotary_tables(seq)
    x2 = x.reshape(bsz * seq, d)

    modt = _ada(c, c_ctx, w_ada[0], b_ada).reshape(MOD_ROWS, N_MOD, d)

    theta = jnp.concatenate([ret_decay_fw[0], ret_decay_bw[0]]).astype(F32)
    theta = jnp.broadcast_to(theta[:, None], (2 * RET_HEADS, RET_DV))
    dmat, rowf, rowb, cdt, ctxw = _tables(theta, lc)

    s_fw, s_bw = _ctx_states(ctx, modt, norm_mix_pre, w_in[0], ctxw)

    yc, q, k, v, sg, sgb, ob = _inproj(x2, modt, norm_mix_pre, cos, sin, w_in[0], w_conv_out[0],
                                       conv_w[0], conv_b, s_bw, rowb, cdt, bsz, seq)
    x1, wup, wdn = _ret_fw(q, k, v, sg, ob, yc, sgb, x2, s_fw, dmat, rowf, cdt, w_ret_out[0], w_o[0],
                           modt, norm_mix_post, w_ffn_up[0], w_ffn_down[0], bsz, seq)
    out = _ffn(x1.reshape(bsz, seq // GRID_W, GRID_W, d), modt, norm_ffn_pre, norm_ffn_post,
               wup, ffn_conv_w[0], ffn_conv_b, wdn)
    return out.reshape(bsz, seq, d)
```

```python
import functools

import jax
import jax.numpy as jnp
import numpy as np
from jax import lax
from jax.experimental import pallas as pl
from jax.experimental.pallas import tpu as pltpu

F32 = jnp.float32
BF16 = jnp.bfloat16

D_MODEL = 1024
GRID_W = 64
CONV_W = 1024
RET_HEADS = 4
RET_DK = 256
RET_DV = 512
RET_QK = RET_HEADS * RET_DK
RET_V = RET_HEADS * RET_DV
ROPE_PAIRS = RET_DK // 4
ROPE_BASE = 10000.0
D_FF = 2816
N_MOD = 6
EPS = 1e-6

OFF_CX = 0
OFF_CB = OFF_CX + CONV_W
OFF_CC = OFF_CB + CONV_W
OFF_Q = OFF_CC + CONV_W
OFF_K = OFF_Q + RET_QK
OFF_V = OFF_K + RET_QK
OFF_G = OFF_V + RET_V
OFF_GA = OFF_G + RET_V
OFF_GB = OFF_GA + D_MODEL
IN_COLS = OFF_GB + D_MODEL

RET_C = 256
TM_IN = RET_C
TS_RET = 512
RET_EPI_BOUNDS = (0, TS_RET // 2, TS_RET)
FFN_COLS = 8
FFN_BLOCKS = ((0, 1024), (1024, 1024), (2048, 768))
FFN_EPI_PIECES = 4
IN_W_CHUNKS = 11
CO_W_CHUNKS = 8
RET_W_CHUNKS = 4
FFN_DN_PASS_STEPS = 16
ADA_BN = 1536
MOD_ROWS = 8
VMEM_LIMIT = 60 * 1024 * 1024


def _const_spec(shape):
    zeros = (0,) * len(shape)
    return pl.BlockSpec(shape, lambda *_: zeros, pipeline_mode=pl.Buffered(1))


def _chunk_spec(shape, n_chunks, axis):
    block = list(shape)
    block[axis] //= n_chunks
    assert block[axis] * n_chunks == shape[axis]

    def index_map(i):
        c = jnp.minimum(i, n_chunks - 1)
        return (c, 0) if axis == 0 else (0, c)

    return pl.BlockSpec(tuple(block), index_map)


def _pass_spec(shape, n_steps, first_step=0):
    rows = shape[0] // n_steps
    assert rows * n_steps == shape[0] and rows % 16 == 0
    return pl.BlockSpec((rows, shape[1]), lambda i: (jnp.clip(i - first_step, 0, n_steps - 1), 0))


def _cast_chunks(step, weights):
    n_steps = max(n for _, _, _, n in weights)

    @pl.when(step < n_steps)
    def _():
        for c in range(n_steps):
            @pl.when(step == c)
            def _():
                for src_ref, dst_ref, axis, n in weights:
                    if c < n:
                        size = src_ref.shape[axis]
                        if axis == 0:
                            dst_ref[c * size:(c + 1) * size, :] = src_ref[...].astype(BF16)
                        else:
                            dst_ref[:, c * size:(c + 1) * size] = src_ref[...].astype(BF16)


def _rms(xf):
    return xf * lax.rsqrt(jnp.mean(xf * xf, axis=-1, keepdims=True) + EPS)


def _sigmoid(x):
    return 1.0 / (1.0 + jnp.exp(-x))


def _dot(a, b):
    return jnp.dot(a, b, preferred_element_type=F32)


def _dot_nt(a, b):
    return lax.dot_general(a, b, (((1,), (1,)), ((), ())), preferred_element_type=F32)


def _dot_tn(a, b):
    return lax.dot_general(a, b, (((0,), (0,)), ((), ())), preferred_element_type=F32)


def _ada_kernel(c_ref, cctx_ref, w_ref, b_ref, o_ref):
    bsz = c_ref.shape[0]
    cv = jnp.concatenate([c_ref[...], cctx_ref[...], jnp.zeros((MOD_ROWS - bsz - 1, D_MODEL), F32)], axis=0)
    s = cv * _sigmoid(cv)
    o_ref[...] = _dot(s.astype(BF16), w_ref[...].astype(BF16)) + b_ref[...]


def _ada(c, c_ctx, w_ada, b_ada):
    ncol = w_ada.shape[1]
    return pl.pallas_call(
        _ada_kernel,
        name="ada",
        grid=(ncol // ADA_BN,),
        in_specs=[
            pl.BlockSpec(c.shape, lambda j: (0, 0)),
            pl.BlockSpec((1, D_MODEL), lambda j: (0, 0)),
            pl.BlockSpec((D_MODEL, ADA_BN), lambda j: (0, j)),
            pl.BlockSpec((1, ADA_BN), lambda j: (0, j)),
        ],
        out_specs=pl.BlockSpec((MOD_ROWS, ADA_BN), lambda j: (0, j)),
        out_shape=jax.ShapeDtypeStruct((MOD_ROWS, ncol), F32),
        compiler_params=pltpu.CompilerParams(dimension_semantics=("arbitrary",)),
    )(c, c_ctx.reshape(1, D_MODEL), w_ada, b_ada)


def _tables_kernel(theta_ref, dmat_ref, rowf_ref, rowb_ref, cdt_ref, ctxw_ref, *, ctx_len):
    th = theta_ref[...]
    lg = -(jnp.maximum(-th, 0.0) + jnp.log1p(jnp.exp(-jnp.abs(th))))
    n_i = lax.broadcasted_iota(jnp.int32, (RET_C, RET_C), 0)
    m_i = lax.broadcasted_iota(jnp.int32, (RET_C, RET_C), 1)
    diff = (n_i - m_i).astype(F32)
    idx = lax.broadcasted_iota(jnp.int32, (RET_C, RET_DK), 0).astype(F32)
    midx = lax.broadcasted_iota(jnp.int32, (ctx_len, RET_DK), 0).astype(F32)
    for h in range(RET_HEADS):
        lf = lg[h:h + 1, :]
        lb = lg[RET_HEADS + h:RET_HEADS + h + 1, :]
        lf_c, lb_c = lf[:, :RET_C], lb[:, :RET_C]
        dmat_ref[h] = jnp.exp(jnp.where(diff >= 0.0, diff * lf_c, -diff * lb_c))
        lf_k, lb_k = lf[:, :RET_DK], lb[:, :RET_DK]
        rowf_ref[h, 0] = jnp.exp((idx + 1.0) * lf_k).astype(BF16)
        rowf_ref[h, 1] = jnp.exp((RET_C - 1.0 - idx) * lf_k).astype(BF16)
        rowb_ref[h, 0] = jnp.exp((RET_C - idx) * lb_k).astype(BF16)
        rowb_ref[h, 1] = jnp.exp(idx * lb_k).astype(BF16)
        cdt_ref[h, 0] = jnp.exp(RET_C * lf)
        cdt_ref[h, 1] = jnp.exp(RET_C * lb)
        ctxw_ref[h, 0] = jnp.exp((ctx_len - 1.0 - midx) * lf_k)
        ctxw_ref[h, 1] = jnp.exp(midx * lb_k)


def _ctx_kernel(ctx_ref, modc_ref, gain_ref, w_ref, theta_ref,
                sfw_ref, sbw_ref, dmat_ref, rowf_ref, rowb_ref, cdt_ref, hc_ref, kc_ref, ctxw_ref):
    j = pl.program_id(0)
    bsz, lc, _ = ctx_ref.shape

    @pl.when(j == 0)
    def _():
        _tables_kernel(theta_ref, dmat_ref, rowf_ref, rowb_ref, cdt_ref, ctxw_ref, ctx_len=lc)
        mc = modc_ref[0]
        xc = ctx_ref[...].reshape(bsz * lc, D_MODEL)
        hc_ref[...] = (_rms(xc) * gain_ref[...] * (1.0 + mc[1:2]) + mc[0:1]).astype(BF16)

    p = _dot(hc_ref[...], w_ref[...].astype(BF16))

    @pl.when(j == 0)
    def _():
        kc_ref[...] = p * RET_DK ** -0.5

    heads_per_step = RET_QK // RET_DV
    for step in range(1, RET_HEADS // heads_per_step + 1):
        @pl.when(j == step)
        def _():
            for hh in range(heads_per_step):
                h = (step - 1) * heads_per_step + hh
                for b in range(bsz):
                    rows = slice(b * lc, (b + 1) * lc)
                    kh = kc_ref[rows, h * RET_DK:(h + 1) * RET_DK]
                    vh = p[rows, hh * RET_DV:(hh + 1) * RET_DV].astype(BF16)
                    kfb = jnp.concatenate([(kh * ctxw_ref[h, 0]).astype(BF16),
                                           (kh * ctxw_ref[h, 1]).astype(BF16)], axis=1)
                    s_both = _dot_tn(kfb, vh)
                    sfw_ref[b, hh] = s_both[:RET_DK]
                    sbw_ref[b, hh] = s_both[RET_DK:]


def _ctx_states(ctx, modt, gain, w_in, theta):
    bsz, lc, _ = ctx.shape
    assert OFF_K % RET_QK == 0 and OFF_V == OFF_K + RET_QK and RET_QK % RET_DV == 0
    heads_per_step = RET_QK // RET_DV
    n_steps = 1 + RET_HEADS // heads_per_step
    state = jax.ShapeDtypeStruct((bsz, RET_HEADS, RET_DK, RET_DV), F32)
    sspec = pl.BlockSpec((bsz, heads_per_step, RET_DK, RET_DV), lambda j: (0, jnp.maximum(j - 1, 0), 0, 0))
    rows = jax.ShapeDtypeStruct((RET_HEADS, 2, RET_C, RET_DK), BF16)
    tables = (jax.ShapeDtypeStruct((RET_HEADS, RET_C, RET_C), F32), rows, rows,
              jax.ShapeDtypeStruct((RET_HEADS, 2, 1, RET_DV), F32))

    def whole(shape):
        return pl.BlockSpec(shape, lambda j: (0,) * len(shape))

    s_fw, s_bw, dmat, rowf, rowb, cdt = pl.pallas_call(
        _ctx_kernel,
        name="ctx",
        grid=(n_steps,),
        in_specs=[
            pl.BlockSpec(ctx.shape, lambda j: (0, 0, 0)),
            pl.BlockSpec((1, N_MOD, D_MODEL), lambda j: (bsz, 0, 0)),
            pl.BlockSpec((1, D_MODEL), lambda j: (0, 0)),
            pl.BlockSpec((D_MODEL, RET_QK), lambda j: (0, OFF_K // RET_QK + j)),
            whole(theta.shape),
        ],
        out_specs=(sspec, sspec, *[whole(t.shape) for t in tables]),
        out_shape=(state, state, *tables),
        scratch_shapes=[pltpu.VMEM((bsz * lc, D_MODEL), BF16), pltpu.VMEM((bsz * lc, RET_QK), F32),
                        pltpu.VMEM((RET_HEADS, 2, lc, RET_DK), F32)],
        compiler_params=pltpu.CompilerParams(dimension_semantics=("arbitrary",),
                                             vmem_limit_bytes=VMEM_LIMIT),
    )(ctx, modt, gain, w_in, theta)
    return s_fw, s_bw, dmat, rowf, rowb, cdt


def _inproj_kernel(x_ref, mod_ref, gain_ref, cos_ref, sin_ref, win32_ref, wco32_ref, cw_ref, cb_ref,
                   s0_ref, rowb_ref, cdt_ref,
                   yc_ref, q_ref, k_ref, v_ref, sg_ref, sgb_ref, ob_ref, t_ref, win_ref, wco_ref,
                   *, tiles_per_seq):
    step = pl.program_id(0)
    _cast_chunks(step, ((win32_ref, win_ref, 1, IN_W_CHUNKS), (wco32_ref, wco_ref, 0, CO_W_CHUNKS)))

    @pl.when(step >= IN_W_CHUNKS)
    def _():
        _inproj_tile(x_ref, mod_ref, gain_ref, cos_ref, sin_ref, win_ref, wco_ref, cw_ref, cb_ref,
                     s0_ref, rowb_ref, cdt_ref, yc_ref, q_ref, k_ref, v_ref, sg_ref, sgb_ref, ob_ref,
                     t_ref, (step - IN_W_CHUNKS) % tiles_per_seq == 0)


def _inproj_tile(x_ref, mod_ref, gain_ref, cos_ref, sin_ref, win_ref, wco_ref, cw_ref, cb_ref,
                 s0_ref, rowb_ref, cdt_ref,
                 yc_ref, q_ref, k_ref, v_ref, sg_ref, sgb_ref, ob_ref, t_ref, first_tile):
    @pl.when(first_tile)
    def _():
        t_ref[...] = s0_ref[0]

    m = mod_ref[0]
    h = _rms(x_ref[...]) * gain_ref[...] * (1.0 + m[1:2]) + m[0:1]
    hb = h.astype(BF16)

    def proj(off, width):
        return _dot(hb, win_ref[:, off:off + width])

    u = proj(OFF_CC, CONV_W) * proj(OFF_CX, CONV_W)
    col = lax.broadcasted_iota(jnp.int32, (TM_IN, CONV_W), 0) % GRID_W
    prev = jnp.where(col == 0, 0.0, pltpu.roll(u, 1, 0))
    nxt = jnp.where(col == GRID_W - 1, 0.0, pltpu.roll(u, TM_IN - 1, 0))
    cw = cw_ref[...]
    conv = cw[0:1] * prev + cw[1:2] * u + cw[2:3] * nxt + cb_ref[...]
    z = (proj(OFF_CB, CONV_W) * conv).astype(BF16)
    yc_ref[...] = (_sigmoid(proj(OFF_GA, D_MODEL)) * _dot(z, wco_ref[...])).astype(BF16)
    sgb_ref[...] = _sigmoid(proj(OFF_GB, D_MODEL)).astype(BF16)

    cos = cos_ref[...]
    sin = sin_ref[...]
    half = RET_DK // 2
    for off, ref, scale in ((OFF_Q, q_ref, 1.0), (OFF_K, k_ref, RET_DK ** -0.5)):
        t = proj(off, RET_QK)
        for hh in range(RET_HEADS):
            lo = hh * RET_DK
            t1 = t[:, lo:lo + half]
            t2 = t[:, lo + half:lo + RET_DK]
            ref[:, lo:lo + half] = ((t1 * cos - t2 * sin) * scale).astype(BF16)
            ref[:, lo + half:lo + RET_DK] = ((t2 * cos + t1 * sin) * scale).astype(BF16)
    v_ref[...] = proj(OFF_V, RET_V).astype(BF16)
    g = proj(OFF_G, RET_V)
    sg_ref[...] = (g * _sigmoid(g)).astype(BF16)

    for hh in range(RET_HEADS):
        qs = q_ref[:, hh * RET_DK:(hh + 1) * RET_DK] * rowb_ref[hh, 0]
        ks = k_ref[:, hh * RET_DK:(hh + 1) * RET_DK] * rowb_ref[hh, 1]
        vh = v_ref[:, hh * RET_DV:(hh + 1) * RET_DV]
        t = t_ref[hh]
        ob_ref[:, hh * RET_DV:(hh + 1) * RET_DV] = _dot(qs, t.astype(BF16)).astype(BF16)
        t_ref[hh] = cdt_ref[hh, 1] * t + _dot_tn(ks, vh)


def _inproj(x2, modt, gain, cos, sin, win, wco, cw, cb, s_bw, rowb, cdt, bsz, seq):
    nt = seq // TM_IN

    def tile(i):
        t = jnp.maximum(i - IN_W_CHUNKS, 0)
        return t // nt, nt - 1 - t % nt

    def row_spec(width):
        return pl.BlockSpec((TM_IN, width), lambda i: (tile(i)[0] * nt + tile(i)[1], 0))

    def out(width):
        return jax.ShapeDtypeStruct((bsz * seq, width), BF16)

    rope = pl.BlockSpec((TM_IN, RET_DK // 2), lambda i: (tile(i)[1], 0))
    return pl.pallas_call(
        functools.partial(_inproj_kernel, tiles_per_seq=nt),
        name="inproj",
        grid=(IN_W_CHUNKS + bsz * nt,),
        in_specs=[
            row_spec(D_MODEL),
            pl.BlockSpec((1, N_MOD, D_MODEL), lambda i: (tile(i)[0], 0, 0)),
            pl.BlockSpec((1, D_MODEL), lambda i: (0, 0)),
            rope, rope,
            _chunk_spec(win.shape, IN_W_CHUNKS, 1),
            _chunk_spec(wco.shape, CO_W_CHUNKS, 0),
            pl.BlockSpec(cw.shape, lambda i: (0, 0)),
            pl.BlockSpec((1, CONV_W), lambda i: (0, 0)),
            pl.BlockSpec((1, RET_HEADS, RET_DK, RET_DV), lambda i: (tile(i)[0], 0, 0, 0)),
            _const_spec(rowb.shape),
            _const_spec(cdt.shape),
        ],
        out_specs=(row_spec(D_MODEL), row_spec(RET_QK), row_spec(RET_QK), row_spec(RET_V),
                   row_spec(RET_V), row_spec(D_MODEL), row_spec(RET_V)),
        out_shape=(out(D_MODEL), out(RET_QK), out(RET_QK), out(RET_V), out(RET_V), out(D_MODEL),
                   out(RET_V)),
        scratch_shapes=[pltpu.VMEM((RET_HEADS, RET_DK, RET_DV), F32),
                        pltpu.VMEM(win.shape, BF16), pltpu.VMEM(wco.shape, BF16)],
        compiler_params=pltpu.CompilerParams(dimension_semantics=("arbitrary",),
                                             vmem_limit_bytes=VMEM_LIMIT),
    )(x2, modt, gain, cos, sin, win, wco, cw, cb, s_bw, rowb, cdt)


def _ret_fw_kernel(q_ref, k_ref, v_ref, sg_ref, ob_ref, yc_ref, sgb_ref, x_ref, s0_ref, dmat_ref,
                   rowf_ref, cdt_ref, wro32_ref, wo32_ref, mod_ref, gpost_ref, wup32_ref, wdn32_ref,
                   x1_ref, wup_out_ref, wdn_out_ref, s_ref, gat_ref, wro_ref, wo_ref, *, steps_per_seq):
    step = pl.program_id(0)
    _cast_chunks(step, ((wro32_ref, wro_ref, 0, RET_W_CHUNKS), (wo32_ref, wo_ref, 0, RET_W_CHUNKS)))

    @pl.when(step >= RET_W_CHUNKS)
    def _():
        wup_out_ref[...] = wup32_ref[...].astype(BF16)
        wdn_out_ref[...] = wdn32_ref[...].astype(BF16)
        _ret_fw_tile(q_ref, k_ref, v_ref, sg_ref, ob_ref, yc_ref, sgb_ref, x_ref, s0_ref, dmat_ref,
                     rowf_ref, cdt_ref, wro_ref, wo_ref, mod_ref, gpost_ref, x1_ref, s_ref, gat_ref,
                     (step - RET_W_CHUNKS) % steps_per_seq == 0)


def _ret_fw_tile(q_ref, k_ref, v_ref, sg_ref, ob_ref, yc_ref, sgb_ref, x_ref, s0_ref, dmat_ref,
                 rowf_ref, cdt_ref, wro_ref, wo_ref, mod_ref, gpost_ref,
                 x1_ref, s_ref, gat_ref, first_tile):
    @pl.when(first_tile)
    def _():
        s_ref[...] = s0_ref[0]

    for ci in range(TS_RET // RET_C):
        r0 = ci * RET_C
        for h in range(RET_HEADS):
            qb = q_ref[r0:r0 + RET_C, h * RET_DK:(h + 1) * RET_DK]
            kb = k_ref[r0:r0 + RET_C, h * RET_DK:(h + 1) * RET_DK]
            vh = v_ref[r0:r0 + RET_C, h * RET_DV:(h + 1) * RET_DV]
            scores = (_dot_nt(qb, kb) * dmat_ref[h]).astype(BF16)
            s = s_ref[h]
            o = (_dot(scores, vh) + _dot(qb * rowf_ref[h, 0], s.astype(BF16))
                 + ob_ref[r0:r0 + RET_C, h * RET_DV:(h + 1) * RET_DV].astype(F32))
            s_ref[h] = cdt_ref[h, 0] * s + _dot_tn(kb * rowf_ref[h, 1], vh)
            gate = sg_ref[r0:r0 + RET_C, h * RET_DV:(h + 1) * RET_DV].astype(F32)
            gat_ref[r0:r0 + RET_C, h * RET_DV:(h + 1) * RET_DV] = (gate * _rms(o)).astype(BF16)

    for p in range(len(RET_EPI_BOUNDS) - 1):
        rows = slice(RET_EPI_BOUNDS[p], RET_EPI_BOUNDS[p + 1])
        y_ret = _dot(gat_ref[rows, :], wro_ref[...])
        merged = yc_ref[rows, :].astype(F32) + sgb_ref[rows, :].astype(F32) * y_ret
        y = _dot(merged.astype(BF16), wo_ref[...])
        x1_ref[rows, :] = x_ref[rows, :] + mod_ref[0][2:3] * (_rms(y) * gpost_ref[...])


def _ret_fw(q, k, v, sg, ob, yc, sgb, x2, s_fw, dmat, rowf, cdt, wro, wo, modt, gpost, wup, wdn,
            bsz, seq):
    ns = seq // TS_RET
    passed = [_pass_spec(wup.shape, bsz * ns, RET_W_CHUNKS),
              _pass_spec(wdn.shape, FFN_DN_PASS_STEPS, RET_W_CHUNKS)]

    def tile(i):
        return jnp.maximum(i - RET_W_CHUNKS, 0)

    def row_spec(width):
        return pl.BlockSpec((TS_RET, width), lambda i: (tile(i), 0))

    return pl.pallas_call(
        functools.partial(_ret_fw_kernel, steps_per_seq=ns),
        name="ret_fw",
        grid=(RET_W_CHUNKS + bsz * ns,),
        in_specs=[
            row_spec(RET_QK), row_spec(RET_QK), row_spec(RET_V), row_spec(RET_V), row_spec(RET_V),
            row_spec(D_MODEL), row_spec(D_MODEL), row_spec(D_MODEL),
            pl.BlockSpec((1, RET_HEADS, RET_DK, RET_DV), lambda i: (tile(i) // ns, 0, 0, 0)),
            _const_spec(dmat.shape), _const_spec(rowf.shape), _const_spec(cdt.shape),
            _chunk_spec(wro.shape, RET_W_CHUNKS, 0), _chunk_spec(wo.shape, RET_W_CHUNKS, 0),
            pl.BlockSpec((1, N_MOD, D_MODEL), lambda i: (tile(i) // ns, 0, 0)),
            pl.BlockSpec((1, D_MODEL), lambda i: (0, 0)),
            *passed,
        ],
        out_specs=(row_spec(D_MODEL), *passed),
        out_shape=(jax.ShapeDtypeStruct((bsz * seq, D_MODEL), F32),
                   jax.ShapeDtypeStruct(wup.shape, BF16), jax.ShapeDtypeStruct(wdn.shape, BF16)),
        scratch_shapes=[pltpu.VMEM((RET_HEADS, RET_DK, RET_DV), F32),
                        pltpu.VMEM((TS_RET, RET_V), BF16),
                        pltpu.VMEM(wro.shape, BF16), pltpu.VMEM(wo.shape, BF16)],
        compiler_params=pltpu.CompilerParams(dimension_semantics=("arbitrary",),
                                             vmem_limit_bytes=VMEM_LIMIT),
    )(q, k, v, sg, ob, yc, sgb, x2, s_fw, dmat, rowf, cdt, wro, wo, modt, gpost, wup, wdn)


def _ffn_kernel(x1_ref, mod_ref, gpre_ref, gpost_ref, wup_ref, cw_ref, cb_ref, wdn_ref,
                out_ref, acc_ref, *, grid_rows):
    tm = grid_rows * FFN_COLS
    x1 = x1_ref[0].reshape(tm, D_MODEL)
    m = mod_ref[0]
    hb = (_rms(x1) * gpre_ref[...] * (1.0 + m[4:5]) + m[3:4]).astype(BF16)

    def conv(u, off, width):
        w = cw_ref[:, off:off + width]
        pad = jnp.zeros((FFN_COLS, width), F32)
        above = jnp.concatenate([pad, u[:tm - FFN_COLS]], axis=0)
        below = jnp.concatenate([u[FFN_COLS:], pad], axis=0)
        return w[0:1] * above + w[1:2] * u + w[2:3] * below + cb_ref[:, off:off + width]

    def up(c0, width):
        return (_dot(hb, wup_ref[:, c0:c0 + width]),
                _dot(hb, wup_ref[:, D_FF + c0:D_FF + c0 + width]))

    ahead = up(*FFN_BLOCKS[0])
    for j, (c0, width) in enumerate(FFN_BLOCKS):
        ug, uv = ahead
        if j + 1 < len(FFN_BLOCKS):
            ahead = up(*FFN_BLOCKS[j + 1])
        g = conv(ug, c0, width)
        val = conv(uv, D_FF + c0, width)
        a = (g * _sigmoid(g) * val).astype(BF16)
        if j == 0:
            acc_ref[...] = _dot(a, wdn_ref[c0:c0 + width, :])
        elif j + 1 < len(FFN_BLOCKS):
            acc_ref[...] += _dot(a, wdn_ref[c0:c0 + width, :])
        else:
            gr = grid_rows // FFN_EPI_PIECES
            for p in range(FFN_EPI_PIECES):
                rows = slice(p * gr * FFN_COLS, (p + 1) * gr * FFN_COLS)
                f = acc_ref[rows, :] + _dot(a[rows, :], wdn_ref[c0:c0 + width, :])
                res = x1[rows, :] + m[5:6] * (_rms(f) * gpost_ref[...])
                out_ref[0, p * gr:(p + 1) * gr] = res.reshape(gr, FFN_COLS, D_MODEL)


def _ffn(x1g, modt, gpre, gpost, wup, cw, cb, wdn):
    bsz, grid_rows, grid_w, _ = x1g.shape
    strips = grid_w // FFN_COLS
    strip = pl.BlockSpec((1, grid_rows, FFN_COLS, D_MODEL), lambda i: (i // strips, 0, i % strips, 0))
    vec = pl.BlockSpec((1, D_MODEL), lambda i: (0, 0))
    return pl.pallas_call(
        functools.partial(_ffn_kernel, grid_rows=grid_rows),
        name="ffn",
        grid=(bsz * strips,),
        in_specs=[
            strip,
            pl.BlockSpec((1, N_MOD, D_MODEL), lambda i: (i // strips, 0, 0)),
            vec, vec,
            _const_spec(wup.shape),
            pl.BlockSpec(cw.shape, lambda i: (0, 0)),
            pl.BlockSpec((1, 2 * D_FF), lambda i: (0, 0)),
            _const_spec(wdn.shape),
        ],
        out_specs=strip,
        out_shape=jax.ShapeDtypeStruct(x1g.shape, F32),
        scratch_shapes=[pltpu.VMEM((grid_rows * FFN_COLS, D_MODEL), F32)],
        compiler_params=pltpu.CompilerParams(dimension_semantics=("arbitrary",),
                                             vmem_limit_bytes=VMEM_LIMIT),
    )(x1g, modt, gpre, gpost, wup, cw, cb, wdn)


def _rotary_tables(n):
    t = np.arange(n)
    row = (t // GRID_W).astype(np.float64)
    col = (t % GRID_W).astype(np.float64)
    inv = ROPE_BASE ** (-np.arange(ROPE_PAIRS, dtype=np.float64) / ROPE_PAIRS)
    ang = np.concatenate([row[:, None] * inv, col[:, None] * inv], axis=-1)
    return jnp.asarray(np.cos(ang), F32), jnp.asarray(np.sin(ang), F32)


def kernel(x, c, ctx, c_ctx, w_ada, b_ada, norm_mix_pre, norm_mix_post, w_in, conv_w, conv_b,
           w_conv_out, ret_decay_fw, ret_decay_bw, w_ret_out, w_o, norm_ffn_pre, norm_ffn_post,
           w_ffn_up, ffn_conv_w, ffn_conv_b, w_ffn_down):
    bsz, seq, d = x.shape
    depth = w_ada.shape[0]
    assert d == D_MODEL and depth == 1 and bsz + 1 <= MOD_ROWS
    assert seq % TS_RET == 0 and seq % TM_IN == 0 and TM_IN % GRID_W == 0 and GRID_W % FFN_COLS == 0
    lc = ctx.shape[1]
    cos, sin = _rotary_tables(seq)
    x2 = x.reshape(bsz * seq, d)

    modt = _ada(c, c_ctx, w_ada[0], b_ada).reshape(MOD_ROWS, N_MOD, d)

    theta = jnp.concatenate([ret_decay_fw[0], ret_decay_bw[0]]).astype(F32)
    theta = jnp.broadcast_to(theta[:, None], (2 * RET_HEADS, RET_DV))
    s_fw, s_bw, dmat, rowf, rowb, cdt = _ctx_states(ctx, modt, norm_mix_pre, w_in[0], theta)

    yc, q, k, v, sg, sgb, ob = _inproj(x2, modt, norm_mix_pre, cos, sin, w_in[0], w_conv_out[0],
                                       conv_w[0], conv_b, s_bw, rowb, cdt, bsz, seq)
    x1, wup, wdn = _ret_fw(q, k, v, sg, ob, yc, sgb, x2, s_fw, dmat, rowf, cdt, w_ret_out[0], w_o[0],
                           modt, norm_mix_post, w_ffn_up[0], w_ffn_down[0], bsz, seq)
    out = _ffn(x1.reshape(bsz, seq // GRID_W, GRID_W, d), modt, norm_ffn_pre, norm_ffn_post,
               wup, ffn_conv_w[0], ffn_conv_b, wdn)
    return out.reshape(bsz, seq, d)
```

```python
import functools

import jax
import jax.numpy as jnp
import numpy as np
from jax import lax
from jax.experimental import pallas as pl
from jax.experimental.pallas import tpu as pltpu

F32 = jnp.float32
BF16 = jnp.bfloat16

D_MODEL = 1024
GRID_W = 64
CONV_W = 1024
RET_HEADS = 4
RET_DK = 256
RET_DV = 512
RET_QK = RET_HEADS * RET_DK
RET_V = RET_HEADS * RET_DV
ROPE_PAIRS = RET_DK // 4
ROPE_BASE = 10000.0
D_FF = 2816
N_MOD = 6
EPS = 1e-6

OFF_CX = 0
OFF_CB = OFF_CX + CONV_W
OFF_CC = OFF_CB + CONV_W
OFF_Q = OFF_CC + CONV_W
OFF_K = OFF_Q + RET_QK
OFF_V = OFF_K + RET_QK
OFF_G = OFF_V + RET_V
OFF_GA = OFF_G + RET_V
OFF_GB = OFF_GA + D_MODEL
IN_COLS = OFF_GB + D_MODEL

RET_C = 256
TM_IN = RET_C
TS_RET = 512
RET_EPI_BOUNDS = (0, TS_RET // 2, TS_RET)
FFN_COLS = 8
FFN_BLOCKS = ((0, 1024), (1024, 1024), (2048, 768))
FFN_EPI_PIECES = 4
IN_W_CHUNKS = 11
CO_W_CHUNKS = 8
RET_W_CHUNKS = 4
FFN_DN_PASS_STEPS = 16
MOD_ROWS = 8
BF16_SUBLANES = 16
V7X_VMEM_BYTES = 64 * 1024 * 1024
VMEM_LIMIT = V7X_VMEM_BYTES - 4 * 1024 * 1024


def _const_spec(shape):
    zeros = (0,) * len(shape)
    return pl.BlockSpec(shape, lambda *_: zeros, pipeline_mode=pl.Buffered(1))


def _chunk_spec(shape, n_chunks, axis):
    block = list(shape)
    block[axis] //= n_chunks
    assert block[axis] * n_chunks == shape[axis]

    def index_map(i):
        c = jnp.minimum(i, n_chunks - 1)
        return (c, 0) if axis == 0 else (0, c)

    return pl.BlockSpec(tuple(block), index_map)


def _pass_spec(shape, n_steps, first_step=0):
    rows = shape[0] // n_steps
    assert rows * n_steps == shape[0] and rows % BF16_SUBLANES == 0
    return pl.BlockSpec((rows, shape[1]), lambda i: (jnp.clip(i - first_step, 0, n_steps - 1), 0))


def _cast_chunks(step, weights):
    n_steps = max(n for _, _, _, n in weights)

    @pl.when(step < n_steps)
    def _():
        for c in range(n_steps):
            @pl.when(step == c)
            def _():
                for src_ref, dst_ref, axis, n in weights:
                    if c < n:
                        size = src_ref.shape[axis]
                        if axis == 0:
                            dst_ref[c * size:(c + 1) * size, :] = src_ref[...].astype(BF16)
                        else:
                            dst_ref[:, c * size:(c + 1) * size] = src_ref[...].astype(BF16)


def _rms(xf):
    return xf * lax.rsqrt(jnp.mean(xf * xf, axis=-1, keepdims=True) + EPS)


def _sigmoid(x):
    return 1.0 / (1.0 + jnp.exp(-x))


def _dot(a, b):
    return jnp.dot(a, b, preferred_element_type=F32)


def _dot_nt(a, b):
    return lax.dot_general(a, b, (((1,), (1,)), ((), ())), preferred_element_type=F32)


def _dot_tn(a, b):
    return lax.dot_general(a, b, (((0,), (0,)), ((), ())), preferred_element_type=F32)


def _ada_kernel(c_ref, cctx_ref, w_ref, b_ref, o_ref):
    bsz = c_ref.shape[0]
    cv = jnp.concatenate([c_ref[...], cctx_ref[...], jnp.zeros((MOD_ROWS - bsz - 1, D_MODEL), F32)], axis=0)
    s = cv * _sigmoid(cv)
    res = _dot(s.astype(BF16), w_ref[...].astype(BF16)) + b_ref[...]
    for k in range(N_MOD):
        @pl.when(pl.program_id(0) == k)
        def _():
            o_ref[:, k, :] = res


def _ada(c, c_ctx, w_ada, b_ada):
    assert w_ada.shape[1] == N_MOD * D_MODEL
    return pl.pallas_call(
        _ada_kernel,
        name="ada",
        grid=(N_MOD,),
        in_specs=[
            pl.BlockSpec(c.shape, lambda j: (0, 0)),
            pl.BlockSpec((1, D_MODEL), lambda j: (0, 0)),
            pl.BlockSpec((D_MODEL, D_MODEL), lambda j: (0, j)),
            pl.BlockSpec((1, D_MODEL), lambda j: (0, j)),
        ],
        out_specs=pl.BlockSpec((MOD_ROWS, N_MOD, D_MODEL), lambda j: (0, 0, 0)),
        out_shape=jax.ShapeDtypeStruct((MOD_ROWS, N_MOD, D_MODEL), F32),
        compiler_params=pltpu.CompilerParams(dimension_semantics=("arbitrary",)),
    )(c, c_ctx.reshape(1, D_MODEL), w_ada, b_ada)


def _tables_kernel(theta_ref, dmat_ref, rowf_ref, rowb_ref, cdt_ref, ctxw_ref, *, ctx_len):
    th = theta_ref[...]
    lg = -(jnp.maximum(-th, 0.0) + jnp.log1p(jnp.exp(-jnp.abs(th))))
    n_i = lax.broadcasted_iota(jnp.int32, (RET_C, RET_C), 0)
    m_i = lax.broadcasted_iota(jnp.int32, (RET_C, RET_C), 1)
    diff = (n_i - m_i).astype(F32)
    idx = lax.broadcasted_iota(jnp.int32, (RET_C, RET_DK), 0).astype(F32)
    midx = lax.broadcasted_iota(jnp.int32, (ctx_len, RET_DK), 0).astype(F32)
    for h in range(RET_HEADS):
        lf = lg[h:h + 1, :]
        lb = lg[RET_HEADS + h:RET_HEADS + h + 1, :]
        lf_c, lb_c = lf[:, :RET_C], lb[:, :RET_C]
        dmat_ref[h] = jnp.exp(jnp.where(diff >= 0.0, diff * lf_c, -diff * lb_c))
        lf_k, lb_k = lf[:, :RET_DK], lb[:, :RET_DK]
        rowf_ref[h, 0] = jnp.exp((idx + 1.0) * lf_k).astype(BF16)
        rowf_ref[h, 1] = jnp.exp((RET_C - 1.0 - idx) * lf_k).astype(BF16)
        rowb_ref[h, 0] = jnp.exp((RET_C - idx) * lb_k).astype(BF16)
        rowb_ref[h, 1] = jnp.exp(idx * lb_k).astype(BF16)
        cdt_ref[h, 0] = jnp.exp(RET_C * lf)
        cdt_ref[h, 1] = jnp.exp(RET_C * lb)
        ctxw_ref[h, 0] = jnp.exp((ctx_len - 1.0 - midx) * lf_k)
        ctxw_ref[h, 1] = jnp.exp(midx * lb_k)


def _ctx_kernel(ctx_ref, modc_ref, gain_ref, w_ref, theta_ref,
                sfw_ref, sbw_ref, dmat_ref, rowf_ref, rowb_ref, cdt_ref, hc_ref, kc_ref, ctxw_ref):
    j = pl.program_id(0)
    bsz, lc, _ = ctx_ref.shape

    @pl.when(j == 0)
    def _():
        _tables_kernel(theta_ref, dmat_ref, rowf_ref, rowb_ref, cdt_ref, ctxw_ref, ctx_len=lc)
        mc = modc_ref[0]
        xc = ctx_ref[...].reshape(bsz * lc, D_MODEL)
        hc_ref[...] = (_rms(xc) * gain_ref[...] * (1.0 + mc[1:2]) + mc[0:1]).astype(BF16)

    p = _dot(hc_ref[...], w_ref[...].astype(BF16))

    @pl.when(j == 0)
    def _():
        kc_ref[...] = p * RET_DK ** -0.5

    heads_per_step = RET_QK // RET_DV
    for step in range(1, RET_HEADS // heads_per_step + 1):
        @pl.when(j == step)
        def _():
            for hh in range(heads_per_step):
                h = (step - 1) * heads_per_step + hh
                for b in range(bsz):
                    rows = slice(b * lc, (b + 1) * lc)
                    kh = kc_ref[rows, h * RET_DK:(h + 1) * RET_DK]
                    vh = p[rows, hh * RET_DV:(hh + 1) * RET_DV].astype(BF16)
                    kfb = jnp.concatenate([(kh * ctxw_ref[h, 0]).astype(BF16),
                                           (kh * ctxw_ref[h, 1]).astype(BF16)], axis=1)
                    s_both = _dot_tn(kfb, vh)
                    sfw_ref[b, hh] = s_both[:RET_DK]
                    sbw_ref[b, hh] = s_both[RET_DK:]


def _ctx_states(ctx, modt, gain, w_in, theta):
    bsz, lc, _ = ctx.shape
    assert OFF_K % RET_QK == 0 and OFF_V == OFF_K + RET_QK and RET_QK % RET_DV == 0
    heads_per_step = RET_QK // RET_DV
    n_steps = 1 + RET_HEADS // heads_per_step
    state = jax.ShapeDtypeStruct((bsz, RET_HEADS, RET_DK, RET_DV), F32)
    sspec = pl.BlockSpec((bsz, heads_per_step, RET_DK, RET_DV), lambda j: (0, jnp.maximum(j - 1, 0), 0, 0))
    rows = jax.ShapeDtypeStruct((RET_HEADS, 2, RET_C, RET_DK), BF16)
    tables = (jax.ShapeDtypeStruct((RET_HEADS, RET_C, RET_C), F32), rows, rows,
              jax.ShapeDtypeStruct((RET_HEADS, 2, 1, RET_DV), F32))

    def whole(shape):
        return pl.BlockSpec(shape, lambda j: (0,) * len(shape))

    s_fw, s_bw, dmat, rowf, rowb, cdt = pl.pallas_call(
        _ctx_kernel,
        name="ctx",
        grid=(n_steps,),
        in_specs=[
            pl.BlockSpec(ctx.shape, lambda j: (0, 0, 0)),
            pl.BlockSpec((1, N_MOD, D_MODEL), lambda j: (bsz, 0, 0)),
            pl.BlockSpec((1, D_MODEL), lambda j: (0, 0)),
            pl.BlockSpec((D_MODEL, RET_QK), lambda j: (0, OFF_K // RET_QK + j)),
            whole(theta.shape),
        ],
        out_specs=(sspec, sspec, *[whole(t.shape) for t in tables]),
        out_shape=(state, state, *tables),
        scratch_shapes=[pltpu.VMEM((bsz * lc, D_MODEL), BF16), pltpu.VMEM((bsz * lc, RET_QK), F32),
                        pltpu.VMEM((RET_HEADS, 2, lc, RET_DK), F32)],
        compiler_params=pltpu.CompilerParams(dimension_semantics=("arbitrary",),
                                             vmem_limit_bytes=VMEM_LIMIT),
    )(ctx, modt, gain, w_in, theta)
    return s_fw, s_bw, dmat, rowf, rowb, cdt


def _inproj_kernel(x_ref, mod_ref, gain_ref, cos_ref, sin_ref, win32_ref, wco32_ref, cw_ref, cb_ref,
                   s0_ref, rowb_ref, cdt_ref,
                   yc_ref, q_ref, k_ref, v_ref, sg_ref, sgb_ref, ob_ref, t_ref, win_ref, wco_ref,
                   *, tiles_per_seq):
    step = pl.program_id(0)
    _cast_chunks(step, ((win32_ref, win_ref, 1, IN_W_CHUNKS), (wco32_ref, wco_ref, 0, CO_W_CHUNKS)))

    @pl.when(step >= IN_W_CHUNKS)
    def _():
        _inproj_tile(x_ref, mod_ref, gain_ref, cos_ref, sin_ref, win_ref, wco_ref, cw_ref, cb_ref,
                     s0_ref, rowb_ref, cdt_ref, yc_ref, q_ref, k_ref, v_ref, sg_ref, sgb_ref, ob_ref,
                     t_ref, (step - IN_W_CHUNKS) % tiles_per_seq == 0)


def _inproj_tile(x_ref, mod_ref, gain_ref, cos_ref, sin_ref, win_ref, wco_ref, cw_ref, cb_ref,
                 s0_ref, rowb_ref, cdt_ref,
                 yc_ref, q_ref, k_ref, v_ref, sg_ref, sgb_ref, ob_ref, t_ref, first_tile):
    @pl.when(first_tile)
    def _():
        t_ref[...] = s0_ref[0]

    m = mod_ref[0]
    h = _rms(x_ref[...]) * gain_ref[...] * (1.0 + m[1:2]) + m[0:1]
    hb = h.astype(BF16)

    def proj(off, width):
        return _dot(hb, win_ref[:, off:off + width])

    u = proj(OFF_CC, CONV_W) * proj(OFF_CX, CONV_W)
    col = lax.broadcasted_iota(jnp.int32, (TM_IN, CONV_W), 0) % GRID_W
    prev = jnp.where(col == 0, 0.0, pltpu.roll(u, 1, 0))
    nxt = jnp.where(col == GRID_W - 1, 0.0, pltpu.roll(u, TM_IN - 1, 0))
    cw = cw_ref[...]
    conv = cw[0:1] * prev + cw[1:2] * u + cw[2:3] * nxt + cb_ref[...]
    z = (proj(OFF_CB, CONV_W) * conv).astype(BF16)
    yc_ref[...] = (_sigmoid(proj(OFF_GA, D_MODEL)) * _dot(z, wco_ref[...])).astype(BF16)
    sgb_ref[...] = _sigmoid(proj(OFF_GB, D_MODEL)).astype(BF16)

    cos = cos_ref[...]
    sin = sin_ref[...]
    half = RET_DK // 2
    for off, ref, scale in ((OFF_Q, q_ref, 1.0), (OFF_K, k_ref, RET_DK ** -0.5)):
        t = proj(off, RET_QK)
        for hh in range(RET_HEADS):
            lo = hh * RET_DK
            t1 = t[:, lo:lo + half]
            t2 = t[:, lo + half:lo + RET_DK]
            ref[:, lo:lo + half] = ((t1 * cos - t2 * sin) * scale).astype(BF16)
            ref[:, lo + half:lo + RET_DK] = ((t2 * cos + t1 * sin) * scale).astype(BF16)
    v_ref[...] = proj(OFF_V, RET_V).astype(BF16)
    g = proj(OFF_G, RET_V)
    sg_ref[...] = (g * _sigmoid(g)).astype(BF16)

    for hh in range(RET_HEADS):
        qs = q_ref[:, hh * RET_DK:(hh + 1) * RET_DK] * rowb_ref[hh, 0]
        ks = k_ref[:, hh * RET_DK:(hh + 1) * RET_DK] * rowb_ref[hh, 1]
        vh = v_ref[:, hh * RET_DV:(hh + 1) * RET_DV]
        t = t_ref[hh]
        ob_ref[:, hh * RET_DV:(hh + 1) * RET_DV] = _dot(qs, t.astype(BF16)).astype(BF16)
        t_ref[hh] = cdt_ref[hh, 1] * t + _dot_tn(ks, vh)


def _inproj(x2, modt, gain, cos, sin, win, wco, cw, cb, s_bw, rowb, cdt, bsz, seq):
    nt = seq // TM_IN

    def tile(i):
        t = jnp.maximum(i - IN_W_CHUNKS, 0)
        return t // nt, nt - 1 - t % nt

    def row_spec(width):
        return pl.BlockSpec((TM_IN, width), lambda i: (tile(i)[0] * nt + tile(i)[1], 0))

    def out(width):
        return jax.ShapeDtypeStruct((bsz * seq, width), BF16)

    rope = pl.BlockSpec((TM_IN, RET_DK // 2), lambda i: (tile(i)[1], 0))
    return pl.pallas_call(
        functools.partial(_inproj_kernel, tiles_per_seq=nt),
        name="inproj",
        grid=(IN_W_CHUNKS + bsz * nt,),
        in_specs=[
            row_spec(D_MODEL),
            pl.BlockSpec((1, N_MOD, D_MODEL), lambda i: (tile(i)[0], 0, 0)),
            pl.BlockSpec((1, D_MODEL), lambda i: (0, 0)),
            rope, rope,
            _chunk_spec(win.shape, IN_W_CHUNKS, 1),
            _chunk_spec(wco.shape, CO_W_CHUNKS, 0),
            pl.BlockSpec(cw.shape, lambda i: (0, 0)),
            pl.BlockSpec((1, CONV_W), lambda i: (0, 0)),
            pl.BlockSpec((1, RET_HEADS, RET_DK, RET_DV), lambda i: (tile(i)[0], 0, 0, 0)),
            _const_spec(rowb.shape),
            _const_spec(cdt.shape),
        ],
        out_specs=(row_spec(D_MODEL), row_spec(RET_QK), row_spec(RET_QK), row_spec(RET_V),
                   row_spec(RET_V), row_spec(D_MODEL), row_spec(RET_V)),
        out_shape=(out(D_MODEL), out(RET_QK), out(RET_QK), out(RET_V), out(RET_V), out(D_MODEL),
                   out(RET_V)),
        scratch_shapes=[pltpu.VMEM((RET_HEADS, RET_DK, RET_DV), F32),
                        pltpu.VMEM(win.shape, BF16), pltpu.VMEM(wco.shape, BF16)],
        compiler_params=pltpu.CompilerParams(dimension_semantics=("arbitrary",),
                                             vmem_limit_bytes=VMEM_LIMIT),
    )(x2, modt, gain, cos, sin, win, wco, cw, cb, s_bw, rowb, cdt)


def _ret_fw_kernel(q_ref, k_ref, v_ref, sg_ref, ob_ref, yc_ref, sgb_ref, x_ref, s0_ref, dmat_ref,
                   rowf_ref, cdt_ref, wro32_ref, wo32_ref, mod_ref, gpost_ref, wup32_ref, wdn32_ref,
                   x1_ref, wup_out_ref, wdn_out_ref, s_ref, gat_ref, wro_ref, wo_ref, *, steps_per_seq):
    step = pl.program_id(0)
    _cast_chunks(step, ((wro32_ref, wro_ref, 0, RET_W_CHUNKS), (wo32_ref, wo_ref, 0, RET_W_CHUNKS)))

    @pl.when(step >= RET_W_CHUNKS)
    def _():
        wup_out_ref[...] = wup32_ref[...].astype(BF16)
        wdn_out_ref[...] = wdn32_ref[...].astype(BF16)
        _ret_fw_tile(q_ref, k_ref, v_ref, sg_ref, ob_ref, yc_ref, sgb_ref, x_ref, s0_ref, dmat_ref,
                     rowf_ref, cdt_ref, wro_ref, wo_ref, mod_ref, gpost_ref, x1_ref, s_ref, gat_ref,
                     (step - RET_W_CHUNKS) % steps_per_seq == 0)


def _ret_fw_tile(q_ref, k_ref, v_ref, sg_ref, ob_ref, yc_ref, sgb_ref, x_ref, s0_ref, dmat_ref,
                 rowf_ref, cdt_ref, wro_ref, wo_ref, mod_ref, gpost_ref,
                 x1_ref, s_ref, gat_ref, first_tile):
    @pl.when(first_tile)
    def _():
        s_ref[...] = s0_ref[0]

    for ci in range(TS_RET // RET_C):
        r0 = ci * RET_C
        for h in range(RET_HEADS):
            qb = q_ref[r0:r0 + RET_C, h * RET_DK:(h + 1) * RET_DK]
            kb = k_ref[r0:r0 + RET_C, h * RET_DK:(h + 1) * RET_DK]
            vh = v_ref[r0:r0 + RET_C, h * RET_DV:(h + 1) * RET_DV]
            scores = (_dot_nt(qb, kb) * dmat_ref[h]).astype(BF16)
            s = s_ref[h]
            o = (_dot(scores, vh) + _dot(qb * rowf_ref[h, 0], s.astype(BF16))
                 + ob_ref[r0:r0 + RET_C, h * RET_DV:(h + 1) * RET_DV].astype(F32))
            s_ref[h] = cdt_ref[h, 0] * s + _dot_tn(kb * rowf_ref[h, 1], vh)
            gate = sg_ref[r0:r0 + RET_C, h * RET_DV:(h + 1) * RET_DV].astype(F32)
            gat_ref[r0:r0 + RET_C, h * RET_DV:(h + 1) * RET_DV] = (gate * _rms(o)).astype(BF16)

    for p in range(len(RET_EPI_BOUNDS) - 1):
        rows = slice(RET_EPI_BOUNDS[p], RET_EPI_BOUNDS[p + 1])
        y_ret = _dot(gat_ref[rows, :], wro_ref[...])
        merged = yc_ref[rows, :].astype(F32) + sgb_ref[rows, :].astype(F32) * y_ret
        y = _dot(merged.astype(BF16), wo_ref[...])
        x1_ref[rows, :] = x_ref[rows, :] + mod_ref[0][2:3] * (_rms(y) * gpost_ref[...])


def _ret_fw(q, k, v, sg, ob, yc, sgb, x2, s_fw, dmat, rowf, cdt, wro, wo, modt, gpost, wup, wdn,
            bsz, seq):
    ns = seq // TS_RET
    passed = [_pass_spec(wup.shape, bsz * ns, RET_W_CHUNKS),
              _pass_spec(wdn.shape, FFN_DN_PASS_STEPS, RET_W_CHUNKS)]

    def tile(i):
        return jnp.maximum(i - RET_W_CHUNKS, 0)

    def row_spec(width):
        return pl.BlockSpec((TS_RET, width), lambda i: (tile(i), 0))

    return pl.pallas_call(
        functools.partial(_ret_fw_kernel, steps_per_seq=ns),
        name="ret_fw",
        grid=(RET_W_CHUNKS + bsz * ns,),
        in_specs=[
            row_spec(RET_QK), row_spec(RET_QK), row_spec(RET_V), row_spec(RET_V), row_spec(RET_V),
            row_spec(D_MODEL), row_spec(D_MODEL), row_spec(D_MODEL),
            pl.BlockSpec((1, RET_HEADS, RET_DK, RET_DV), lambda i: (tile(i) // ns, 0, 0, 0)),
            _const_spec(dmat.shape), _const_spec(rowf.shape), _const_spec(cdt.shape),
            _chunk_spec(wro.shape, RET_W_CHUNKS, 0), _chunk_spec(wo.shape, RET_W_CHUNKS, 0),
            pl.BlockSpec((1, N_MOD, D_MODEL), lambda i: (tile(i) // ns, 0, 0)),
            pl.BlockSpec((1, D_MODEL), lambda i: (0, 0)),
            *passed,
        ],
        out_specs=(row_spec(D_MODEL), *passed),
        out_shape=(jax.ShapeDtypeStruct((bsz * seq, D_MODEL), F32),
                   jax.ShapeDtypeStruct(wup.shape, BF16), jax.ShapeDtypeStruct(wdn.shape, BF16)),
        scratch_shapes=[pltpu.VMEM((RET_HEADS, RET_DK, RET_DV), F32),
                        pltpu.VMEM((TS_RET, RET_V), BF16),
                        pltpu.VMEM(wro.shape, BF16), pltpu.VMEM(wo.shape, BF16)],
        compiler_params=pltpu.CompilerParams(dimension_semantics=("arbitrary",),
                                             vmem_limit_bytes=VMEM_LIMIT),
    )(q, k, v, sg, ob, yc, sgb, x2, s_fw, dmat, rowf, cdt, wro, wo, modt, gpost, wup, wdn)


def _ffn_kernel(x1_ref, mod_ref, gpre_ref, gpost_ref, wup_ref, cw_ref, cb_ref, wdn_ref,
                out_ref, acc_ref, *, grid_rows):
    tm = grid_rows * FFN_COLS
    x1 = x1_ref[0].reshape(tm, D_MODEL)
    m = mod_ref[0]
    hb = (_rms(x1) * gpre_ref[...] * (1.0 + m[4:5]) + m[3:4]).astype(BF16)

    def conv(u, off, width):
        w = cw_ref[:, off:off + width]
        pad = jnp.zeros((FFN_COLS, width), F32)
        above = jnp.concatenate([pad, u[:tm - FFN_COLS]], axis=0)
        below = jnp.concatenate([u[FFN_COLS:], pad], axis=0)
        return w[0:1] * above + w[1:2] * u + w[2:3] * below + cb_ref[:, off:off + width]

    def up(c0, width):
        return (_dot(hb, wup_ref[:, c0:c0 + width]),
                _dot(hb, wup_ref[:, D_FF + c0:D_FF + c0 + width]))

    ahead = up(*FFN_BLOCKS[0])
    for j, (c0, width) in enumerate(FFN_BLOCKS):
        ug, uv = ahead
        if j + 1 < len(FFN_BLOCKS):
            ahead = up(*FFN_BLOCKS[j + 1])
        g = conv(ug, c0, width)
        val = conv(uv, D_FF + c0, width)
        a = (g * _sigmoid(g) * val).astype(BF16)
        if j == 0:
            acc_ref[...] = _dot(a, wdn_ref[c0:c0 + width, :])
        elif j + 1 < len(FFN_BLOCKS):
            acc_ref[...] += _dot(a, wdn_ref[c0:c0 + width, :])
        else:
            gr = grid_rows // FFN_EPI_PIECES
            for p in range(FFN_EPI_PIECES):
                rows = slice(p * gr * FFN_COLS, (p + 1) * gr * FFN_COLS)
                f = acc_ref[rows, :] + _dot(a[rows, :], wdn_ref[c0:c0 + width, :])
                res = x1[rows, :] + m[5:6] * (_rms(f) * gpost_ref[...])
                out_ref[0, p * gr:(p + 1) * gr] = res.reshape(gr, FFN_COLS, D_MODEL)


def _ffn(x1g, modt, gpre, gpost, wup, cw, cb, wdn):
    bsz, grid_rows, grid_w, _ = x1g.shape
    strips = grid_w // FFN_COLS
    strip = pl.BlockSpec((1, grid_rows, FFN_COLS, D_MODEL), lambda i: (i // strips, 0, i % strips, 0))
    vec = pl.BlockSpec((1, D_MODEL), lambda i: (0, 0))
    return pl.pallas_call(
        functools.partial(_ffn_kernel, grid_rows=grid_rows),
        name="ffn",
        grid=(bsz * strips,),
        in_specs=[
            strip,
            pl.BlockSpec((1, N_MOD, D_MODEL), lambda i: (i // strips, 0, 0)),
            vec, vec,
            _const_spec(wup.shape),
            pl.BlockSpec(cw.shape, lambda i: (0, 0)),
            pl.BlockSpec((1, 2 * D_FF), lambda i: (0, 0)),
            _const_spec(wdn.shape),
        ],
        out_specs=strip,
        out_shape=jax.ShapeDtypeStruct(x1g.shape, F32),
        scratch_shapes=[pltpu.VMEM((grid_rows * FFN_COLS, D_MODEL), F32)],
        compiler_params=pltpu.CompilerParams(dimension_semantics=("arbitrary",),
                                             vmem_limit_bytes=VMEM_LIMIT),
    )(x1g, modt, gpre, gpost, wup, cw, cb, wdn)


def _rotary_tables(n):
    t = np.arange(n)
    row = (t // GRID_W).astype(np.float64)
    col = (t % GRID_W).astype(np.float64)
    inv = ROPE_BASE ** (-np.arange(ROPE_PAIRS, dtype=np.float64) / ROPE_PAIRS)
    ang = np.concatenate([row[:, None] * inv, col[:, None] * inv], axis=-1)
    return jnp.asarray(np.cos(ang), F32), jnp.asarray(np.sin(ang), F32)


def kernel(x, c, ctx, c_ctx, w_ada, b_ada, norm_mix_pre, norm_mix_post, w_in, conv_w, conv_b,
           w_conv_out, ret_decay_fw, ret_decay_bw, w_ret_out, w_o, norm_ffn_pre, norm_ffn_post,
           w_ffn_up, ffn_conv_w, ffn_conv_b, w_ffn_down):
    bsz, seq, d = x.shape
    depth = w_ada.shape[0]
    assert d == D_MODEL and depth == 1 and bsz + 1 <= MOD_ROWS
    assert seq % TS_RET == 0 and seq % TM_IN == 0 and TM_IN % GRID_W == 0 and GRID_W % FFN_COLS == 0
    lc = ctx.shape[1]
    cos, sin = _rotary_tables(seq)
    x2 = x.reshape(bsz * seq, d)

    modt = _ada(c, c_ctx, w_ada[0], b_ada)

    theta = jnp.concatenate([ret_decay_fw[0], ret_decay_bw[0]]).astype(F32)
    theta = jnp.broadcast_to(theta[:, None], (2 * RET_HEADS, RET_DV))
    s_fw, s_bw, dmat, rowf, rowb, cdt = _ctx_states(ctx, modt, norm_mix_pre, w_in[0], theta)

    yc, q, k, v, sg, sgb, ob = _inproj(x2, modt, norm_mix_pre, cos, sin, w_in[0], w_conv_out[0],
                                       conv_w[0], conv_b, s_bw, rowb, cdt, bsz, seq)
    x1, wup, wdn = _ret_fw(q, k, v, sg, ob, yc, sgb, x2, s_fw, dmat, rowf, cdt, w_ret_out[0], w_o[0],
                           modt, norm_mix_post, w_ffn_up[0], w_ffn_down[0], bsz, seq)
    out = _ffn(x1.reshape(bsz, seq // GRID_W, GRID_W, d), modt, norm_ffn_pre, norm_ffn_post,
               wup, ffn_conv_w[0], ffn_conv_b, wdn)
    return out.reshape(bsz, seq, d)
```

```python
import functools

import jax
import jax.numpy as jnp
import numpy as np
from jax import lax
from jax.experimental import pallas as pl
from jax.experimental.pallas import tpu as pltpu

F32 = jnp.float32
BF16 = jnp.bfloat16

D_MODEL = 1024
GRID_W = 64
CONV_W = 1024
RET_HEADS = 4
RET_DK = 256
RET_DV = 512
RET_QK = RET_HEADS * RET_DK
RET_V = RET_HEADS * RET_DV
ROPE_PAIRS = RET_DK // 4
ROPE_BASE = 10000.0
D_FF = 2816
N_MOD = 6
EPS = 1e-6

OFF_CX = 0
OFF_CB = OFF_CX + CONV_W
OFF_CC = OFF_CB + CONV_W
OFF_Q = OFF_CC + CONV_W
OFF_K = OFF_Q + RET_QK
OFF_V = OFF_K + RET_QK
OFF_G = OFF_V + RET_V
OFF_GA = OFF_G + RET_V
OFF_GB = OFF_GA + D_MODEL
IN_COLS = OFF_GB + D_MODEL

RET_C = 256
TM_IN = RET_C
TS_RET = 512
RET_EPI_BOUNDS = (0, TS_RET // 2, TS_RET)
FFN_COLS = 8
FFN_BLOCKS = ((0, 1024), (1024, 1024), (2048, 768))
FFN_EPI_PIECES = 4
IN_W_CHUNKS = 11
CO_W_CHUNKS = 8
RET_W_CHUNKS = 4
FFN_DN_PASS_STEPS = 16
MOD_ROWS = 8
CTX_FIRST_STEP = 2
BF16_SUBLANES = 16
V7X_VMEM_BYTES = 64 * 1024 * 1024
VMEM_LIMIT = V7X_VMEM_BYTES - 4 * 1024 * 1024


def _const_spec(shape):
    zeros = (0,) * len(shape)
    return pl.BlockSpec(shape, lambda *_: zeros, pipeline_mode=pl.Buffered(1))


def _chunk_spec(shape, n_chunks, axis):
    block = list(shape)
    block[axis] //= n_chunks
    assert block[axis] * n_chunks == shape[axis]

    def index_map(i):
        c = jnp.minimum(i, n_chunks - 1)
        return (c, 0) if axis == 0 else (0, c)

    return pl.BlockSpec(tuple(block), index_map)


def _pass_spec(shape, n_steps, first_step=0):
    rows = shape[0] // n_steps
    assert rows * n_steps == shape[0] and rows % BF16_SUBLANES == 0
    return pl.BlockSpec((rows, shape[1]), lambda i: (jnp.clip(i - first_step, 0, n_steps - 1), 0))


def _cast_chunks(step, weights):
    n_steps = max(n for _, _, _, n in weights)

    @pl.when(step < n_steps)
    def _():
        for c in range(n_steps):
            @pl.when(step == c)
            def _():
                for src_ref, dst_ref, axis, n in weights:
                    if c < n:
                        size = src_ref.shape[axis]
                        if axis == 0:
                            dst_ref[c * size:(c + 1) * size, :] = src_ref[...].astype(BF16)
                        else:
                            dst_ref[:, c * size:(c + 1) * size] = src_ref[...].astype(BF16)


def _rms(xf):
    return xf * lax.rsqrt(jnp.mean(xf * xf, axis=-1, keepdims=True) + EPS)


def _sigmoid(x):
    return 1.0 / (1.0 + jnp.exp(-x))


def _dot(a, b):
    return jnp.dot(a, b, preferred_element_type=F32)


def _dot_nt(a, b):
    return lax.dot_general(a, b, (((1,), (1,)), ((), ())), preferred_element_type=F32)


def _dot_tn(a, b):
    return lax.dot_general(a, b, (((0,), (0,)), ((), ())), preferred_element_type=F32)


def _tables_kernel(theta_ref, dmat_ref, rowf_ref, rowb_ref, cdt_ref, ctxw_ref, *, ctx_len):
    th = theta_ref[...]
    lg = -(jnp.maximum(-th, 0.0) + jnp.log1p(jnp.exp(-jnp.abs(th))))
    n_i = lax.broadcasted_iota(jnp.int32, (RET_C, RET_C), 0)
    m_i = lax.broadcasted_iota(jnp.int32, (RET_C, RET_C), 1)
    diff = (n_i - m_i).astype(F32)
    idx = lax.broadcasted_iota(jnp.int32, (RET_C, RET_DK), 0).astype(F32)
    tok = lax.broadcasted_iota(jnp.int32, (RET_DK, RET_C), 1).astype(F32)
    midx = lax.broadcasted_iota(jnp.int32, (ctx_len, RET_DK), 0).astype(F32)
    for h in range(RET_HEADS):
        lf = lg[h:h + 1, :]
        lb = lg[RET_HEADS + h:RET_HEADS + h + 1, :]
        lf_c, lb_c = lf[:, :RET_C], lb[:, :RET_C]
        dmat_ref[h] = jnp.exp(jnp.where(diff >= 0.0, diff * lf_c, -diff * lb_c))
        lf_k, lb_k = lf[:, :RET_DK], lb[:, :RET_DK]
        rowf_ref[h, 0] = jnp.exp((idx + 1.0) * lf_k).astype(BF16)
        rowf_ref[h, 1] = jnp.exp((RET_C - 1.0 - tok) * lf_k).astype(BF16)
        rowb_ref[h, 0] = jnp.exp((RET_C - idx) * lb_k).astype(BF16)
        rowb_ref[h, 1] = jnp.exp(tok * lb_k).astype(BF16)
        cdt_ref[h, 0] = jnp.exp(RET_C * lf)
        cdt_ref[h, 1] = jnp.exp(RET_C * lb)
        ctxw_ref[h, 0] = jnp.exp((ctx_len - 1.0 - midx) * lf_k)
        ctxw_ref[h, 1] = jnp.exp(midx * lb_k)


def _prep_kernel(c_ref, cctx_ref, wada_ref, bada_ref, ctx_ref, gain_ref, w_ref, theta_ref,
                 mod_ref, sfw_ref, sbw_ref, dmat_ref, rowf_ref, rowb_ref, cdt_ref,
                 hc_ref, kc_ref, ctxw_ref):
    j = pl.program_id(0)
    bsz, lc, _ = ctx_ref.shape

    cv = jnp.concatenate([c_ref[...], cctx_ref[...], jnp.zeros((MOD_ROWS - bsz - 1, D_MODEL), F32)], axis=0)
    s = cv * _sigmoid(cv)
    res = _dot(s.astype(BF16), wada_ref[...].astype(BF16)) + bada_ref[...]
    for k in range(N_MOD):
        @pl.when(j == k)
        def _():
            mod_ref[:, k, :] = res

    heads_per_step = RET_QK // RET_DV

    def project():
        return _dot(hc_ref[...], w_ref[...].astype(BF16))

    @pl.when(j == CTX_FIRST_STEP)
    def _():
        _tables_kernel(theta_ref, dmat_ref, rowf_ref, rowb_ref, cdt_ref, ctxw_ref, ctx_len=lc)
        shift = mod_ref[bsz:bsz + 1, 0, :]
        scale = mod_ref[bsz:bsz + 1, 1, :]
        xc = ctx_ref[...].reshape(bsz * lc, D_MODEL)
        hc_ref[...] = (_rms(xc) * gain_ref[...] * (1.0 + scale) + shift).astype(BF16)
        kc_ref[...] = project() * RET_DK ** -0.5

    for step in range(1, RET_HEADS // heads_per_step + 1):
        @pl.when(j == CTX_FIRST_STEP + step)
        def _():
            p = project()
            for hh in range(heads_per_step):
                h = (step - 1) * heads_per_step + hh
                for b in range(bsz):
                    rows = slice(b * lc, (b + 1) * lc)
                    kh = kc_ref[rows, h * RET_DK:(h + 1) * RET_DK]
                    vh = p[rows, hh * RET_DV:(hh + 1) * RET_DV].astype(BF16)
                    kfb = jnp.concatenate([(kh * ctxw_ref[h, 0]).astype(BF16),
                                           (kh * ctxw_ref[h, 1]).astype(BF16)], axis=1)
                    s_both = _dot_tn(kfb, vh)
                    sfw_ref[b, hh] = s_both[:RET_DK]
                    sbw_ref[b, hh] = s_both[RET_DK:]


def _prep(c, c_ctx, w_ada, b_ada, ctx, gain, w_in, theta):
    bsz, lc, _ = ctx.shape
    assert w_ada.shape[1] == N_MOD * D_MODEL
    assert OFF_K % RET_QK == 0 and OFF_V == OFF_K + RET_QK and RET_QK % RET_DV == 0
    heads_per_step = RET_QK // RET_DV
    ctx_steps = 1 + RET_HEADS // heads_per_step
    assert CTX_FIRST_STEP + ctx_steps <= N_MOD
    state = jax.ShapeDtypeStruct((bsz, RET_HEADS, RET_DK, RET_DV), F32)
    sspec = pl.BlockSpec((bsz, heads_per_step, RET_DK, RET_DV),
                         lambda j: (0, jnp.clip(j - CTX_FIRST_STEP - 1, 0, ctx_steps - 2), 0, 0))
    rows = jax.ShapeDtypeStruct((RET_HEADS, 2, RET_C, RET_DK), BF16)
    mod = jax.ShapeDtypeStruct((MOD_ROWS, N_MOD, D_MODEL), F32)
    tables = (jax.ShapeDtypeStruct((RET_HEADS, RET_C, RET_C), F32), rows, rows,
              jax.ShapeDtypeStruct((RET_HEADS, 2, 1, RET_DV), F32))

    def whole(shape):
        return pl.BlockSpec(shape, lambda j: (0,) * len(shape))

    return pl.pallas_call(
        _prep_kernel,
        name="prep",
        grid=(N_MOD,),
        in_specs=[
            whole(c.shape),
            whole((1, D_MODEL)),
            pl.BlockSpec((D_MODEL, D_MODEL), lambda j: (0, j)),
            pl.BlockSpec((1, D_MODEL), lambda j: (0, j)),
            whole(ctx.shape),
            whole((1, D_MODEL)),
            pl.BlockSpec((D_MODEL, RET_QK),
                         lambda j: (0, OFF_K // RET_QK + jnp.clip(j - CTX_FIRST_STEP, 0, ctx_steps - 1))),
            whole(theta.shape),
        ],
        out_specs=(whole(mod.shape), sspec, sspec, *[whole(t.shape) for t in tables]),
        out_shape=(mod, state, state, *tables),
        scratch_shapes=[pltpu.VMEM((bsz * lc, D_MODEL), BF16), pltpu.VMEM((bsz * lc, RET_QK), F32),
                        pltpu.VMEM((RET_HEADS, 2, lc, RET_DK), F32)],
        compiler_params=pltpu.CompilerParams(dimension_semantics=("arbitrary",),
                                             vmem_limit_bytes=VMEM_LIMIT),
    )(c, c_ctx.reshape(1, D_MODEL), w_ada, b_ada, ctx, gain, w_in, theta)


def _inproj_kernel(x_ref, mod_ref, gain_ref, cos_ref, sin_ref, win32_ref, wco32_ref, cw_ref, cb_ref,
                   s0_ref, rowb_ref, cdt_ref,
                   yc_ref, q_ref, kt_ref, v_ref, sg_ref, sgb_ref, ob_ref, t_ref, win_ref, wco_ref,
                   *, tiles_per_seq):
    step = pl.program_id(0)
    _cast_chunks(step, ((win32_ref, win_ref, 1, IN_W_CHUNKS), (wco32_ref, wco_ref, 0, CO_W_CHUNKS)))

    @pl.when(step >= IN_W_CHUNKS)
    def _():
        _inproj_tile(x_ref, mod_ref, gain_ref, cos_ref, sin_ref, win_ref, wco_ref, cw_ref, cb_ref,
                     s0_ref, rowb_ref, cdt_ref, yc_ref, q_ref, kt_ref, v_ref, sg_ref, sgb_ref, ob_ref,
                     t_ref, (step - IN_W_CHUNKS) % tiles_per_seq == 0)


def _inproj_tile(x_ref, mod_ref, gain_ref, cos_ref, sin_ref, win_ref, wco_ref, cw_ref, cb_ref,
                 s0_ref, rowb_ref, cdt_ref,
                 yc_ref, q_ref, kt_ref, v_ref, sg_ref, sgb_ref, ob_ref, t_ref, first_tile):
    @pl.when(first_tile)
    def _():
        t_ref[...] = s0_ref[0]

    m = mod_ref[0]
    h = _rms(x_ref[...]) * gain_ref[...] * (1.0 + m[1:2]) + m[0:1]
    hb = h.astype(BF16)

    def proj(off, width):
        return _dot(hb, win_ref[:, off:off + width])

    u = proj(OFF_CC, CONV_W) * proj(OFF_CX, CONV_W)
    col = lax.broadcasted_iota(jnp.int32, (TM_IN, CONV_W), 0) % GRID_W
    prev = jnp.where(col == 0, 0.0, pltpu.roll(u, 1, 0))
    nxt = jnp.where(col == GRID_W - 1, 0.0, pltpu.roll(u, TM_IN - 1, 0))
    cw = cw_ref[...]
    conv = cw[0:1] * prev + cw[1:2] * u + cw[2:3] * nxt + cb_ref[...]
    z = (proj(OFF_CB, CONV_W) * conv).astype(BF16)
    yc_ref[...] = (_sigmoid(proj(OFF_GA, D_MODEL)) * _dot(z, wco_ref[...])).astype(BF16)
    sgb_ref[...] = _sigmoid(proj(OFF_GB, D_MODEL)).astype(BF16)

    cos = cos_ref[...]
    sin = sin_ref[...]
    half = RET_DK // 2
    for off, ref, scale in ((OFF_Q, q_ref, 1.0), (OFF_K, kt_ref, RET_DK ** -0.5)):
        t = proj(off, RET_QK)
        for hh in range(RET_HEADS):
            lo = hh * RET_DK
            t1 = t[:, lo:lo + half]
            t2 = t[:, lo + half:lo + RET_DK]
            r1 = (t1 * cos - t2 * sin) * scale
            r2 = (t2 * cos + t1 * sin) * scale
            if ref is q_ref:
                ref[:, lo:lo + half] = r1.astype(BF16)
                ref[:, lo + half:lo + RET_DK] = r2.astype(BF16)
            else:
                ref[lo:lo + half, :] = r1.T.astype(BF16)
                ref[lo + half:lo + RET_DK, :] = r2.T.astype(BF16)
    v_ref[...] = proj(OFF_V, RET_V).astype(BF16)
    g = proj(OFF_G, RET_V)
    sg_ref[...] = (g * _sigmoid(g)).astype(BF16)

    for hh in range(RET_HEADS):
        qs = q_ref[:, hh * RET_DK:(hh + 1) * RET_DK] * rowb_ref[hh, 0]
        kst = kt_ref[hh * RET_DK:(hh + 1) * RET_DK, :] * rowb_ref[hh, 1]
        vh = v_ref[:, hh * RET_DV:(hh + 1) * RET_DV]
        t = t_ref[hh]
        ob_ref[:, hh * RET_DV:(hh + 1) * RET_DV] = _dot(qs, t.astype(BF16)).astype(BF16)
        t_ref[hh] = cdt_ref[hh, 1] * t + _dot(kst, vh)


def _inproj(x2, modt, gain, cos, sin, win, wco, cw, cb, s_bw, rowb, cdt, bsz, seq):
    nt = seq // TM_IN

    def tile(i):
        t = jnp.maximum(i - IN_W_CHUNKS, 0)
        return t // nt, nt - 1 - t % nt

    def row_spec(width):
        return pl.BlockSpec((TM_IN, width), lambda i: (tile(i)[0] * nt + tile(i)[1], 0))

    def out(width):
        return jax.ShapeDtypeStruct((bsz * seq, width), BF16)

    rope = pl.BlockSpec((TM_IN, RET_DK // 2), lambda i: (tile(i)[1], 0))
    return pl.pallas_call(
        functools.partial(_inproj_kernel, tiles_per_seq=nt),
        name="inproj",
        grid=(IN_W_CHUNKS + bsz * nt,),
        in_specs=[
            row_spec(D_MODEL),
            pl.BlockSpec((1, N_MOD, D_MODEL), lambda i: (tile(i)[0], 0, 0)),
            pl.BlockSpec((1, D_MODEL), lambda i: (0, 0)),
            rope, rope,
            _chunk_spec(win.shape, IN_W_CHUNKS, 1),
            _chunk_spec(wco.shape, CO_W_CHUNKS, 0),
            pl.BlockSpec(cw.shape, lambda i: (0, 0)),
            pl.BlockSpec((1, CONV_W), lambda i: (0, 0)),
            pl.BlockSpec((1, RET_HEADS, RET_DK, RET_DV), lambda i: (tile(i)[0], 0, 0, 0)),
            _const_spec(rowb.shape),
            _const_spec(cdt.shape),
        ],
        out_specs=(row_spec(D_MODEL), row_spec(RET_QK),
                   pl.BlockSpec((RET_QK, TM_IN), lambda i: (0, tile(i)[0] * nt + tile(i)[1])),
                   row_spec(RET_V), row_spec(RET_V), row_spec(D_MODEL), row_spec(RET_V)),
        out_shape=(out(D_MODEL), out(RET_QK), jax.ShapeDtypeStruct((RET_QK, bsz * seq), BF16),
                   out(RET_V), out(RET_V), out(D_MODEL), out(RET_V)),
        scratch_shapes=[pltpu.VMEM((RET_HEADS, RET_DK, RET_DV), F32),
                        pltpu.VMEM(win.shape, BF16), pltpu.VMEM(wco.shape, BF16)],
        compiler_params=pltpu.CompilerParams(dimension_semantics=("arbitrary",),
                                             vmem_limit_bytes=VMEM_LIMIT),
    )(x2, modt, gain, cos, sin, win, wco, cw, cb, s_bw, rowb, cdt)


def _ret_fw_kernel(q_ref, kt_ref, v_ref, sg_ref, ob_ref, yc_ref, sgb_ref, x_ref, s0_ref, dmat_ref,
                   rowf_ref, cdt_ref, wro32_ref, wo32_ref, mod_ref, gpost_ref, wup32_ref, wdn32_ref,
                   x1_ref, wup_out_ref, wdn_out_ref, s_ref, gat_ref, wro_ref, wo_ref, *, steps_per_seq):
    step = pl.program_id(0)
    _cast_chunks(step, ((wro32_ref, wro_ref, 0, RET_W_CHUNKS), (wo32_ref, wo_ref, 0, RET_W_CHUNKS)))

    @pl.when(step >= RET_W_CHUNKS)
    def _():
        wup_out_ref[...] = wup32_ref[...].astype(BF16)
        wdn_out_ref[...] = wdn32_ref[...].astype(BF16)
        _ret_fw_tile(q_ref, kt_ref, v_ref, sg_ref, ob_ref, yc_ref, sgb_ref, x_ref, s0_ref, dmat_ref,
                     rowf_ref, cdt_ref, wro_ref, wo_ref, mod_ref, gpost_ref, x1_ref, s_ref, gat_ref,
                     (step - RET_W_CHUNKS) % steps_per_seq == 0)


def _ret_fw_tile(q_ref, kt_ref, v_ref, sg_ref, ob_ref, yc_ref, sgb_ref, x_ref, s0_ref, dmat_ref,
                 rowf_ref, cdt_ref, wro_ref, wo_ref, mod_ref, gpost_ref,
                 x1_ref, s_ref, gat_ref, first_tile):
    @pl.when(first_tile)
    def _():
        s_ref[...] = s0_ref[0]

    for ci in range(TS_RET // RET_C):
        r0 = ci * RET_C
        for h in range(RET_HEADS):
            qb = q_ref[r0:r0 + RET_C, h * RET_DK:(h + 1) * RET_DK]
            kt = kt_ref[h * RET_DK:(h + 1) * RET_DK, r0:r0 + RET_C]
            vh = v_ref[r0:r0 + RET_C, h * RET_DV:(h + 1) * RET_DV]
            scores = (_dot(qb, kt) * dmat_ref[h]).astype(BF16)
            s = s_ref[h]
            o = (_dot(scores, vh) + _dot(qb * rowf_ref[h, 0], s.astype(BF16))
                 + ob_ref[r0:r0 + RET_C, h * RET_DV:(h + 1) * RET_DV].astype(F32))
            s_ref[h] = cdt_ref[h, 0] * s + _dot(kt * rowf_ref[h, 1], vh)
            gate = sg_ref[r0:r0 + RET_C, h * RET_DV:(h + 1) * RET_DV].astype(F32)
            gat_ref[r0:r0 + RET_C, h * RET_DV:(h + 1) * RET_DV] = (gate * _rms(o)).astype(BF16)

    for p in range(len(RET_EPI_BOUNDS) - 1):
        rows = slice(RET_EPI_BOUNDS[p], RET_EPI_BOUNDS[p + 1])
        y_ret = _dot(gat_ref[rows, :], wro_ref[...])
        merged = yc_ref[rows, :].astype(F32) + sgb_ref[rows, :].astype(F32) * y_ret
        y = _dot(merged.astype(BF16), wo_ref[...])
        x1_ref[rows, :] = x_ref[rows, :] + mod_ref[0][2:3] * (_rms(y) * gpost_ref[...])


def _ret_fw(q, k, v, sg, ob, yc, sgb, x2, s_fw, dmat, rowf, cdt, wro, wo, modt, gpost, wup, wdn,
            bsz, seq):
    ns = seq // TS_RET
    passed = [_pass_spec(wup.shape, bsz * ns, RET_W_CHUNKS),
              _pass_spec(wdn.shape, FFN_DN_PASS_STEPS, RET_W_CHUNKS)]

    def tile(i):
        return jnp.maximum(i - RET_W_CHUNKS, 0)

    def row_spec(width):
        return pl.BlockSpec((TS_RET, width), lambda i: (tile(i), 0))

    return pl.pallas_call(
        functools.partial(_ret_fw_kernel, steps_per_seq=ns),
        name="ret_fw",
        grid=(RET_W_CHUNKS + bsz * ns,),
        in_specs=[
            row_spec(RET_QK), pl.BlockSpec((RET_QK, TS_RET), lambda i: (0, tile(i))),
            row_spec(RET_V), row_spec(RET_V), row_spec(RET_V),
            row_spec(D_MODEL), row_spec(D_MODEL), row_spec(D_MODEL),
            pl.BlockSpec((1, RET_HEADS, RET_DK, RET_DV), lambda i: (tile(i) // ns, 0, 0, 0)),
            _const_spec(dmat.shape), _const_spec(rowf.shape), _const_spec(cdt.shape),
            _chunk_spec(wro.shape, RET_W_CHUNKS, 0), _chunk_spec(wo.shape, RET_W_CHUNKS, 0),
            pl.BlockSpec((1, N_MOD, D_MODEL), lambda i: (tile(i) // ns, 0, 0)),
            pl.BlockSpec((1, D_MODEL), lambda i: (0, 0)),
            *passed,
        ],
        out_specs=(row_spec(D_MODEL), *passed),
        out_shape=(jax.ShapeDtypeStruct((bsz * seq, D_MODEL), F32),
                   jax.ShapeDtypeStruct(wup.shape, BF16), jax.ShapeDtypeStruct(wdn.shape, BF16)),
        scratch_shapes=[pltpu.VMEM((RET_HEADS, RET_DK, RET_DV), F32),
                        pltpu.VMEM((TS_RET, RET_V), BF16),
                        pltpu.VMEM(wro.shape, BF16), pltpu.VMEM(wo.shape, BF16)],
        compiler_params=pltpu.CompilerParams(dimension_semantics=("arbitrary",),
                                             vmem_limit_bytes=VMEM_LIMIT),
    )(q, k, v, sg, ob, yc, sgb, x2, s_fw, dmat, rowf, cdt, wro, wo, modt, gpost, wup, wdn)


def _ffn_kernel(x1_ref, mod_ref, gpre_ref, gpost_ref, wup_ref, cw_ref, cb_ref, wdn_ref,
                out_ref, acc_ref, *, grid_rows):
    tm = grid_rows * FFN_COLS
    x1 = x1_ref[0].reshape(tm, D_MODEL)
    m = mod_ref[0]
    hb = (_rms(x1) * gpre_ref[...] * (1.0 + m[4:5]) + m[3:4]).astype(BF16)

    def conv(u, off, width):
        w = cw_ref[:, off:off + width]
        pad = jnp.zeros((FFN_COLS, width), F32)
        above = jnp.concatenate([pad, u[:tm - FFN_COLS]], axis=0)
        below = jnp.concatenate([u[FFN_COLS:], pad], axis=0)
        return w[0:1] * above + w[1:2] * u + w[2:3] * below + cb_ref[:, off:off + width]

    def up(c0, width):
        return (_dot(hb, wup_ref[:, c0:c0 + width]),
                _dot(hb, wup_ref[:, D_FF + c0:D_FF + c0 + width]))

    ahead = up(*FFN_BLOCKS[0])
    for j, (c0, width) in enumerate(FFN_BLOCKS):
        ug, uv = ahead
        if j + 1 < len(FFN_BLOCKS):
            ahead = up(*FFN_BLOCKS[j + 1])
        g = conv(ug, c0, width)
        val = conv(uv, D_FF + c0, width)
        a = (g * _sigmoid(g) * val).astype(BF16)
        if j == 0:
            acc_ref[...] = _dot(a, wdn_ref[c0:c0 + width, :])
        elif j + 1 < len(FFN_BLOCKS):
            acc_ref[...] += _dot(a, wdn_ref[c0:c0 + width, :])
        else:
            gr = grid_rows // FFN_EPI_PIECES
            for p in range(FFN_EPI_PIECES):
                rows = slice(p * gr * FFN_COLS, (p + 1) * gr * FFN_COLS)
                f = acc_ref[rows, :] + _dot(a[rows, :], wdn_ref[c0:c0 + width, :])
                res = x1[rows, :] + m[5:6] * (_rms(f) * gpost_ref[...])
                out_ref[0, p * gr:(p + 1) * gr] = res.reshape(gr, FFN_COLS, D_MODEL)


def _ffn(x1g, modt, gpre, gpost, wup, cw, cb, wdn):
    bsz, grid_rows, grid_w, _ = x1g.shape
    strips = grid_w // FFN_COLS
    strip = pl.BlockSpec((1, grid_rows, FFN_COLS, D_MODEL), lambda i: (i // strips, 0, i % strips, 0))
    vec = pl.BlockSpec((1, D_MODEL), lambda i: (0, 0))
    return pl.pallas_call(
        functools.partial(_ffn_kernel, grid_rows=grid_rows),
        name="ffn",
        grid=(bsz * strips,),
        in_specs=[
            strip,
            pl.BlockSpec((1, N_MOD, D_MODEL), lambda i: (i // strips, 0, 0)),
            vec, vec,
            _const_spec(wup.shape),
            pl.BlockSpec(cw.shape, lambda i: (0, 0)),
            pl.BlockSpec((1, 2 * D_FF), lambda i: (0, 0)),
            _const_spec(wdn.shape),
        ],
        out_specs=strip,
        out_shape=jax.ShapeDtypeStruct(x1g.shape, F32),
        scratch_shapes=[pltpu.VMEM((grid_rows * FFN_COLS, D_MODEL), F32)],
        compiler_params=pltpu.CompilerParams(dimension_semantics=("arbitrary",),
                                             vmem_limit_bytes=VMEM_LIMIT),
    )(x1g, modt, gpre, gpost, wup, cw, cb, wdn)


def _rotary_tables(n):
    t = np.arange(n)
    row = (t // GRID_W).astype(np.float64)
    col = (t % GRID_W).astype(np.float64)
    inv = ROPE_BASE ** (-np.arange(ROPE_PAIRS, dtype=np.float64) / ROPE_PAIRS)
    ang = np.concatenate([row[:, None] * inv, col[:, None] * inv], axis=-1)
    return jnp.asarray(np.cos(ang), F32), jnp.asarray(np.sin(ang), F32)


def kernel(x, c, ctx, c_ctx, w_ada, b_ada, norm_mix_pre, norm_mix_post, w_in, conv_w, conv_b,
           w_conv_out, ret_decay_fw, ret_decay_bw, w_ret_out, w_o, norm_ffn_pre, norm_ffn_post,
           w_ffn_up, ffn_conv_w, ffn_conv_b, w_ffn_down):
    bsz, seq, d = x.shape
    depth = w_ada.shape[0]
    assert d == D_MODEL and depth == 1 and bsz + 1 <= MOD_ROWS
    assert seq % TS_RET == 0 and seq % TM_IN == 0 and TM_IN % GRID_W == 0 and GRID_W % FFN_COLS == 0
    lc = ctx.shape[1]
    cos, sin = _rotary_tables(seq)
    x2 = x.reshape(bsz * seq, d)


    theta = jnp.concatenate([ret_decay_fw[0], ret_decay_bw[0]]).astype(F32)
    theta = jnp.broadcast_to(theta[:, None], (2 * RET_HEADS, RET_DV))
    modt, s_fw, s_bw, dmat, rowf, rowb, cdt = _prep(c, c_ctx, w_ada[0], b_ada, ctx, norm_mix_pre,
                                                    w_in[0], theta)

    yc, q, k, v, sg, sgb, ob = _inproj(x2, modt, norm_mix_pre, cos, sin, w_in[0], w_conv_out[0],
                                       conv_w[0], conv_b, s_bw, rowb, cdt, bsz, seq)
    x1, wup, wdn = _ret_fw(q, k, v, sg, ob, yc, sgb, x2, s_fw, dmat, rowf, cdt, w_ret_out[0], w_o[0],
                           modt, norm_mix_post, w_ffn_up[0], w_ffn_down[0], bsz, seq)
    out = _ffn(x1.reshape(bsz, seq // GRID_W, GRID_W, d), modt, norm_ffn_pre, norm_ffn_post,
               wup, ffn_conv_w[0], ffn_conv_b, wdn)
    return out.reshape(bsz, seq, d)
```

```python
import functools

import jax
import jax.numpy as jnp
import numpy as np
from jax import lax
from jax.experimental import pallas as pl
from jax.experimental.pallas import tpu as pltpu

F32 = jnp.float32
BF16 = jnp.bfloat16

D_MODEL = 1024
GRID_W = 64
CONV_W = 1024
RET_HEADS = 4
RET_DK = 256
RET_DV = 512
RET_QK = RET_HEADS * RET_DK
RET_V = RET_HEADS * RET_DV
ROPE_PAIRS = RET_DK // 4
ROPE_BASE = 10000.0
D_FF = 2816
N_MOD = 6
EPS = 1e-6

OFF_CX = 0
OFF_CB = OFF_CX + CONV_W
OFF_CC = OFF_CB + CONV_W
OFF_Q = OFF_CC + CONV_W
OFF_K = OFF_Q + RET_QK
OFF_V = OFF_K + RET_QK
OFF_G = OFF_V + RET_V
OFF_GA = OFF_G + RET_V
OFF_GB = OFF_GA + D_MODEL
IN_COLS = OFF_GB + D_MODEL

RET_C = 256
TM_IN = RET_C
TS_RET = 512
RET_EPI_BOUNDS = (0, TS_RET // 2, TS_RET)
FFN_COLS = 8
FFN_BLOCKS = ((0, 1024), (1024, 1024), (2048, 768))
FFN_EPI_PIECES = 4
IN_W_CHUNKS = 11
CO_W_CHUNKS = 8
RET_W_CHUNKS = 4
FFN_DN_PASS_STEPS = 16
MOD_ROWS = 8
CTX_FIRST_STEP = 2
BF16_SUBLANES = 16
V7X_VMEM_BYTES = 64 * 1024 * 1024
VMEM_LIMIT = V7X_VMEM_BYTES - 4 * 1024 * 1024


def _const_spec(shape):
    zeros = (0,) * len(shape)
    return pl.BlockSpec(shape, lambda *_: zeros, pipeline_mode=pl.Buffered(1))


def _chunk_spec(shape, n_chunks, axis):
    block = list(shape)
    block[axis] //= n_chunks
    assert block[axis] * n_chunks == shape[axis]

    def index_map(i):
        c = jnp.minimum(i, n_chunks - 1)
        return (c, 0) if axis == 0 else (0, c)

    return pl.BlockSpec(tuple(block), index_map)


def _pass_spec(shape, n_steps, first_step=0):
    rows = shape[0] // n_steps
    assert rows * n_steps == shape[0] and rows % BF16_SUBLANES == 0
    return pl.BlockSpec((rows, shape[1]), lambda i: (jnp.clip(i - first_step, 0, n_steps - 1), 0))


def _cast_chunks(step, weights):
    n_steps = max(n for _, _, _, n in weights)

    @pl.when(step < n_steps)
    def _():
        for c in range(n_steps):
            @pl.when(step == c)
            def _():
                for src_ref, dst_ref, axis, n in weights:
                    if c < n:
                        size = src_ref.shape[axis]
                        if axis == 0:
                            dst_ref[c * size:(c + 1) * size, :] = src_ref[...].astype(BF16)
                        else:
                            dst_ref[:, c * size:(c + 1) * size] = src_ref[...].astype(BF16)


def _rms(xf):
    return xf * lax.rsqrt(jnp.mean(xf * xf, axis=-1, keepdims=True) + EPS)


def _sigmoid(x):
    return 1.0 / (1.0 + jnp.exp(-x))


def _dot(a, b):
    return jnp.dot(a, b, preferred_element_type=F32)


def _dot_nt(a, b):
    return lax.dot_general(a, b, (((1,), (1,)), ((), ())), preferred_element_type=F32)


def _dot_tn(a, b):
    return lax.dot_general(a, b, (((0,), (0,)), ((), ())), preferred_element_type=F32)


def _tables_kernel(theta_ref, dmat_ref, rowf_ref, rowb_ref, cdt_ref, ctxw_ref, *, ctx_len):
    th = theta_ref[...]
    lg = -(jnp.maximum(-th, 0.0) + jnp.log1p(jnp.exp(-jnp.abs(th))))
    n_i = lax.broadcasted_iota(jnp.int32, (RET_C, RET_C), 0)
    m_i = lax.broadcasted_iota(jnp.int32, (RET_C, RET_C), 1)
    diff = (n_i - m_i).astype(F32)
    idx = lax.broadcasted_iota(jnp.int32, (RET_C, RET_DK), 0).astype(F32)
    tok = lax.broadcasted_iota(jnp.int32, (RET_DK, RET_C), 1).astype(F32)
    midx = lax.broadcasted_iota(jnp.int32, (ctx_len, RET_DK), 0).astype(F32)
    for h in range(RET_HEADS):
        lf = lg[h:h + 1, :]
        lb = lg[RET_HEADS + h:RET_HEADS + h + 1, :]
        lf_c, lb_c = lf[:, :RET_C], lb[:, :RET_C]
        dmat_ref[h] = jnp.exp(jnp.where(diff >= 0.0, diff * lf_c, -diff * lb_c))
        lf_k, lb_k = lf[:, :RET_DK], lb[:, :RET_DK]
        rowf_ref[h, 0] = jnp.exp((idx + 1.0) * lf_k).astype(BF16)
        rowf_ref[h, 1] = jnp.exp((RET_C - 1.0 - tok) * lf_k).astype(BF16)
        rowb_ref[h, 0] = jnp.exp((RET_C - idx) * lb_k).astype(BF16)
        rowb_ref[h, 1] = jnp.exp(tok * lb_k).astype(BF16)
        cdt_ref[h, 0] = jnp.exp(RET_C * lf)
        cdt_ref[h, 1] = jnp.exp(RET_C * lb)
        ctxw_ref[h, 0] = jnp.exp((ctx_len - 1.0 - midx) * lf_k)
        ctxw_ref[h, 1] = jnp.exp(midx * lb_k)


def _prep_kernel(c_ref, cctx_ref, wada_ref, bada_ref, ctx_ref, gain_ref, w_ref, theta_ref,
                 mod_ref, sfw_ref, sbw_ref, dmat_ref, rowf_ref, rowb_ref, cdt_ref,
                 hc_ref, kc_ref, ctxw_ref):
    j = pl.program_id(0)
    bsz, lc, _ = ctx_ref.shape

    cv = jnp.concatenate([c_ref[...], cctx_ref[...], jnp.zeros((MOD_ROWS - bsz - 1, D_MODEL), F32)], axis=0)
    s = cv * _sigmoid(cv)
    res = _dot(s.astype(BF16), wada_ref[...].astype(BF16)) + bada_ref[...]
    for k in range(N_MOD):
        @pl.when(j == k)
        def _():
            mod_ref[:, k, :] = res

    heads_per_step = RET_QK // RET_DV

    def project():
        return _dot(hc_ref[...], w_ref[...].astype(BF16))

    @pl.when(j == CTX_FIRST_STEP)
    def _():
        _tables_kernel(theta_ref, dmat_ref, rowf_ref, rowb_ref, cdt_ref, ctxw_ref, ctx_len=lc)
        shift = mod_ref[bsz:bsz + 1, 0, :]
        scale = mod_ref[bsz:bsz + 1, 1, :]
        xc = ctx_ref[...].reshape(bsz * lc, D_MODEL)
        hc_ref[...] = (_rms(xc) * gain_ref[...] * (1.0 + scale) + shift).astype(BF16)
        kc_ref[...] = project() * RET_DK ** -0.5

    for step in range(1, RET_HEADS // heads_per_step + 1):
        @pl.when(j == CTX_FIRST_STEP + step)
        def _():
            p = project()
            for hh in range(heads_per_step):
                h = (step - 1) * heads_per_step + hh
                for b in range(bsz):
                    rows = slice(b * lc, (b + 1) * lc)
                    kh = kc_ref[rows, h * RET_DK:(h + 1) * RET_DK]
                    vh = p[rows, hh * RET_DV:(hh + 1) * RET_DV].astype(BF16)
                    kfb = jnp.concatenate([(kh * ctxw_ref[h, 0]).astype(BF16),
                                           (kh * ctxw_ref[h, 1]).astype(BF16)], axis=1)
                    s_both = _dot_tn(kfb, vh)
                    sfw_ref[b, hh] = s_both[:RET_DK]
                    sbw_ref[b, hh] = s_both[RET_DK:]


def _prep(c, c_ctx, w_ada, b_ada, ctx, gain, w_in, theta):
    bsz, lc, _ = ctx.shape
    assert w_ada.shape[1] == N_MOD * D_MODEL
    assert OFF_K % RET_QK == 0 and OFF_V == OFF_K + RET_QK and RET_QK % RET_DV == 0
    heads_per_step = RET_QK // RET_DV
    ctx_steps = 1 + RET_HEADS // heads_per_step
    assert CTX_FIRST_STEP + ctx_steps <= N_MOD
    state = jax.ShapeDtypeStruct((bsz, RET_HEADS, RET_DK, RET_DV), F32)
    sspec = pl.BlockSpec((bsz, heads_per_step, RET_DK, RET_DV),
                         lambda j: (0, jnp.clip(j - CTX_FIRST_STEP - 1, 0, ctx_steps - 2), 0, 0))
    rows = jax.ShapeDtypeStruct((RET_HEADS, 2, RET_C, RET_DK), BF16)
    mod = jax.ShapeDtypeStruct((MOD_ROWS, N_MOD, D_MODEL), F32)
    tables = (jax.ShapeDtypeStruct((RET_HEADS, RET_C, RET_C), F32), rows, rows,
              jax.ShapeDtypeStruct((RET_HEADS, 2, 1, RET_DV), F32))

    def whole(shape):
        return pl.BlockSpec(shape, lambda j: (0,) * len(shape))

    return pl.pallas_call(
        _prep_kernel,
        name="prep",
        grid=(N_MOD,),
        in_specs=[
            whole(c.shape),
            whole((1, D_MODEL)),
            pl.BlockSpec((D_MODEL, D_MODEL), lambda j: (0, j)),
            pl.BlockSpec((1, D_MODEL), lambda j: (0, j)),
            whole(ctx.shape),
            whole((1, D_MODEL)),
            pl.BlockSpec((D_MODEL, RET_QK),
                         lambda j: (0, OFF_K // RET_QK + jnp.clip(j - CTX_FIRST_STEP, 0, ctx_steps - 1))),
            whole(theta.shape),
        ],
        out_specs=(whole(mod.shape), sspec, sspec, *[whole(t.shape) for t in tables]),
        out_shape=(mod, state, state, *tables),
        scratch_shapes=[pltpu.VMEM((bsz * lc, D_MODEL), BF16), pltpu.VMEM((bsz * lc, RET_QK), F32),
                        pltpu.VMEM((RET_HEADS, 2, lc, RET_DK), F32)],
        compiler_params=pltpu.CompilerParams(dimension_semantics=("arbitrary",),
                                             vmem_limit_bytes=VMEM_LIMIT),
    )(c, c_ctx.reshape(1, D_MODEL), w_ada, b_ada, ctx, gain, w_in, theta)


def _inproj_kernel(x_ref, mod_ref, gain_ref, cos_ref, sin_ref, win32_ref, wco32_ref, cw_ref, cb_ref,
                   s0_ref, rowb_ref, cdt_ref,
                   yc_ref, q_ref, kt_ref, v_ref, sg_ref, sgb_ref, ob_ref, t_ref, win_ref, wco_ref,
                   *, tiles_per_seq):
    step = pl.program_id(0)
    _cast_chunks(step, ((win32_ref, win_ref, 1, IN_W_CHUNKS), (wco32_ref, wco_ref, 0, CO_W_CHUNKS)))

    @pl.when(step >= IN_W_CHUNKS)
    def _():
        _inproj_tile(x_ref, mod_ref, gain_ref, cos_ref, sin_ref, win_ref, wco_ref, cw_ref, cb_ref,
                     s0_ref, rowb_ref, cdt_ref, yc_ref, q_ref, kt_ref, v_ref, sg_ref, sgb_ref, ob_ref,
                     t_ref, (step - IN_W_CHUNKS) % tiles_per_seq == 0)


def _inproj_tile(x_ref, mod_ref, gain_ref, cos_ref, sin_ref, win_ref, wco_ref, cw_ref, cb_ref,
                 s0_ref, rowb_ref, cdt_ref,
                 yc_ref, q_ref, kt_ref, v_ref, sg_ref, sgb_ref, ob_ref, t_ref, first_tile):
    @pl.when(first_tile)
    def _():
        t_ref[...] = s0_ref[0]

    m = mod_ref[0]
    h = _rms(x_ref[...]) * gain_ref[...] * (1.0 + m[1:2]) + m[0:1]
    hb = h.astype(BF16)

    def proj(off, width):
        return _dot(hb, win_ref[:, off:off + width])

    u = proj(OFF_CC, CONV_W) * proj(OFF_CX, CONV_W)
    col = lax.broadcasted_iota(jnp.int32, (TM_IN, CONV_W), 0) % GRID_W
    prev = jnp.where(col == 0, 0.0, pltpu.roll(u, 1, 0))
    nxt = jnp.where(col == GRID_W - 1, 0.0, pltpu.roll(u, TM_IN - 1, 0))
    cw = cw_ref[...]
    conv = cw[0:1] * prev + cw[1:2] * u + cw[2:3] * nxt + cb_ref[...]
    z = (proj(OFF_CB, CONV_W) * conv).astype(BF16)
    yc_ref[...] = (_sigmoid(proj(OFF_GA, D_MODEL)) * _dot(z, wco_ref[...])).astype(BF16)
    sgb_ref[...] = _sigmoid(proj(OFF_GB, D_MODEL)).astype(BF16)

    cos = cos_ref[...]
    sin = sin_ref[...]
    half = RET_DK // 2
    for off, ref, scale in ((OFF_Q, q_ref, 1.0), (OFF_K, kt_ref, RET_DK ** -0.5)):
        t = proj(off, RET_QK)
        for hh in range(RET_HEADS):
            lo = hh * RET_DK
            t1 = t[:, lo:lo + half]
            t2 = t[:, lo + half:lo + RET_DK]
            r1 = (t1 * cos - t2 * sin) * scale
            r2 = (t2 * cos + t1 * sin) * scale
            if ref is q_ref:
                ref[:, lo:lo + half] = r1.astype(BF16)
                ref[:, lo + half:lo + RET_DK] = r2.astype(BF16)
            else:
                ref[lo:lo + half, :] = r1.T.astype(BF16)
                ref[lo + half:lo + RET_DK, :] = r2.T.astype(BF16)
    v_ref[...] = proj(OFF_V, RET_V).astype(BF16)
    g = proj(OFF_G, RET_V)
    sg_ref[...] = (g * _sigmoid(g)).astype(BF16)

    for hh in range(RET_HEADS):
        qs = q_ref[:, hh * RET_DK:(hh + 1) * RET_DK] * rowb_ref[hh, 0]
        kst = kt_ref[hh * RET_DK:(hh + 1) * RET_DK, :] * rowb_ref[hh, 1]
        vh = v_ref[:, hh * RET_DV:(hh + 1) * RET_DV]
        t = t_ref[hh]
        ob_ref[:, hh * RET_DV:(hh + 1) * RET_DV] = _dot(qs, t.astype(BF16)).astype(BF16)
        t_ref[hh] = cdt_ref[hh, 1] * t + _dot(kst, vh)


def _inproj(x2, modt, gain, cos, sin, win, wco, cw, cb, s_bw, rowb, cdt, bsz, seq):
    nt = seq // TM_IN

    def tile(i):
        t = jnp.maximum(i - IN_W_CHUNKS, 0)
        return t // nt, nt - 1 - t % nt

    def row_spec(width):
        return pl.BlockSpec((TM_IN, width), lambda i: (tile(i)[0] * nt + tile(i)[1], 0))

    def out(width):
        return jax.ShapeDtypeStruct((bsz * seq, width), BF16)

    rope = pl.BlockSpec((TM_IN, RET_DK // 2), lambda i: (tile(i)[1], 0))
    return pl.pallas_call(
        functools.partial(_inproj_kernel, tiles_per_seq=nt),
        name="inproj",
        grid=(IN_W_CHUNKS + bsz * nt,),
        in_specs=[
            row_spec(D_MODEL),
            pl.BlockSpec((1, N_MOD, D_MODEL), lambda i: (tile(i)[0], 0, 0)),
            pl.BlockSpec((1, D_MODEL), lambda i: (0, 0)),
            rope, rope,
            _chunk_spec(win.shape, IN_W_CHUNKS, 1),
            _chunk_spec(wco.shape, CO_W_CHUNKS, 0),
            pl.BlockSpec(cw.shape, lambda i: (0, 0)),
            pl.BlockSpec((1, CONV_W), lambda i: (0, 0)),
            pl.BlockSpec((1, RET_HEADS, RET_DK, RET_DV), lambda i: (tile(i)[0], 0, 0, 0)),
            _const_spec(rowb.shape),
            _const_spec(cdt.shape),
        ],
        out_specs=(row_spec(D_MODEL), row_spec(RET_QK),
                   pl.BlockSpec((RET_QK, TM_IN), lambda i: (0, tile(i)[0] * nt + tile(i)[1])),
                   row_spec(RET_V), row_spec(RET_V), row_spec(D_MODEL), row_spec(RET_V)),
        out_shape=(out(D_MODEL), out(RET_QK), jax.ShapeDtypeStruct((RET_QK, bsz * seq), BF16),
                   out(RET_V), out(RET_V), out(D_MODEL), out(RET_V)),
        scratch_shapes=[pltpu.VMEM((RET_HEADS, RET_DK, RET_DV), F32),
                        pltpu.VMEM(win.shape, BF16), pltpu.VMEM(wco.shape, BF16)],
        compiler_params=pltpu.CompilerParams(dimension_semantics=("arbitrary",),
                                             vmem_limit_bytes=VMEM_LIMIT),
    )(x2, modt, gain, cos, sin, win, wco, cw, cb, s_bw, rowb, cdt)


def _ret_fw_kernel(q_ref, kt_ref, v_ref, sg_ref, ob_ref, yc_ref, sgb_ref, x_ref, s0_ref, dmat_ref,
                   rowf_ref, cdt_ref, wro32_ref, wo32_ref, mod_ref, gpost_ref, wup32_ref, wdn32_ref,
                   x1_ref, wup_out_ref, wdn_out_ref, s_ref, gat_ref, wro_ref, wo_ref, *, steps_per_seq):
    step = pl.program_id(0)
    _cast_chunks(step, ((wro32_ref, wro_ref, 0, RET_W_CHUNKS), (wo32_ref, wo_ref, 0, RET_W_CHUNKS)))

    @pl.when(step >= RET_W_CHUNKS)
    def _():
        wup_out_ref[...] = wup32_ref[...].astype(BF16)
        wdn_out_ref[...] = wdn32_ref[...].astype(BF16)
        _ret_fw_tile(q_ref, kt_ref, v_ref, sg_ref, ob_ref, yc_ref, sgb_ref, x_ref, s0_ref, dmat_ref,
                     rowf_ref, cdt_ref, wro_ref, wo_ref, mod_ref, gpost_ref, x1_ref, s_ref, gat_ref,
                     (step - RET_W_CHUNKS) % steps_per_seq == 0)


def _ret_fw_tile(q_ref, kt_ref, v_ref, sg_ref, ob_ref, yc_ref, sgb_ref, x_ref, s0_ref, dmat_ref,
                 rowf_ref, cdt_ref, wro_ref, wo_ref, mod_ref, gpost_ref,
                 x1_ref, s_ref, gat_ref, first_tile):
    @pl.when(first_tile)
    def _():
        s_ref[...] = s0_ref[0]

    heads = range(RET_HEADS)
    for ci in range(TS_RET // RET_C):
        rows = slice(ci * RET_C, (ci + 1) * RET_C)
        qb = [q_ref[rows, h * RET_DK:(h + 1) * RET_DK] for h in heads]
        kt = [kt_ref[h * RET_DK:(h + 1) * RET_DK, rows] for h in heads]
        vh = [v_ref[rows, h * RET_DV:(h + 1) * RET_DV] for h in heads]
        scores = [(_dot(qb[h], kt[h]) * dmat_ref[h]).astype(BF16) for h in heads]
        s = [s_ref[h] for h in heads]
        o = [(_dot(scores[h], vh[h]) + _dot(qb[h] * rowf_ref[h, 0], s[h].astype(BF16))
              + ob_ref[rows, h * RET_DV:(h + 1) * RET_DV].astype(F32)) for h in heads]
        for h in heads:
            s_ref[h] = cdt_ref[h, 0] * s[h] + _dot(kt[h] * rowf_ref[h, 1], vh[h])
        for h in heads:
            gate = sg_ref[rows, h * RET_DV:(h + 1) * RET_DV].astype(F32)
            gat_ref[rows, h * RET_DV:(h + 1) * RET_DV] = (gate * _rms(o[h])).astype(BF16)

    for p in range(len(RET_EPI_BOUNDS) - 1):
        rows = slice(RET_EPI_BOUNDS[p], RET_EPI_BOUNDS[p + 1])
        y_ret = _dot(gat_ref[rows, :], wro_ref[...])
        merged = yc_ref[rows, :].astype(F32) + sgb_ref[rows, :].astype(F32) * y_ret
        y = _dot(merged.astype(BF16), wo_ref[...])
        x1_ref[rows, :] = x_ref[rows, :] + mod_ref[0][2:3] * (_rms(y) * gpost_ref[...])


def _ret_fw(q, k, v, sg, ob, yc, sgb, x2, s_fw, dmat, rowf, cdt, wro, wo, modt, gpost, wup, wdn,
            bsz, seq):
    ns = seq // TS_RET
    passed = [_pass_spec(wup.shape, bsz * ns, RET_W_CHUNKS),
              _pass_spec(wdn.shape, FFN_DN_PASS_STEPS, RET_W_CHUNKS)]

    def tile(i):
        return jnp.maximum(i - RET_W_CHUNKS, 0)

    def row_spec(width):
        return pl.BlockSpec((TS_RET, width), lambda i: (tile(i), 0))

    return pl.pallas_call(
        functools.partial(_ret_fw_kernel, steps_per_seq=ns),
        name="ret_fw",
        grid=(RET_W_CHUNKS + bsz * ns,),
        in_specs=[
            row_spec(RET_QK), pl.BlockSpec((RET_QK, TS_RET), lambda i: (0, tile(i))),
            row_spec(RET_V), row_spec(RET_V), row_spec(RET_V),
            row_spec(D_MODEL), row_spec(D_MODEL), row_spec(D_MODEL),
            pl.BlockSpec((1, RET_HEADS, RET_DK, RET_DV), lambda i: (tile(i) // ns, 0, 0, 0)),
            _const_spec(dmat.shape), _const_spec(rowf.shape), _const_spec(cdt.shape),
            _chunk_spec(wro.shape, RET_W_CHUNKS, 0), _chunk_spec(wo.shape, RET_W_CHUNKS, 0),
            pl.BlockSpec((1, N_MOD, D_MODEL), lambda i: (tile(i) // ns, 0, 0)),
            pl.BlockSpec((1, D_MODEL), lambda i: (0, 0)),
            *passed,
        ],
        out_specs=(row_spec(D_MODEL), *passed),
        out_shape=(jax.ShapeDtypeStruct((bsz * seq, D_MODEL), F32),
                   jax.ShapeDtypeStruct(wup.shape, BF16), jax.ShapeDtypeStruct(wdn.shape, BF16)),
        scratch_shapes=[pltpu.VMEM((RET_HEADS, RET_DK, RET_DV), F32),
                        pltpu.VMEM((TS_RET, RET_V), BF16),
                        pltpu.VMEM(wro.shape, BF16), pltpu.VMEM(wo.shape, BF16)],
        compiler_params=pltpu.CompilerParams(dimension_semantics=("arbitrary",),
                                             vmem_limit_bytes=VMEM_LIMIT),
    )(q, k, v, sg, ob, yc, sgb, x2, s_fw, dmat, rowf, cdt, wro, wo, modt, gpost, wup, wdn)


def _ffn_kernel(x1_ref, mod_ref, gpre_ref, gpost_ref, wup_ref, cw_ref, cb_ref, wdn_ref,
                out_ref, acc_ref, *, grid_rows):
    tm = grid_rows * FFN_COLS
    x1 = x1_ref[0].reshape(tm, D_MODEL)
    m = mod_ref[0]
    hb = (_rms(x1) * gpre_ref[...] * (1.0 + m[4:5]) + m[3:4]).astype(BF16)

    def conv(u, off, width):
        w = cw_ref[:, off:off + width]
        pad = jnp.zeros((FFN_COLS, width), F32)
        above = jnp.concatenate([pad, u[:tm - FFN_COLS]], axis=0)
        below = jnp.concatenate([u[FFN_COLS:], pad], axis=0)
        return w[0:1] * above + w[1:2] * u + w[2:3] * below + cb_ref[:, off:off + width]

    def up(c0, width):
        return (_dot(hb, wup_ref[:, c0:c0 + width]),
                _dot(hb, wup_ref[:, D_FF + c0:D_FF + c0 + width]))

    ahead = up(*FFN_BLOCKS[0])
    for j, (c0, width) in enumerate(FFN_BLOCKS):
        ug, uv = ahead
        if j + 1 < len(FFN_BLOCKS):
            ahead = up(*FFN_BLOCKS[j + 1])
        g = conv(ug, c0, width)
        val = conv(uv, D_FF + c0, width)
        a = (g * _sigmoid(g) * val).astype(BF16)
        if j == 0:
            acc_ref[...] = _dot(a, wdn_ref[c0:c0 + width, :])
        elif j + 1 < len(FFN_BLOCKS):
            acc_ref[...] += _dot(a, wdn_ref[c0:c0 + width, :])
        else:
            gr = grid_rows // FFN_EPI_PIECES
            for p in range(FFN_EPI_PIECES):
                rows = slice(p * gr * FFN_COLS, (p + 1) * gr * FFN_COLS)
                f = acc_ref[rows, :] + _dot(a[rows, :], wdn_ref[c0:c0 + width, :])
                res = x1[rows, :] + m[5:6] * (_rms(f) * gpost_ref[...])
                out_ref[0, p * gr:(p + 1) * gr] = res.reshape(gr, FFN_COLS, D_MODEL)


def _ffn(x1g, modt, gpre, gpost, wup, cw, cb, wdn):
    bsz, grid_rows, grid_w, _ = x1g.shape
    strips = grid_w // FFN_COLS
    strip = pl.BlockSpec((1, grid_rows, FFN_COLS, D_MODEL), lambda i: (i // strips, 0, i % strips, 0))
    vec = pl.BlockSpec((1, D_MODEL), lambda i: (0, 0))
    return pl.pallas_call(
        functools.partial(_ffn_kernel, grid_rows=grid_rows),
        name="ffn",
        grid=(bsz * strips,),
        in_specs=[
            strip,
            pl.BlockSpec((1, N_MOD, D_MODEL), lambda i: (i // strips, 0, 0)),
            vec, vec,
            _const_spec(wup.shape),
            pl.BlockSpec(cw.shape, lambda i: (0, 0)),
            pl.BlockSpec((1, 2 * D_FF), lambda i: (0, 0)),
            _const_spec(wdn.shape),
        ],
        out_specs=strip,
        out_shape=jax.ShapeDtypeStruct(x1g.shape, F32),
        scratch_shapes=[pltpu.VMEM((grid_rows * FFN_COLS, D_MODEL), F32)],
        compiler_params=pltpu.CompilerParams(dimension_semantics=("arbitrary",),
                                             vmem_limit_bytes=VMEM_LIMIT),
    )(x1g, modt, gpre, gpost, wup, cw, cb, wdn)


def _rotary_tables(n):
    t = np.arange(n)
    row = (t // GRID_W).astype(np.float64)
    col = (t % GRID_W).astype(np.float64)
    inv = ROPE_BASE ** (-np.arange(ROPE_PAIRS, dtype=np.float64) / ROPE_PAIRS)
    ang = np.concatenate([row[:, None] * inv, col[:, None] * inv], axis=-1)
    return jnp.asarray(np.cos(ang), F32), jnp.asarray(np.sin(ang), F32)


def kernel(x, c, ctx, c_ctx, w_ada, b_ada, norm_mix_pre, norm_mix_post, w_in, conv_w, conv_b,
           w_conv_out, ret_decay_fw, ret_decay_bw, w_ret_out, w_o, norm_ffn_pre, norm_ffn_post,
           w_ffn_up, ffn_conv_w, ffn_conv_b, w_ffn_down):
    bsz, seq, d = x.shape
    depth = w_ada.shape[0]
    assert d == D_MODEL and depth == 1 and bsz + 1 <= MOD_ROWS
    assert seq % TS_RET == 0 and seq % TM_IN == 0 and TM_IN % GRID_W == 0 and GRID_W % FFN_COLS == 0
    lc = ctx.shape[1]
    cos, sin = _rotary_tables(seq)
    x2 = x.reshape(bsz * seq, d)


    theta = jnp.concatenate([ret_decay_fw[0], ret_decay_bw[0]]).astype(F32)
    theta = jnp.broadcast_to(theta[:, None], (2 * RET_HEADS, RET_DV))
    modt, s_fw, s_bw, dmat, rowf, rowb, cdt = _prep(c, c_ctx, w_ada[0], b_ada, ctx, norm_mix_pre,
                                                    w_in[0], theta)

    yc, q, k, v, sg, sgb, ob = _inproj(x2, modt, norm_mix_pre, cos, sin, w_in[0], w_conv_out[0],
                                       conv_w[0], conv_b, s_bw, rowb, cdt, bsz, seq)
    x1, wup, wdn = _ret_fw(q, k, v, sg, ob, yc, sgb, x2, s_fw, dmat, rowf, cdt, w_ret_out[0], w_o[0],
                           modt, norm_mix_post, w_ffn_up[0], w_ffn_down[0], bsz, seq)
    out = _ffn(x1.reshape(bsz, seq // GRID_W, GRID_W, d), modt, norm_ffn_pre, norm_ffn_post,
               wup, ffn_conv_w[0], ffn_conv_b, wdn)
    return out.reshape(bsz, seq, d)
```

```python
import functools

import jax
import jax.numpy as jnp
import numpy as np
from jax import lax
from jax.experimental import pallas as pl
from jax.experimental.pallas import tpu as pltpu

F32 = jnp.float32
BF16 = jnp.bfloat16

D_MODEL = 1024
GRID_W = 64
CONV_W = 1024
RET_HEADS = 4
RET_DK = 256
RET_DV = 512
RET_QK = RET_HEADS * RET_DK
RET_V = RET_HEADS * RET_DV
ROPE_PAIRS = RET_DK // 4
ROPE_BASE = 10000.0
D_FF = 2816
N_MOD = 6
EPS = 1e-6

OFF_CX = 0
OFF_CB = OFF_CX + CONV_W
OFF_CC = OFF_CB + CONV_W
OFF_Q = OFF_CC + CONV_W
OFF_K = OFF_Q + RET_QK
OFF_V = OFF_K + RET_QK
OFF_G = OFF_V + RET_V
OFF_GA = OFF_G + RET_V
OFF_GB = OFF_GA + D_MODEL
IN_COLS = OFF_GB + D_MODEL

RET_C = 256
TM_IN = RET_C
TS_RET = 512
RET_EPI_BOUNDS = (0, TS_RET // 2, TS_RET)
FFN_COLS = 8
FFN_BLOCKS = ((0, 1024), (1024, 1024), (2048, 768))
FFN_EPI_PIECES = 4
IN_PROLOGUE_STEPS = 1
IN_W_CHUNKS = 11
CO_W_CHUNKS = 8
RET_W_CHUNKS = 4
FFN_DN_PASS_STEPS = 16
MOD_ROWS = 8
CTX_FIRST_STEP = 2
BF16_SUBLANES = 16
V7X_VMEM_BYTES = 64 * 1024 * 1024
VMEM_LIMIT = V7X_VMEM_BYTES - 4 * 1024 * 1024


def _const_spec(shape):
    zeros = (0,) * len(shape)
    return pl.BlockSpec(shape, lambda *_: zeros, pipeline_mode=pl.Buffered(1))


def _chunk_spec(shape, n_chunks, axis):
    block = list(shape)
    block[axis] //= n_chunks
    assert block[axis] * n_chunks == shape[axis]

    def index_map(i):
        c = jnp.minimum(i, n_chunks - 1)
        return (c, 0) if axis == 0 else (0, c)

    return pl.BlockSpec(tuple(block), index_map)


def _pass_spec(shape, n_steps, first_step=0):
    rows = shape[0] // n_steps
    assert rows * n_steps == shape[0] and rows % BF16_SUBLANES == 0
    return pl.BlockSpec((rows, shape[1]), lambda i: (jnp.clip(i - first_step, 0, n_steps - 1), 0))


def _stream_cast(src_hbm, dst_ref, stage_ref, sem, n_chunks, axis):
    size = src_hbm.shape[axis] // n_chunks
    assert size * n_chunks == src_hbm.shape[axis]

    def window(c):
        chunk = slice(c * size, (c + 1) * size)
        return (chunk, slice(None)) if axis == 0 else (slice(None), chunk)

    def copy(c):
        slot = c % 2
        return pltpu.make_async_copy(src_hbm.at[window(c)], stage_ref.at[slot], sem.at[slot])

    copy(0).start()
    for c in range(n_chunks):
        if c + 1 < n_chunks:
            copy(c + 1).start()
        copy(c).wait()
        dst_ref[window(c)] = stage_ref[c % 2].astype(BF16)


def _cast_chunks(step, weights):
    n_steps = max(n for _, _, _, n in weights)

    @pl.when(step < n_steps)
    def _():
        for c in range(n_steps):
            @pl.when(step == c)
            def _():
                for src_ref, dst_ref, axis, n in weights:
                    if c < n:
                        size = src_ref.shape[axis]
                        if axis == 0:
                            dst_ref[c * size:(c + 1) * size, :] = src_ref[...].astype(BF16)
                        else:
                            dst_ref[:, c * size:(c + 1) * size] = src_ref[...].astype(BF16)


def _rms(xf):
    return xf * lax.rsqrt(jnp.mean(xf * xf, axis=-1, keepdims=True) + EPS)


def _sigmoid(x):
    return 1.0 / (1.0 + jnp.exp(-x))


def _dot(a, b):
    return jnp.dot(a, b, preferred_element_type=F32)


def _dot_nt(a, b):
    return lax.dot_general(a, b, (((1,), (1,)), ((), ())), preferred_element_type=F32)


def _dot_tn(a, b):
    return lax.dot_general(a, b, (((0,), (0,)), ((), ())), preferred_element_type=F32)


def _tables_kernel(theta_ref, dmat_ref, rowf_ref, rowb_ref, cdt_ref, ctxw_ref, *, ctx_len):
    th = theta_ref[...]
    lg = -(jnp.maximum(-th, 0.0) + jnp.log1p(jnp.exp(-jnp.abs(th))))
    n_i = lax.broadcasted_iota(jnp.int32, (RET_C, RET_C), 0)
    m_i = lax.broadcasted_iota(jnp.int32, (RET_C, RET_C), 1)
    diff = (n_i - m_i).astype(F32)
    idx = lax.broadcasted_iota(jnp.int32, (RET_C, RET_DK), 0).astype(F32)
    tok = lax.broadcasted_iota(jnp.int32, (RET_DK, RET_C), 1).astype(F32)
    midx = lax.broadcasted_iota(jnp.int32, (ctx_len, RET_DK), 0).astype(F32)
    for h in range(RET_HEADS):
        lf = lg[h:h + 1, :]
        lb = lg[RET_HEADS + h:RET_HEADS + h + 1, :]
        lf_c, lb_c = lf[:, :RET_C], lb[:, :RET_C]
        dmat_ref[h] = jnp.exp(jnp.where(diff >= 0.0, diff * lf_c, -diff * lb_c))
        lf_k, lb_k = lf[:, :RET_DK], lb[:, :RET_DK]
        rowf_ref[h, 0] = jnp.exp((idx + 1.0) * lf_k).astype(BF16)
        rowf_ref[h, 1] = jnp.exp((RET_C - 1.0 - tok) * lf_k).astype(BF16)
        rowb_ref[h, 0] = jnp.exp((RET_C - idx) * lb_k).astype(BF16)
        rowb_ref[h, 1] = jnp.exp(tok * lb_k).astype(BF16)
        cdt_ref[h, 0] = jnp.exp(RET_C * lf)
        cdt_ref[h, 1] = jnp.exp(RET_C * lb)
        ctxw_ref[h, 0] = jnp.exp((ctx_len - 1.0 - midx) * lf_k)
        ctxw_ref[h, 1] = jnp.exp(midx * lb_k)


def _prep_kernel(c_ref, cctx_ref, wada_ref, bada_ref, ctx_ref, gain_ref, w_ref, theta_ref,
                 mod_ref, sfw_ref, sbw_ref, dmat_ref, rowf_ref, rowb_ref, cdt_ref,
                 hc_ref, kc_ref, ctxw_ref):
    j = pl.program_id(0)
    bsz, lc, _ = ctx_ref.shape

    cv = jnp.concatenate([c_ref[...], cctx_ref[...], jnp.zeros((MOD_ROWS - bsz - 1, D_MODEL), F32)], axis=0)
    s = cv * _sigmoid(cv)
    res = _dot(s.astype(BF16), wada_ref[...].astype(BF16)) + bada_ref[...]
    for k in range(N_MOD):
        @pl.when(j == k)
        def _():
            mod_ref[:, k, :] = res

    heads_per_step = RET_QK // RET_DV

    def project():
        return _dot(hc_ref[...], w_ref[...].astype(BF16))

    @pl.when(j == CTX_FIRST_STEP)
    def _():
        _tables_kernel(theta_ref, dmat_ref, rowf_ref, rowb_ref, cdt_ref, ctxw_ref, ctx_len=lc)
        shift = mod_ref[bsz:bsz + 1, 0, :]
        scale = mod_ref[bsz:bsz + 1, 1, :]
        xc = ctx_ref[...].reshape(bsz * lc, D_MODEL)
        hc_ref[...] = (_rms(xc) * gain_ref[...] * (1.0 + scale) + shift).astype(BF16)
        kc_ref[...] = project() * RET_DK ** -0.5

    for step in range(1, RET_HEADS // heads_per_step + 1):
        @pl.when(j == CTX_FIRST_STEP + step)
        def _():
            p = project()
            for hh in range(heads_per_step):
                h = (step - 1) * heads_per_step + hh
                for b in range(bsz):
                    rows = slice(b * lc, (b + 1) * lc)
                    kh = kc_ref[rows, h * RET_DK:(h + 1) * RET_DK]
                    vh = p[rows, hh * RET_DV:(hh + 1) * RET_DV].astype(BF16)
                    kfb = jnp.concatenate([(kh * ctxw_ref[h, 0]).astype(BF16),
                                           (kh * ctxw_ref[h, 1]).astype(BF16)], axis=1)
                    s_both = _dot_tn(kfb, vh)
                    sfw_ref[b, hh] = s_both[:RET_DK]
                    sbw_ref[b, hh] = s_both[RET_DK:]


def _prep(c, c_ctx, w_ada, b_ada, ctx, gain, w_in, theta):
    bsz, lc, _ = ctx.shape
    assert w_ada.shape[1] == N_MOD * D_MODEL
    assert OFF_K % RET_QK == 0 and OFF_V == OFF_K + RET_QK and RET_QK % RET_DV == 0
    heads_per_step = RET_QK // RET_DV
    ctx_steps = 1 + RET_HEADS // heads_per_step
    assert CTX_FIRST_STEP + ctx_steps <= N_MOD
    state = jax.ShapeDtypeStruct((bsz, RET_HEADS, RET_DK, RET_DV), F32)
    sspec = pl.BlockSpec((bsz, heads_per_step, RET_DK, RET_DV),
                         lambda j: (0, jnp.clip(j - CTX_FIRST_STEP - 1, 0, ctx_steps - 2), 0, 0))
    rows = jax.ShapeDtypeStruct((RET_HEADS, 2, RET_C, RET_DK), BF16)
    mod = jax.ShapeDtypeStruct((MOD_ROWS, N_MOD, D_MODEL), F32)
    tables = (jax.ShapeDtypeStruct((RET_HEADS, RET_C, RET_C), F32), rows, rows,
              jax.ShapeDtypeStruct((RET_HEADS, 2, 1, RET_DV), F32))

    def whole(shape):
        return pl.BlockSpec(shape, lambda j: (0,) * len(shape))

    return pl.pallas_call(
        _prep_kernel,
        name="prep",
        grid=(N_MOD,),
        in_specs=[
            whole(c.shape),
            whole((1, D_MODEL)),
            pl.BlockSpec((D_MODEL, D_MODEL), lambda j: (0, j)),
            pl.BlockSpec((1, D_MODEL), lambda j: (0, j)),
            whole(ctx.shape),
            whole((1, D_MODEL)),
            pl.BlockSpec((D_MODEL, RET_QK),
                         lambda j: (0, OFF_K // RET_QK + jnp.clip(j - CTX_FIRST_STEP, 0, ctx_steps - 1))),
            whole(theta.shape),
        ],
        out_specs=(whole(mod.shape), sspec, sspec, *[whole(t.shape) for t in tables]),
        out_shape=(mod, state, state, *tables),
        scratch_shapes=[pltpu.VMEM((bsz * lc, D_MODEL), BF16), pltpu.VMEM((bsz * lc, RET_QK), F32),
                        pltpu.VMEM((RET_HEADS, 2, lc, RET_DK), F32)],
        compiler_params=pltpu.CompilerParams(dimension_semantics=("arbitrary",),
                                             vmem_limit_bytes=VMEM_LIMIT),
    )(c, c_ctx.reshape(1, D_MODEL), w_ada, b_ada, ctx, gain, w_in, theta)


def _inproj_kernel(x_ref, mod_ref, gain_ref, cos_ref, sin_ref, win32_ref, wco32_ref, cw_ref, cb_ref,
                   s0_ref, rowb_ref, cdt_ref,
                   yc_ref, q_ref, kt_ref, v_ref, sg_ref, sgb_ref, ob_ref, t_ref, win_ref, wco_ref,
                   win_stage_ref, wco_stage_ref, win_sem, wco_sem, *, tiles_per_seq):
    step = pl.program_id(0)

    @pl.when(step == 0)
    def _():
        _stream_cast(win32_ref, win_ref, win_stage_ref, win_sem, IN_W_CHUNKS, 1)
        _stream_cast(wco32_ref, wco_ref, wco_stage_ref, wco_sem, CO_W_CHUNKS, 0)

    @pl.when(step >= IN_PROLOGUE_STEPS)
    def _():
        _inproj_tile(x_ref, mod_ref, gain_ref, cos_ref, sin_ref, win_ref, wco_ref, cw_ref, cb_ref,
                     s0_ref, rowb_ref, cdt_ref, yc_ref, q_ref, kt_ref, v_ref, sg_ref, sgb_ref, ob_ref,
                     t_ref, (step - IN_PROLOGUE_STEPS) % tiles_per_seq == 0)


def _inproj_tile(x_ref, mod_ref, gain_ref, cos_ref, sin_ref, win_ref, wco_ref, cw_ref, cb_ref,
                 s0_ref, rowb_ref, cdt_ref,
                 yc_ref, q_ref, kt_ref, v_ref, sg_ref, sgb_ref, ob_ref, t_ref, first_tile):
    @pl.when(first_tile)
    def _():
        t_ref[...] = s0_ref[0]

    m = mod_ref[0]
    h = _rms(x_ref[...]) * gain_ref[...] * (1.0 + m[1:2]) + m[0:1]
    hb = h.astype(BF16)

    def proj(off, width):
        return _dot(hb, win_ref[:, off:off + width])

    u = proj(OFF_CC, CONV_W) * proj(OFF_CX, CONV_W)
    col = lax.broadcasted_iota(jnp.int32, (TM_IN, CONV_W), 0) % GRID_W
    prev = jnp.where(col == 0, 0.0, pltpu.roll(u, 1, 0))
    nxt = jnp.where(col == GRID_W - 1, 0.0, pltpu.roll(u, TM_IN - 1, 0))
    cw = cw_ref[...]
    conv = cw[0:1] * prev + cw[1:2] * u + cw[2:3] * nxt + cb_ref[...]
    z = (proj(OFF_CB, CONV_W) * conv).astype(BF16)
    yc_ref[...] = (_sigmoid(proj(OFF_GA, D_MODEL)) * _dot(z, wco_ref[...])).astype(BF16)
    sgb_ref[...] = _sigmoid(proj(OFF_GB, D_MODEL)).astype(BF16)

    cos = cos_ref[...]
    sin = sin_ref[...]
    half = RET_DK // 2
    for off, ref, scale in ((OFF_Q, q_ref, 1.0), (OFF_K, kt_ref, RET_DK ** -0.5)):
        t = proj(off, RET_QK)
        for hh in range(RET_HEADS):
            lo = hh * RET_DK
            t1 = t[:, lo:lo + half]
            t2 = t[:, lo + half:lo + RET_DK]
            r1 = (t1 * cos - t2 * sin) * scale
            r2 = (t2 * cos + t1 * sin) * scale
            if ref is q_ref:
                ref[:, lo:lo + half] = r1.astype(BF16)
                ref[:, lo + half:lo + RET_DK] = r2.astype(BF16)
            else:
                ref[lo:lo + half, :] = r1.T.astype(BF16)
                ref[lo + half:lo + RET_DK, :] = r2.T.astype(BF16)
    v_ref[...] = proj(OFF_V, RET_V).astype(BF16)
    g = proj(OFF_G, RET_V)
    sg_ref[...] = (g * _sigmoid(g)).astype(BF16)

    for hh in range(RET_HEADS):
        qs = q_ref[:, hh * RET_DK:(hh + 1) * RET_DK] * rowb_ref[hh, 0]
        kst = kt_ref[hh * RET_DK:(hh + 1) * RET_DK, :] * rowb_ref[hh, 1]
        vh = v_ref[:, hh * RET_DV:(hh + 1) * RET_DV]
        t = t_ref[hh]
        ob_ref[:, hh * RET_DV:(hh + 1) * RET_DV] = _dot(qs, t.astype(BF16)).astype(BF16)
        t_ref[hh] = cdt_ref[hh, 1] * t + _dot(kst, vh)


def _inproj(x2, modt, gain, cos, sin, win, wco, cw, cb, s_bw, rowb, cdt, bsz, seq):
    nt = seq // TM_IN

    def tile(i):
        t = jnp.maximum(i - IN_PROLOGUE_STEPS, 0)
        return t // nt, nt - 1 - t % nt

    def row_spec(width):
        return pl.BlockSpec((TM_IN, width), lambda i: (tile(i)[0] * nt + tile(i)[1], 0))

    def out(width):
        return jax.ShapeDtypeStruct((bsz * seq, width), BF16)

    rope = pl.BlockSpec((TM_IN, RET_DK // 2), lambda i: (tile(i)[1], 0))
    return pl.pallas_call(
        functools.partial(_inproj_kernel, tiles_per_seq=nt),
        name="inproj",
        grid=(IN_PROLOGUE_STEPS + bsz * nt,),
        in_specs=[
            row_spec(D_MODEL),
            pl.BlockSpec((1, N_MOD, D_MODEL), lambda i: (tile(i)[0], 0, 0)),
            pl.BlockSpec((1, D_MODEL), lambda i: (0, 0)),
            rope, rope,
            pl.BlockSpec(memory_space=pl.ANY),
            pl.BlockSpec(memory_space=pl.ANY),
            pl.BlockSpec(cw.shape, lambda i: (0, 0)),
            pl.BlockSpec((1, CONV_W), lambda i: (0, 0)),
            pl.BlockSpec((1, RET_HEADS, RET_DK, RET_DV), lambda i: (tile(i)[0], 0, 0, 0)),
            _const_spec(rowb.shape),
            _const_spec(cdt.shape),
        ],
        out_specs=(row_spec(D_MODEL), row_spec(RET_QK),
                   pl.BlockSpec((RET_QK, TM_IN), lambda i: (0, tile(i)[0] * nt + tile(i)[1])),
                   row_spec(RET_V), row_spec(RET_V), row_spec(D_MODEL), row_spec(RET_V)),
        out_shape=(out(D_MODEL), out(RET_QK), jax.ShapeDtypeStruct((RET_QK, bsz * seq), BF16),
                   out(RET_V), out(RET_V), out(D_MODEL), out(RET_V)),
        scratch_shapes=[pltpu.VMEM((RET_HEADS, RET_DK, RET_DV), F32),
                        pltpu.VMEM(win.shape, BF16), pltpu.VMEM(wco.shape, BF16),
                        pltpu.VMEM((2, win.shape[0], win.shape[1] // IN_W_CHUNKS), F32),
                        pltpu.VMEM((2, wco.shape[0] // CO_W_CHUNKS, wco.shape[1]), F32),
                        pltpu.SemaphoreType.DMA((2,)), pltpu.SemaphoreType.DMA((2,))],
        compiler_params=pltpu.CompilerParams(dimension_semantics=("arbitrary",),
                                             vmem_limit_bytes=VMEM_LIMIT),
    )(x2, modt, gain, cos, sin, win, wco, cw, cb, s_bw, rowb, cdt)


def _ret_fw_kernel(q_ref, kt_ref, v_ref, sg_ref, ob_ref, yc_ref, sgb_ref, x_ref, s0_ref, dmat_ref,
                   rowf_ref, cdt_ref, wro32_ref, wo32_ref, mod_ref, gpost_ref, wup32_ref, wdn32_ref,
                   x1_ref, wup_out_ref, wdn_out_ref, s_ref, gat_ref, wro_ref, wo_ref, *, steps_per_seq):
    step = pl.program_id(0)
    _cast_chunks(step, ((wro32_ref, wro_ref, 0, RET_W_CHUNKS), (wo32_ref, wo_ref, 0, RET_W_CHUNKS)))

    @pl.when(step >= RET_W_CHUNKS)
    def _():
        wup_out_ref[...] = wup32_ref[...].astype(BF16)
        wdn_out_ref[...] = wdn32_ref[...].astype(BF16)
        _ret_fw_tile(q_ref, kt_ref, v_ref, sg_ref, ob_ref, yc_ref, sgb_ref, x_ref, s0_ref, dmat_ref,
                     rowf_ref, cdt_ref, wro_ref, wo_ref, mod_ref, gpost_ref, x1_ref, s_ref, gat_ref,
                     (step - RET_W_CHUNKS) % steps_per_seq == 0)


def _ret_fw_tile(q_ref, kt_ref, v_ref, sg_ref, ob_ref, yc_ref, sgb_ref, x_ref, s0_ref, dmat_ref,
                 rowf_ref, cdt_ref, wro_ref, wo_ref, mod_ref, gpost_ref,
                 x1_ref, s_ref, gat_ref, first_tile):
    @pl.when(first_tile)
    def _():
        s_ref[...] = s0_ref[0]

    heads = range(RET_HEADS)
    for ci in range(TS_RET // RET_C):
        rows = slice(ci * RET_C, (ci + 1) * RET_C)
        qb = [q_ref[rows, h * RET_DK:(h + 1) * RET_DK] for h in heads]
        kt = [kt_ref[h * RET_DK:(h + 1) * RET_DK, rows] for h in heads]
        vh = [v_ref[rows, h * RET_DV:(h + 1) * RET_DV] for h in heads]
        scores = [(_dot(qb[h], kt[h]) * dmat_ref[h]).astype(BF16) for h in heads]
        s = [s_ref[h] for h in heads]
        o = [(_dot(scores[h], vh[h]) + _dot(qb[h] * rowf_ref[h, 0], s[h].astype(BF16))
              + ob_ref[rows, h * RET_DV:(h + 1) * RET_DV].astype(F32)) for h in heads]
        for h in heads:
            s_ref[h] = cdt_ref[h, 0] * s[h] + _dot(kt[h] * rowf_ref[h, 1], vh[h])
        for h in heads:
            gate = sg_ref[rows, h * RET_DV:(h + 1) * RET_DV].astype(F32)
            gat_ref[rows, h * RET_DV:(h + 1) * RET_DV] = (gate * _rms(o[h])).astype(BF16)

    for p in range(len(RET_EPI_BOUNDS) - 1):
        rows = slice(RET_EPI_BOUNDS[p], RET_EPI_BOUNDS[p + 1])
        y_ret = _dot(gat_ref[rows, :], wro_ref[...])
        merged = yc_ref[rows, :].astype(F32) + sgb_ref[rows, :].astype(F32) * y_ret
        y = _dot(merged.astype(BF16), wo_ref[...])
        x1_ref[rows, :] = x_ref[rows, :] + mod_ref[0][2:3] * (_rms(y) * gpost_ref[...])


def _ret_fw(q, k, v, sg, ob, yc, sgb, x2, s_fw, dmat, rowf, cdt, wro, wo, modt, gpost, wup, wdn,
            bsz, seq):
    ns = seq // TS_RET
    passed = [_pass_spec(wup.shape, bsz * ns, RET_W_CHUNKS),
              _pass_spec(wdn.shape, FFN_DN_PASS_STEPS, RET_W_CHUNKS)]

    def tile(i):
        return jnp.maximum(i - RET_W_CHUNKS, 0)

    def row_spec(width):
        return pl.BlockSpec((TS_RET, width), lambda i: (tile(i), 0))

    return pl.pallas_call(
        functools.partial(_ret_fw_kernel, steps_per_seq=ns),
        name="ret_fw",
        grid=(RET_W_CHUNKS + bsz * ns,),
        in_specs=[
            row_spec(RET_QK), pl.BlockSpec((RET_QK, TS_RET), lambda i: (0, tile(i))),
            row_spec(RET_V), row_spec(RET_V), row_spec(RET_V),
            row_spec(D_MODEL), row_spec(D_MODEL), row_spec(D_MODEL),
            pl.BlockSpec((1, RET_HEADS, RET_DK, RET_DV), lambda i: (tile(i) // ns, 0, 0, 0)),
            _const_spec(dmat.shape), _const_spec(rowf.shape), _const_spec(cdt.shape),
            _chunk_spec(wro.shape, RET_W_CHUNKS, 0), _chunk_spec(wo.shape, RET_W_CHUNKS, 0),
            pl.BlockSpec((1, N_MOD, D_MODEL), lambda i: (tile(i) // ns, 0, 0)),
            pl.BlockSpec((1, D_MODEL), lambda i: (0, 0)),
            *passed,
        ],
        out_specs=(row_spec(D_MODEL), *passed),
        out_shape=(jax.ShapeDtypeStruct((bsz * seq, D_MODEL), F32),
                   jax.ShapeDtypeStruct(wup.shape, BF16), jax.ShapeDtypeStruct(wdn.shape, BF16)),
        scratch_shapes=[pltpu.VMEM((RET_HEADS, RET_DK, RET_DV), F32),
                        pltpu.VMEM((TS_RET, RET_V), BF16),
                        pltpu.VMEM(wro.shape, BF16), pltpu.VMEM(wo.shape, BF16)],
        compiler_params=pltpu.CompilerParams(dimension_semantics=("arbitrary",),
                                             vmem_limit_bytes=VMEM_LIMIT),
    )(q, k, v, sg, ob, yc, sgb, x2, s_fw, dmat, rowf, cdt, wro, wo, modt, gpost, wup, wdn)


def _ffn_kernel(x1_ref, mod_ref, gpre_ref, gpost_ref, wup_ref, cw_ref, cb_ref, wdn_ref,
                out_ref, acc_ref, *, grid_rows):
    tm = grid_rows * FFN_COLS
    x1 = x1_ref[0].reshape(tm, D_MODEL)
    m = mod_ref[0]
    hb = (_rms(x1) * gpre_ref[...] * (1.0 + m[4:5]) + m[3:4]).astype(BF16)

    def conv(u, off, width):
        w = cw_ref[:, off:off + width]
        pad = jnp.zeros((FFN_COLS, width), F32)
        above = jnp.concatenate([pad, u[:tm - FFN_COLS]], axis=0)
        below = jnp.concatenate([u[FFN_COLS:], pad], axis=0)
        return w[0:1] * above + w[1:2] * u + w[2:3] * below + cb_ref[:, off:off + width]

    def up(c0, width):
        return (_dot(hb, wup_ref[:, c0:c0 + width]),
                _dot(hb, wup_ref[:, D_FF + c0:D_FF + c0 + width]))

    ahead = up(*FFN_BLOCKS[0])
    for j, (c0, width) in enumerate(FFN_BLOCKS):
        ug, uv = ahead
        if j + 1 < len(FFN_BLOCKS):
            ahead = up(*FFN_BLOCKS[j + 1])
        g = conv(ug, c0, width)
        val = conv(uv, D_FF + c0, width)
        a = (g * _sigmoid(g) * val).astype(BF16)
        if j == 0:
            acc_ref[...] = _dot(a, wdn_ref[c0:c0 + width, :])
        elif j + 1 < len(FFN_BLOCKS):
            acc_ref[...] += _dot(a, wdn_ref[c0:c0 + width, :])
        else:
            gr = grid_rows // FFN_EPI_PIECES
            for p in range(FFN_EPI_PIECES):
                rows = slice(p * gr * FFN_COLS, (p + 1) * gr * FFN_COLS)
                f = acc_ref[rows, :] + _dot(a[rows, :], wdn_ref[c0:c0 + width, :])
                res = x1[rows, :] + m[5:6] * (_rms(f) * gpost_ref[...])
                out_ref[0, p * gr:(p + 1) * gr] = res.reshape(gr, FFN_COLS, D_MODEL)


def _ffn(x1g, modt, gpre, gpost, wup, cw, cb, wdn):
    bsz, grid_rows, grid_w, _ = x1g.shape
    strips = grid_w // FFN_COLS
    strip = pl.BlockSpec((1, grid_rows, FFN_COLS, D_MODEL), lambda i: (i // strips, 0, i % strips, 0))
    vec = pl.BlockSpec((1, D_MODEL), lambda i: (0, 0))
    return pl.pallas_call(
        functools.partial(_ffn_kernel, grid_rows=grid_rows),
        name="ffn",
        grid=(bsz * strips,),
        in_specs=[
            strip,
            pl.BlockSpec((1, N_MOD, D_MODEL), lambda i: (i // strips, 0, 0)),
            vec, vec,
            _const_spec(wup.shape),
            pl.BlockSpec(cw.shape, lambda i: (0, 0)),
            pl.BlockSpec((1, 2 * D_FF), lambda i: (0, 0)),
            _const_spec(wdn.shape),
        ],
        out_specs=strip,
        out_shape=jax.ShapeDtypeStruct(x1g.shape, F32),
        scratch_shapes=[pltpu.VMEM((grid_rows * FFN_COLS, D_MODEL), F32)],
        compiler_params=pltpu.CompilerParams(dimension_semantics=("arbitrary",),
                                             vmem_limit_bytes=VMEM_LIMIT),
    )(x1g, modt, gpre, gpost, wup, cw, cb, wdn)


def _rotary_tables(n):
    t = np.arange(n)
    row = (t // GRID_W).astype(np.float64)
    col = (t % GRID_W).astype(np.float64)
    inv = ROPE_BASE ** (-np.arange(ROPE_PAIRS, dtype=np.float64) / ROPE_PAIRS)
    ang = np.concatenate([row[:, None] * inv, col[:, None] * inv], axis=-1)
    return jnp.asarray(np.cos(ang), F32), jnp.asarray(np.sin(ang), F32)


def kernel(x, c, ctx, c_ctx, w_ada, b_ada, norm_mix_pre, norm_mix_post, w_in, conv_w, conv_b,
           w_conv_out, ret_decay_fw, ret_decay_bw, w_ret_out, w_o, norm_ffn_pre, norm_ffn_post,
           w_ffn_up, ffn_conv_w, ffn_conv_b, w_ffn_down):
    bsz, seq, d = x.shape
    depth = w_ada.shape[0]
    assert d == D_MODEL and depth == 1 and bsz + 1 <= MOD_ROWS
    assert seq % TS_RET == 0 and seq % TM_IN == 0 and TM_IN % GRID_W == 0 and GRID_W % FFN_COLS == 0
    lc = ctx.shape[1]
    cos, sin = _rotary_tables(seq)
    x2 = x.reshape(bsz * seq, d)


    theta = jnp.concatenate([ret_decay_fw[0], ret_decay_bw[0]]).astype(F32)
    theta = jnp.broadcast_to(theta[:, None], (2 * RET_HEADS, RET_DV))
    modt, s_fw, s_bw, dmat, rowf, rowb, cdt = _prep(c, c_ctx, w_ada[0], b_ada, ctx, norm_mix_pre,
                                                    w_in[0], theta)

    yc, q, k, v, sg, sgb, ob = _inproj(x2, modt, norm_mix_pre, cos, sin, w_in[0], w_conv_out[0],
                                       conv_w[0], conv_b, s_bw, rowb, cdt, bsz, seq)
    x1, wup, wdn = _ret_fw(q, k, v, sg, ob, yc, sgb, x2, s_fw, dmat, rowf, cdt, w_ret_out[0], w_o[0],
                           modt, norm_mix_post, w_ffn_up[0], w_ffn_down[0], bsz, seq)
    out = _ffn(x1.reshape(bsz, seq // GRID_W, GRID_W, d), modt, norm_ffn_pre, norm_ffn_post,
               wup, ffn_conv_w[0], ffn_conv_b, wdn)
    return out.reshape(bsz, seq, d)
```

```python
import functools

import jax
import jax.numpy as jnp
import numpy as np
from jax import lax
from jax.experimental import pallas as pl
from jax.experimental.pallas import tpu as pltpu

F32 = jnp.float32
BF16 = jnp.bfloat16

D_MODEL = 1024
GRID_W = 64
CONV_W = 1024
RET_HEADS = 4
RET_DK = 256
RET_DV = 512
RET_QK = RET_HEADS * RET_DK
RET_V = RET_HEADS * RET_DV
ROPE_PAIRS = RET_DK // 4
ROPE_BASE = 10000.0
D_FF = 2816
N_MOD = 6
EPS = 1e-6

OFF_CX = 0
OFF_CB = OFF_CX + CONV_W
OFF_CC = OFF_CB + CONV_W
OFF_Q = OFF_CC + CONV_W
OFF_K = OFF_Q + RET_QK
OFF_V = OFF_K + RET_QK
OFF_G = OFF_V + RET_V
OFF_GA = OFF_G + RET_V
OFF_GB = OFF_GA + D_MODEL
IN_COLS = OFF_GB + D_MODEL

RET_C = 256
TM_IN = RET_C
TS_RET = 512
RET_EPI_BOUNDS = (0, TS_RET // 2, TS_RET)
FFN_COLS = 8
FFN_BLOCKS = ((0, 1024), (1024, 1024), (2048, 768))
FFN_EPI_PIECES = 4
IN_PROLOGUE_STEPS = 1
IN_W_CHUNKS = 16
CO_W_CHUNKS = 8
RET_W_CHUNKS = 4
FFN_DN_PASS_STEPS = 16
MOD_ROWS = 8
CTX_FIRST_STEP = 2
BF16_SUBLANES = 16
V7X_VMEM_BYTES = 64 * 1024 * 1024
VMEM_LIMIT = V7X_VMEM_BYTES - 4 * 1024 * 1024


def _const_spec(shape):
    zeros = (0,) * len(shape)
    return pl.BlockSpec(shape, lambda *_: zeros, pipeline_mode=pl.Buffered(1))


def _chunk_spec(shape, n_chunks, axis):
    block = list(shape)
    block[axis] //= n_chunks
    assert block[axis] * n_chunks == shape[axis]

    def index_map(i):
        c = jnp.minimum(i, n_chunks - 1)
        return (c, 0) if axis == 0 else (0, c)

    return pl.BlockSpec(tuple(block), index_map)


def _pass_spec(shape, n_steps, first_step=0):
    rows = shape[0] // n_steps
    assert rows * n_steps == shape[0] and rows % BF16_SUBLANES == 0
    return pl.BlockSpec((rows, shape[1]), lambda i: (jnp.clip(i - first_step, 0, n_steps - 1), 0))


def _stream_cast(src_hbm, dst_ref, stage_ref, sem, n_chunks, axis):
    size = src_hbm.shape[axis] // n_chunks
    assert size * n_chunks == src_hbm.shape[axis]

    def window(c):
        chunk = slice(c * size, (c + 1) * size)
        return (chunk, slice(None)) if axis == 0 else (slice(None), chunk)

    def copy(c):
        slot = c % 2
        return pltpu.make_async_copy(src_hbm.at[window(c)], stage_ref.at[slot], sem.at[slot])

    copy(0).start()
    for c in range(n_chunks):
        if c + 1 < n_chunks:
            copy(c + 1).start()
        copy(c).wait()
        dst_ref[window(c)] = stage_ref[c % 2].astype(BF16)


def _cast_chunks(step, weights):
    n_steps = max(n for _, _, _, n in weights)

    @pl.when(step < n_steps)
    def _():
        for c in range(n_steps):
            @pl.when(step == c)
            def _():
                for src_ref, dst_ref, axis, n in weights:
                    if c < n:
                        size = src_ref.shape[axis]
                        if axis == 0:
                            dst_ref[c * size:(c + 1) * size, :] = src_ref[...].astype(BF16)
                        else:
                            dst_ref[:, c * size:(c + 1) * size] = src_ref[...].astype(BF16)


def _rms(xf):
    return xf * lax.rsqrt(jnp.mean(xf * xf, axis=-1, keepdims=True) + EPS)


def _sigmoid(x):
    return 1.0 / (1.0 + jnp.exp(-x))


def _dot(a, b):
    return jnp.dot(a, b, preferred_element_type=F32)


def _dot_nt(a, b):
    return lax.dot_general(a, b, (((1,), (1,)), ((), ())), preferred_element_type=F32)


def _dot_tn(a, b):
    return lax.dot_general(a, b, (((0,), (0,)), ((), ())), preferred_element_type=F32)


def _tables_kernel(theta_ref, dmat_ref, rowf_ref, rowb_ref, cdt_ref, ctxw_ref, *, ctx_len):
    th = theta_ref[...]
    lg = -(jnp.maximum(-th, 0.0) + jnp.log1p(jnp.exp(-jnp.abs(th))))
    n_i = lax.broadcasted_iota(jnp.int32, (RET_C, RET_C), 0)
    m_i = lax.broadcasted_iota(jnp.int32, (RET_C, RET_C), 1)
    diff = (n_i - m_i).astype(F32)
    idx = lax.broadcasted_iota(jnp.int32, (RET_C, RET_DK), 0).astype(F32)
    tok = lax.broadcasted_iota(jnp.int32, (RET_DK, RET_C), 1).astype(F32)
    midx = lax.broadcasted_iota(jnp.int32, (ctx_len, RET_DK), 0).astype(F32)
    for h in range(RET_HEADS):
        lf = lg[h:h + 1, :]
        lb = lg[RET_HEADS + h:RET_HEADS + h + 1, :]
        lf_c, lb_c = lf[:, :RET_C], lb[:, :RET_C]
        dmat_ref[h] = jnp.exp(jnp.where(diff >= 0.0, diff * lf_c, -diff * lb_c))
        lf_k, lb_k = lf[:, :RET_DK], lb[:, :RET_DK]
        rowf_ref[h, 0] = jnp.exp((idx + 1.0) * lf_k).astype(BF16)
        rowf_ref[h, 1] = jnp.exp((RET_C - 1.0 - tok) * lf_k).astype(BF16)
        rowb_ref[h, 0] = jnp.exp((RET_C - idx) * lb_k).astype(BF16)
        rowb_ref[h, 1] = jnp.exp(tok * lb_k).astype(BF16)
        cdt_ref[h, 0] = jnp.exp(RET_C * lf)
        cdt_ref[h, 1] = jnp.exp(RET_C * lb)
        ctxw_ref[h, 0] = jnp.exp((ctx_len - 1.0 - midx) * lf_k)
        ctxw_ref[h, 1] = jnp.exp(midx * lb_k)


def _prep_kernel(c_ref, cctx_ref, wada_ref, bada_ref, ctx_ref, gain_ref, w_ref, theta_ref,
                 mod_ref, sfw_ref, sbw_ref, dmat_ref, rowf_ref, rowb_ref, cdt_ref,
                 hc_ref, kc_ref, ctxw_ref):
    j = pl.program_id(0)
    bsz, lc, _ = ctx_ref.shape

    cv = jnp.concatenate([c_ref[...], cctx_ref[...], jnp.zeros((MOD_ROWS - bsz - 1, D_MODEL), F32)], axis=0)
    s = cv * _sigmoid(cv)
    res = _dot(s.astype(BF16), wada_ref[...].astype(BF16)) + bada_ref[...]
    for k in range(N_MOD):
        @pl.when(j == k)
        def _():
            mod_ref[:, k, :] = res

    heads_per_step = RET_QK // RET_DV

    def project():
        return _dot(hc_ref[...], w_ref[...].astype(BF16))

    @pl.when(j == CTX_FIRST_STEP)
    def _():
        _tables_kernel(theta_ref, dmat_ref, rowf_ref, rowb_ref, cdt_ref, ctxw_ref, ctx_len=lc)
        shift = mod_ref[bsz:bsz + 1, 0, :]
        scale = mod_ref[bsz:bsz + 1, 1, :]
        xc = ctx_ref[...].reshape(bsz * lc, D_MODEL)
        hc_ref[...] = (_rms(xc) * gain_ref[...] * (1.0 + scale) + shift).astype(BF16)
        kc_ref[...] = project() * RET_DK ** -0.5

    for step in range(1, RET_HEADS // heads_per_step + 1):
        @pl.when(j == CTX_FIRST_STEP + step)
        def _():
            p = project()
            for hh in range(heads_per_step):
                h = (step - 1) * heads_per_step + hh
                for b in range(bsz):
                    rows = slice(b * lc, (b + 1) * lc)
                    kh = kc_ref[rows, h * RET_DK:(h + 1) * RET_DK]
                    vh = p[rows, hh * RET_DV:(hh + 1) * RET_DV].astype(BF16)
                    kfb = jnp.concatenate([(kh * ctxw_ref[h, 0]).astype(BF16),
                                           (kh * ctxw_ref[h, 1]).astype(BF16)], axis=1)
                    s_both = _dot_tn(kfb, vh)
                    sfw_ref[b, hh] = s_both[:RET_DK]
                    sbw_ref[b, hh] = s_both[RET_DK:]


def _prep(c, c_ctx, w_ada, b_ada, ctx, gain, w_in, theta):
    bsz, lc, _ = ctx.shape
    assert w_ada.shape[1] == N_MOD * D_MODEL
    assert OFF_K % RET_QK == 0 and OFF_V == OFF_K + RET_QK and RET_QK % RET_DV == 0
    heads_per_step = RET_QK // RET_DV
    ctx_steps = 1 + RET_HEADS // heads_per_step
    assert CTX_FIRST_STEP + ctx_steps <= N_MOD
    state = jax.ShapeDtypeStruct((bsz, RET_HEADS, RET_DK, RET_DV), F32)
    sspec = pl.BlockSpec((bsz, heads_per_step, RET_DK, RET_DV),
                         lambda j: (0, jnp.clip(j - CTX_FIRST_STEP - 1, 0, ctx_steps - 2), 0, 0))
    rows = jax.ShapeDtypeStruct((RET_HEADS, 2, RET_C, RET_DK), BF16)
    mod = jax.ShapeDtypeStruct((MOD_ROWS, N_MOD, D_MODEL), F32)
    tables = (jax.ShapeDtypeStruct((RET_HEADS, RET_C, RET_C), F32), rows, rows,
              jax.ShapeDtypeStruct((RET_HEADS, 2, 1, RET_DV), F32))

    def whole(shape):
        return pl.BlockSpec(shape, lambda j: (0,) * len(shape))

    return pl.pallas_call(
        _prep_kernel,
        name="prep",
        grid=(N_MOD,),
        in_specs=[
            whole(c.shape),
            whole((1, D_MODEL)),
            pl.BlockSpec((D_MODEL, D_MODEL), lambda j: (0, j)),
            pl.BlockSpec((1, D_MODEL), lambda j: (0, j)),
            whole(ctx.shape),
            whole((1, D_MODEL)),
            pl.BlockSpec((D_MODEL, RET_QK),
                         lambda j: (0, OFF_K // RET_QK + jnp.clip(j - CTX_FIRST_STEP, 0, ctx_steps - 1))),
            whole(theta.shape),
        ],
        out_specs=(whole(mod.shape), sspec, sspec, *[whole(t.shape) for t in tables]),
        out_shape=(mod, state, state, *tables),
        scratch_shapes=[pltpu.VMEM((bsz * lc, D_MODEL), BF16), pltpu.VMEM((bsz * lc, RET_QK), F32),
                        pltpu.VMEM((RET_HEADS, 2, lc, RET_DK), F32)],
        compiler_params=pltpu.CompilerParams(dimension_semantics=("arbitrary",),
                                             vmem_limit_bytes=VMEM_LIMIT),
    )(c, c_ctx.reshape(1, D_MODEL), w_ada, b_ada, ctx, gain, w_in, theta)


def _inproj_kernel(x_ref, mod_ref, gain_ref, cos_ref, sin_ref, win32_ref, wco32_ref, cw_ref, cb_ref,
                   s0_ref, rowb_ref, cdt_ref,
                   yc_ref, q_ref, kt_ref, v_ref, sg_ref, sgb_ref, ob_ref, t_ref, win_ref, wco_ref,
                   win_stage_ref, wco_stage_ref, win_sem, wco_sem, *, tiles_per_seq):
    step = pl.program_id(0)

    @pl.when(step == 0)
    def _():
        _stream_cast(win32_ref, win_ref, win_stage_ref, win_sem, IN_W_CHUNKS, 0)
        _stream_cast(wco32_ref, wco_ref, wco_stage_ref, wco_sem, CO_W_CHUNKS, 0)

    @pl.when(step >= IN_PROLOGUE_STEPS)
    def _():
        _inproj_tile(x_ref, mod_ref, gain_ref, cos_ref, sin_ref, win_ref, wco_ref, cw_ref, cb_ref,
                     s0_ref, rowb_ref, cdt_ref, yc_ref, q_ref, kt_ref, v_ref, sg_ref, sgb_ref, ob_ref,
                     t_ref, (step - IN_PROLOGUE_STEPS) % tiles_per_seq == 0)


def _inproj_tile(x_ref, mod_ref, gain_ref, cos_ref, sin_ref, win_ref, wco_ref, cw_ref, cb_ref,
                 s0_ref, rowb_ref, cdt_ref,
                 yc_ref, q_ref, kt_ref, v_ref, sg_ref, sgb_ref, ob_ref, t_ref, first_tile):
    @pl.when(first_tile)
    def _():
        t_ref[...] = s0_ref[0]

    m = mod_ref[0]
    h = _rms(x_ref[...]) * gain_ref[...] * (1.0 + m[1:2]) + m[0:1]
    hb = h.astype(BF16)

    def proj(off, width):
        return _dot(hb, win_ref[:, off:off + width])

    u = proj(OFF_CC, CONV_W) * proj(OFF_CX, CONV_W)
    col = lax.broadcasted_iota(jnp.int32, (TM_IN, CONV_W), 0) % GRID_W
    prev = jnp.where(col == 0, 0.0, pltpu.roll(u, 1, 0))
    nxt = jnp.where(col == GRID_W - 1, 0.0, pltpu.roll(u, TM_IN - 1, 0))
    cw = cw_ref[...]
    conv = cw[0:1] * prev + cw[1:2] * u + cw[2:3] * nxt + cb_ref[...]
    z = (proj(OFF_CB, CONV_W) * conv).astype(BF16)
    yc_ref[...] = (_sigmoid(proj(OFF_GA, D_MODEL)) * _dot(z, wco_ref[...])).astype(BF16)
    sgb_ref[...] = _sigmoid(proj(OFF_GB, D_MODEL)).astype(BF16)

    cos = cos_ref[...]
    sin = sin_ref[...]
    half = RET_DK // 2
    for off, ref, scale in ((OFF_Q, q_ref, 1.0), (OFF_K, kt_ref, RET_DK ** -0.5)):
        t = proj(off, RET_QK)
        for hh in range(RET_HEADS):
            lo = hh * RET_DK
            t1 = t[:, lo:lo + half]
            t2 = t[:, lo + half:lo + RET_DK]
            r1 = (t1 * cos - t2 * sin) * scale
            r2 = (t2 * cos + t1 * sin) * scale
            if ref is q_ref:
                ref[:, lo:lo + half] = r1.astype(BF16)
                ref[:, lo + half:lo + RET_DK] = r2.astype(BF16)
            else:
                ref[lo:lo + half, :] = r1.T.astype(BF16)
                ref[lo + half:lo + RET_DK, :] = r2.T.astype(BF16)
    v_ref[...] = proj(OFF_V, RET_V).astype(BF16)
    g = proj(OFF_G, RET_V)
    sg_ref[...] = (g * _sigmoid(g)).astype(BF16)

    for hh in range(RET_HEADS):
        qs = q_ref[:, hh * RET_DK:(hh + 1) * RET_DK] * rowb_ref[hh, 0]
        kst = kt_ref[hh * RET_DK:(hh + 1) * RET_DK, :] * rowb_ref[hh, 1]
        vh = v_ref[:, hh * RET_DV:(hh + 1) * RET_DV]
        t = t_ref[hh]
        ob_ref[:, hh * RET_DV:(hh + 1) * RET_DV] = _dot(qs, t.astype(BF16)).astype(BF16)
        t_ref[hh] = cdt_ref[hh, 1] * t + _dot(kst, vh)


def _inproj(x2, modt, gain, cos, sin, win, wco, cw, cb, s_bw, rowb, cdt, bsz, seq):
    nt = seq // TM_IN

    def tile(i):
        t = jnp.maximum(i - IN_PROLOGUE_STEPS, 0)
        return t // nt, nt - 1 - t % nt

    def row_spec(width):
        return pl.BlockSpec((TM_IN, width), lambda i: (tile(i)[0] * nt + tile(i)[1], 0))

    def out(width):
        return jax.ShapeDtypeStruct((bsz * seq, width), BF16)

    rope = pl.BlockSpec((TM_IN, RET_DK // 2), lambda i: (tile(i)[1], 0))
    return pl.pallas_call(
        functools.partial(_inproj_kernel, tiles_per_seq=nt),
        name="inproj",
        grid=(IN_PROLOGUE_STEPS + bsz * nt,),
        in_specs=[
            row_spec(D_MODEL),
            pl.BlockSpec((1, N_MOD, D_MODEL), lambda i: (tile(i)[0], 0, 0)),
            pl.BlockSpec((1, D_MODEL), lambda i: (0, 0)),
            rope, rope,
            pl.BlockSpec(memory_space=pl.ANY),
            pl.BlockSpec(memory_space=pl.ANY),
            pl.BlockSpec(cw.shape, lambda i: (0, 0)),
            pl.BlockSpec((1, CONV_W), lambda i: (0, 0)),
            pl.BlockSpec((1, RET_HEADS, RET_DK, RET_DV), lambda i: (tile(i)[0], 0, 0, 0)),
            _const_spec(rowb.shape),
            _const_spec(cdt.shape),
        ],
        out_specs=(row_spec(D_MODEL), row_spec(RET_QK),
                   pl.BlockSpec((RET_QK, TM_IN), lambda i: (0, tile(i)[0] * nt + tile(i)[1])),
                   row_spec(RET_V), row_spec(RET_V), row_spec(D_MODEL), row_spec(RET_V)),
        out_shape=(out(D_MODEL), out(RET_QK), jax.ShapeDtypeStruct((RET_QK, bsz * seq), BF16),
                   out(RET_V), out(RET_V), out(D_MODEL), out(RET_V)),
        scratch_shapes=[pltpu.VMEM((RET_HEADS, RET_DK, RET_DV), F32),
                        pltpu.VMEM(win.shape, BF16), pltpu.VMEM(wco.shape, BF16),
                        pltpu.VMEM((2, win.shape[0] // IN_W_CHUNKS, win.shape[1]), F32),
                        pltpu.VMEM((2, wco.shape[0] // CO_W_CHUNKS, wco.shape[1]), F32),
                        pltpu.SemaphoreType.DMA((2,)), pltpu.SemaphoreType.DMA((2,))],
        compiler_params=pltpu.CompilerParams(dimension_semantics=("arbitrary",),
                                             vmem_limit_bytes=VMEM_LIMIT),
    )(x2, modt, gain, cos, sin, win, wco, cw, cb, s_bw, rowb, cdt)


def _ret_fw_kernel(q_ref, kt_ref, v_ref, sg_ref, ob_ref, yc_ref, sgb_ref, x_ref, s0_ref, dmat_ref,
                   rowf_ref, cdt_ref, wro32_ref, wo32_ref, mod_ref, gpost_ref, wup32_ref, wdn32_ref,
                   x1_ref, wup_out_ref, wdn_out_ref, s_ref, gat_ref, wro_ref, wo_ref, *, steps_per_seq):
    step = pl.program_id(0)
    _cast_chunks(step, ((wro32_ref, wro_ref, 0, RET_W_CHUNKS), (wo32_ref, wo_ref, 0, RET_W_CHUNKS)))

    @pl.when(step >= RET_W_CHUNKS)
    def _():
        wup_out_ref[...] = wup32_ref[...].astype(BF16)
        wdn_out_ref[...] = wdn32_ref[...].astype(BF16)
        _ret_fw_tile(q_ref, kt_ref, v_ref, sg_ref, ob_ref, yc_ref, sgb_ref, x_ref, s0_ref, dmat_ref,
                     rowf_ref, cdt_ref, wro_ref, wo_ref, mod_ref, gpost_ref, x1_ref, s_ref, gat_ref,
                     (step - RET_W_CHUNKS) % steps_per_seq == 0)


def _ret_fw_tile(q_ref, kt_ref, v_ref, sg_ref, ob_ref, yc_ref, sgb_ref, x_ref, s0_ref, dmat_ref,
                 rowf_ref, cdt_ref, wro_ref, wo_ref, mod_ref, gpost_ref,
                 x1_ref, s_ref, gat_ref, first_tile):
    @pl.when(first_tile)
    def _():
        s_ref[...] = s0_ref[0]

    heads = range(RET_HEADS)
    for ci in range(TS_RET // RET_C):
        rows = slice(ci * RET_C, (ci + 1) * RET_C)
        qb = [q_ref[rows, h * RET_DK:(h + 1) * RET_DK] for h in heads]
        kt = [kt_ref[h * RET_DK:(h + 1) * RET_DK, rows] for h in heads]
        vh = [v_ref[rows, h * RET_DV:(h + 1) * RET_DV] for h in heads]
        scores = [(_dot(qb[h], kt[h]) * dmat_ref[h]).astype(BF16) for h in heads]
        s = [s_ref[h] for h in heads]
        o = [(_dot(scores[h], vh[h]) + _dot(qb[h] * rowf_ref[h, 0], s[h].astype(BF16))
              + ob_ref[rows, h * RET_DV:(h + 1) * RET_DV].astype(F32)) for h in heads]
        for h in heads:
            s_ref[h] = cdt_ref[h, 0] * s[h] + _dot(kt[h] * rowf_ref[h, 1], vh[h])
        for h in heads:
            gate = sg_ref[rows, h * RET_DV:(h + 1) * RET_DV].astype(F32)
            gat_ref[rows, h * RET_DV:(h + 1) * RET_DV] = (gate * _rms(o[h])).astype(BF16)

    for p in range(len(RET_EPI_BOUNDS) - 1):
        rows = slice(RET_EPI_BOUNDS[p], RET_EPI_BOUNDS[p + 1])
        y_ret = _dot(gat_ref[rows, :], wro_ref[...])
        merged = yc_ref[rows, :].astype(F32) + sgb_ref[rows, :].astype(F32) * y_ret
        y = _dot(merged.astype(BF16), wo_ref[...])
        x1_ref[rows, :] = x_ref[rows, :] + mod_ref[0][2:3] * (_rms(y) * gpost_ref[...])


def _ret_fw(q, k, v, sg, ob, yc, sgb, x2, s_fw, dmat, rowf, cdt, wro, wo, modt, gpost, wup, wdn,
            bsz, seq):
    ns = seq // TS_RET
    passed = [_pass_spec(wup.shape, bsz * ns, RET_W_CHUNKS),
              _pass_spec(wdn.shape, FFN_DN_PASS_STEPS, RET_W_CHUNKS)]

    def tile(i):
        return jnp.maximum(i - RET_W_CHUNKS, 0)

    def row_spec(width):
        return pl.BlockSpec((TS_RET, width), lambda i: (tile(i), 0))

    return pl.pallas_call(
        functools.partial(_ret_fw_kernel, steps_per_seq=ns),
        name="ret_fw",
        grid=(RET_W_CHUNKS + bsz * ns,),
        in_specs=[
            row_spec(RET_QK), pl.BlockSpec((RET_QK, TS_RET), lambda i: (0, tile(i))),
            row_spec(RET_V), row_spec(RET_V), row_spec(RET_V),
            row_spec(D_MODEL), row_spec(D_MODEL), row_spec(D_MODEL),
            pl.BlockSpec((1, RET_HEADS, RET_DK, RET_DV), lambda i: (tile(i) // ns, 0, 0, 0)),
            _const_spec(dmat.shape), _const_spec(rowf.shape), _const_spec(cdt.shape),
            _chunk_spec(wro.shape, RET_W_CHUNKS, 0), _chunk_spec(wo.shape, RET_W_CHUNKS, 0),
            pl.BlockSpec((1, N_MOD, D_MODEL), lambda i: (tile(i) // ns, 0, 0)),
            pl.BlockSpec((1, D_MODEL), lambda i: (0, 0)),
            *passed,
        ],
        out_specs=(row_spec(D_MODEL), *passed),
        out_shape=(jax.ShapeDtypeStruct((bsz * seq, D_MODEL), F32),
                   jax.ShapeDtypeStruct(wup.shape, BF16), jax.ShapeDtypeStruct(wdn.shape, BF16)),
        scratch_shapes=[pltpu.VMEM((RET_HEADS, RET_DK, RET_DV), F32),
                        pltpu.VMEM((TS_RET, RET_V), BF16),
                        pltpu.VMEM(wro.shape, BF16), pltpu.VMEM(wo.shape, BF16)],
        compiler_params=pltpu.CompilerParams(dimension_semantics=("arbitrary",),
                                             vmem_limit_bytes=VMEM_LIMIT),
    )(q, k, v, sg, ob, yc, sgb, x2, s_fw, dmat, rowf, cdt, wro, wo, modt, gpost, wup, wdn)


def _ffn_kernel(x1_ref, mod_ref, gpre_ref, gpost_ref, wup_ref, cw_ref, cb_ref, wdn_ref,
                out_ref, acc_ref, *, grid_rows):
    tm = grid_rows * FFN_COLS
    x1 = x1_ref[0].reshape(tm, D_MODEL)
    m = mod_ref[0]
    hb = (_rms(x1) * gpre_ref[...] * (1.0 + m[4:5]) + m[3:4]).astype(BF16)

    def conv(u, off, width):
        w = cw_ref[:, off:off + width]
        pad = jnp.zeros((FFN_COLS, width), F32)
        above = jnp.concatenate([pad, u[:tm - FFN_COLS]], axis=0)
        below = jnp.concatenate([u[FFN_COLS:], pad], axis=0)
        return w[0:1] * above + w[1:2] * u + w[2:3] * below + cb_ref[:, off:off + width]

    def up(c0, width):
        return (_dot(hb, wup_ref[:, c0:c0 + width]),
                _dot(hb, wup_ref[:, D_FF + c0:D_FF + c0 + width]))

    ahead = up(*FFN_BLOCKS[0])
    for j, (c0, width) in enumerate(FFN_BLOCKS):
        ug, uv = ahead
        if j + 1 < len(FFN_BLOCKS):
            ahead = up(*FFN_BLOCKS[j + 1])
        g = conv(ug, c0, width)
        val = conv(uv, D_FF + c0, width)
        a = (g * _sigmoid(g) * val).astype(BF16)
        if j == 0:
            acc_ref[...] = _dot(a, wdn_ref[c0:c0 + width, :])
        elif j + 1 < len(FFN_BLOCKS):
            acc_ref[...] += _dot(a, wdn_ref[c0:c0 + width, :])
        else:
            gr = grid_rows // FFN_EPI_PIECES
            for p in range(FFN_EPI_PIECES):
                rows = slice(p * gr * FFN_COLS, (p + 1) * gr * FFN_COLS)
                f = acc_ref[rows, :] + _dot(a[rows, :], wdn_ref[c0:c0 + width, :])
                res = x1[rows, :] + m[5:6] * (_rms(f) * gpost_ref[...])
                out_ref[0, p * gr:(p + 1) * gr] = res.reshape(gr, FFN_COLS, D_MODEL)


def _ffn(x1g, modt, gpre, gpost, wup, cw, cb, wdn):
    bsz, grid_rows, grid_w, _ = x1g.shape
    strips = grid_w // FFN_COLS
    strip = pl.BlockSpec((1, grid_rows, FFN_COLS, D_MODEL), lambda i: (i // strips, 0, i % strips, 0))
    vec = pl.BlockSpec((1, D_MODEL), lambda i: (0, 0))
    return pl.pallas_call(
        functools.partial(_ffn_kernel, grid_rows=grid_rows),
        name="ffn",
        grid=(bsz * strips,),
        in_specs=[
            strip,
            pl.BlockSpec((1, N_MOD, D_MODEL), lambda i: (i // strips, 0, 0)),
            vec, vec,
            _const_spec(wup.shape),
            pl.BlockSpec(cw.shape, lambda i: (0, 0)),
            pl.BlockSpec((1, 2 * D_FF), lambda i: (0, 0)),
            _const_spec(wdn.shape),
        ],
        out_specs=strip,
        out_shape=jax.ShapeDtypeStruct(x1g.shape, F32),
        scratch_shapes=[pltpu.VMEM((grid_rows * FFN_COLS, D_MODEL), F32)],
        compiler_params=pltpu.CompilerParams(dimension_semantics=("arbitrary",),
                                             vmem_limit_bytes=VMEM_LIMIT),
    )(x1g, modt, gpre, gpost, wup, cw, cb, wdn)


def _rotary_tables(n):
    t = np.arange(n)
    row = (t // GRID_W).astype(np.float64)
    col = (t % GRID_W).astype(np.float64)
    inv = ROPE_BASE ** (-np.arange(ROPE_PAIRS, dtype=np.float64) / ROPE_PAIRS)
    ang = np.concatenate([row[:, None] * inv, col[:, None] * inv], axis=-1)
    return jnp.asarray(np.cos(ang), F32), jnp.asarray(np.sin(ang), F32)


def kernel(x, c, ctx, c_ctx, w_ada, b_ada, norm_mix_pre, norm_mix_post, w_in, conv_w, conv_b,
           w_conv_out, ret_decay_fw, ret_decay_bw, w_ret_out, w_o, norm_ffn_pre, norm_ffn_post,
           w_ffn_up, ffn_conv_w, ffn_conv_b, w_ffn_down):
    bsz, seq, d = x.shape
    depth = w_ada.shape[0]
    assert d == D_MODEL and depth == 1 and bsz + 1 <= MOD_ROWS
    assert seq % TS_RET == 0 and seq % TM_IN == 0 and TM_IN % GRID_W == 0 and GRID_W % FFN_COLS == 0
    lc = ctx.shape[1]
    cos, sin = _rotary_tables(seq)
    x2 = x.reshape(bsz * seq, d)


    theta = jnp.concatenate([ret_decay_fw[0], ret_decay_bw[0]]).astype(F32)
    theta = jnp.broadcast_to(theta[:, None], (2 * RET_HEADS, RET_DV))
    modt, s_fw, s_bw, dmat, rowf, rowb, cdt = _prep(c, c_ctx, w_ada[0], b_ada, ctx, norm_mix_pre,
                                                    w_in[0], theta)

    yc, q, k, v, sg, sgb, ob = _inproj(x2, modt, norm_mix_pre, cos, sin, w_in[0], w_conv_out[0],
                                       conv_w[0], conv_b, s_bw, rowb, cdt, bsz, seq)
    x1, wup, wdn = _ret_fw(q, k, v, sg, ob, yc, sgb, x2, s_fw, dmat, rowf, cdt, w_ret_out[0], w_o[0],
                           modt, norm_mix_post, w_ffn_up[0], w_ffn_down[0], bsz, seq)
    out = _ffn(x1.reshape(bsz, seq // GRID_W, GRID_W, d), modt, norm_ffn_pre, norm_ffn_post,
               wup, ffn_conv_w[0], ffn_conv_b, wdn)
    return out.reshape(bsz, seq, d)
```

```python
import functools

import jax
import jax.numpy as jnp
import numpy as np
from jax import lax
from jax.experimental import pallas as pl
from jax.experimental.pallas import tpu as pltpu

F32 = jnp.float32
BF16 = jnp.bfloat16

D_MODEL = 1024
GRID_W = 64
CONV_W = 1024
RET_HEADS = 4
RET_DK = 256
RET_DV = 512
RET_QK = RET_HEADS * RET_DK
RET_V = RET_HEADS * RET_DV
ROPE_PAIRS = RET_DK // 4
ROPE_BASE = 10000.0
D_FF = 2816
N_MOD = 6
EPS = 1e-6

OFF_CX = 0
OFF_CB = OFF_CX + CONV_W
OFF_CC = OFF_CB + CONV_W
OFF_Q = OFF_CC + CONV_W
OFF_K = OFF_Q + RET_QK
OFF_V = OFF_K + RET_QK
OFF_G = OFF_V + RET_V
OFF_GA = OFF_G + RET_V
OFF_GB = OFF_GA + D_MODEL
IN_COLS = OFF_GB + D_MODEL

RET_C = 256
TM_IN = RET_C
TS_RET = 512
RET_EPI_BOUNDS = (0, TS_RET // 2, TS_RET)
FFN_COLS = 8
FFN_BLOCKS = ((0, 1024), (1024, 1024), (2048, 768))
FFN_EPI_PIECES = 4
IN_PROLOGUE_STEPS = 1
STAGE_SLOTS = 3
IN_W_CHUNKS = 16
CO_W_CHUNKS = 8
RET_W_CHUNKS = 4
FFN_DN_PASS_STEPS = 16
MOD_ROWS = 8
CTX_FIRST_STEP = 2
BF16_SUBLANES = 16
V7X_VMEM_BYTES = 64 * 1024 * 1024
VMEM_LIMIT = V7X_VMEM_BYTES - 4 * 1024 * 1024


def _const_spec(shape):
    zeros = (0,) * len(shape)
    return pl.BlockSpec(shape, lambda *_: zeros, pipeline_mode=pl.Buffered(1))


def _chunk_spec(shape, n_chunks, axis):
    block = list(shape)
    block[axis] //= n_chunks
    assert block[axis] * n_chunks == shape[axis]

    def index_map(i):
        c = jnp.minimum(i, n_chunks - 1)
        return (c, 0) if axis == 0 else (0, c)

    return pl.BlockSpec(tuple(block), index_map)


def _pass_spec(shape, n_steps, first_step=0):
    rows = shape[0] // n_steps
    assert rows * n_steps == shape[0] and rows % BF16_SUBLANES == 0
    return pl.BlockSpec((rows, shape[1]), lambda i: (jnp.clip(i - first_step, 0, n_steps - 1), 0))


def _stream_cast(src_hbm, dst_ref, stage_ref, sem, n_chunks, axis):
    size = src_hbm.shape[axis] // n_chunks
    assert size * n_chunks == src_hbm.shape[axis]

    def window(c):
        chunk = slice(c * size, (c + 1) * size)
        return (chunk, slice(None)) if axis == 0 else (slice(None), chunk)

    n_slots = stage_ref.shape[0]

    def copy(c):
        slot = c % n_slots
        return pltpu.make_async_copy(src_hbm.at[window(c)], stage_ref.at[slot], sem.at[slot])

    for c in range(min(n_slots - 1, n_chunks)):
        copy(c).start()
    for c in range(n_chunks):
        ahead = c + n_slots - 1
        if ahead < n_chunks:
            copy(ahead).start()
        copy(c).wait()
        dst_ref[window(c)] = stage_ref[c % n_slots].astype(BF16)


def _cast_chunks(step, weights):
    n_steps = max(n for _, _, _, n in weights)

    @pl.when(step < n_steps)
    def _():
        for c in range(n_steps):
            @pl.when(step == c)
            def _():
                for src_ref, dst_ref, axis, n in weights:
                    if c < n:
                        size = src_ref.shape[axis]
                        if axis == 0:
                            dst_ref[c * size:(c + 1) * size, :] = src_ref[...].astype(BF16)
                        else:
                            dst_ref[:, c * size:(c + 1) * size] = src_ref[...].astype(BF16)


def _rms(xf):
    return xf * lax.rsqrt(jnp.mean(xf * xf, axis=-1, keepdims=True) + EPS)


def _sigmoid(x):
    return 1.0 / (1.0 + jnp.exp(-x))


def _dot(a, b):
    return jnp.dot(a, b, preferred_element_type=F32)


def _dot_nt(a, b):
    return lax.dot_general(a, b, (((1,), (1,)), ((), ())), preferred_element_type=F32)


def _dot_tn(a, b):
    return lax.dot_general(a, b, (((0,), (0,)), ((), ())), preferred_element_type=F32)


def _tables_kernel(theta_ref, dmat_ref, rowf_ref, rowb_ref, cdt_ref, ctxw_ref, *, ctx_len):
    th = theta_ref[...]
    lg = -(jnp.maximum(-th, 0.0) + jnp.log1p(jnp.exp(-jnp.abs(th))))
    n_i = lax.broadcasted_iota(jnp.int32, (RET_C, RET_C), 0)
    m_i = lax.broadcasted_iota(jnp.int32, (RET_C, RET_C), 1)
    diff = (n_i - m_i).astype(F32)
    idx = lax.broadcasted_iota(jnp.int32, (RET_C, RET_DK), 0).astype(F32)
    tok = lax.broadcasted_iota(jnp.int32, (RET_DK, RET_C), 1).astype(F32)
    midx = lax.broadcasted_iota(jnp.int32, (ctx_len, RET_DK), 0).astype(F32)
    for h in range(RET_HEADS):
        lf = lg[h:h + 1, :]
        lb = lg[RET_HEADS + h:RET_HEADS + h + 1, :]
        lf_c, lb_c = lf[:, :RET_C], lb[:, :RET_C]
        dmat_ref[h] = jnp.exp(jnp.where(diff >= 0.0, diff * lf_c, -diff * lb_c))
        lf_k, lb_k = lf[:, :RET_DK], lb[:, :RET_DK]
        rowf_ref[h, 0] = jnp.exp((idx + 1.0) * lf_k).astype(BF16)
        rowf_ref[h, 1] = jnp.exp((RET_C - 1.0 - tok) * lf_k).astype(BF16)
        rowb_ref[h, 0] = jnp.exp((RET_C - idx) * lb_k).astype(BF16)
        rowb_ref[h, 1] = jnp.exp(tok * lb_k).astype(BF16)
        cdt_ref[h, 0] = jnp.exp(RET_C * lf)
        cdt_ref[h, 1] = jnp.exp(RET_C * lb)
        ctxw_ref[h, 0] = jnp.exp((ctx_len - 1.0 - midx) * lf_k)
        ctxw_ref[h, 1] = jnp.exp(midx * lb_k)


def _prep_kernel(c_ref, cctx_ref, wada_ref, bada_ref, ctx_ref, gain_ref, w_ref, theta_ref,
                 mod_ref, sfw_ref, sbw_ref, dmat_ref, rowf_ref, rowb_ref, cdt_ref,
                 hc_ref, kc_ref, ctxw_ref):
    j = pl.program_id(0)
    bsz, lc, _ = ctx_ref.shape

    cv = jnp.concatenate([c_ref[...], cctx_ref[...], jnp.zeros((MOD_ROWS - bsz - 1, D_MODEL), F32)], axis=0)
    s = cv * _sigmoid(cv)
    res = _dot(s.astype(BF16), wada_ref[...].astype(BF16)) + bada_ref[...]
    for k in range(N_MOD):
        @pl.when(j == k)
        def _():
            mod_ref[:, k, :] = res

    heads_per_step = RET_QK // RET_DV

    def project():
        return _dot(hc_ref[...], w_ref[...].astype(BF16))

    @pl.when(j == CTX_FIRST_STEP)
    def _():
        _tables_kernel(theta_ref, dmat_ref, rowf_ref, rowb_ref, cdt_ref, ctxw_ref, ctx_len=lc)
        shift = mod_ref[bsz:bsz + 1, 0, :]
        scale = mod_ref[bsz:bsz + 1, 1, :]
        xc = ctx_ref[...].reshape(bsz * lc, D_MODEL)
        hc_ref[...] = (_rms(xc) * gain_ref[...] * (1.0 + scale) + shift).astype(BF16)
        kc_ref[...] = project() * RET_DK ** -0.5

    for step in range(1, RET_HEADS // heads_per_step + 1):
        @pl.when(j == CTX_FIRST_STEP + step)
        def _():
            p = project()
            for hh in range(heads_per_step):
                h = (step - 1) * heads_per_step + hh
                for b in range(bsz):
                    rows = slice(b * lc, (b + 1) * lc)
                    kh = kc_ref[rows, h * RET_DK:(h + 1) * RET_DK]
                    vh = p[rows, hh * RET_DV:(hh + 1) * RET_DV].astype(BF16)
                    kfb = jnp.concatenate([(kh * ctxw_ref[h, 0]).astype(BF16),
                                           (kh * ctxw_ref[h, 1]).astype(BF16)], axis=1)
                    s_both = _dot_tn(kfb, vh)
                    sfw_ref[b, hh] = s_both[:RET_DK]
                    sbw_ref[b, hh] = s_both[RET_DK:]


def _prep(c, c_ctx, w_ada, b_ada, ctx, gain, w_in, theta):
    bsz, lc, _ = ctx.shape
    assert w_ada.shape[1] == N_MOD * D_MODEL
    assert OFF_K % RET_QK == 0 and OFF_V == OFF_K + RET_QK and RET_QK % RET_DV == 0
    heads_per_step = RET_QK // RET_DV
    ctx_steps = 1 + RET_HEADS // heads_per_step
    assert CTX_FIRST_STEP + ctx_steps <= N_MOD
    state = jax.ShapeDtypeStruct((bsz, RET_HEADS, RET_DK, RET_DV), F32)
    sspec = pl.BlockSpec((bsz, heads_per_step, RET_DK, RET_DV),
                         lambda j: (0, jnp.clip(j - CTX_FIRST_STEP - 1, 0, ctx_steps - 2), 0, 0))
    rows = jax.ShapeDtypeStruct((RET_HEADS, 2, RET_C, RET_DK), BF16)
    mod = jax.ShapeDtypeStruct((MOD_ROWS, N_MOD, D_MODEL), F32)
    tables = (jax.ShapeDtypeStruct((RET_HEADS, RET_C, RET_C), F32), rows, rows,
              jax.ShapeDtypeStruct((RET_HEADS, 2, 1, RET_DV), F32))

    def whole(shape):
        return pl.BlockSpec(shape, lambda j: (0,) * len(shape))

    return pl.pallas_call(
        _prep_kernel,
        name="prep",
        grid=(N_MOD,),
        in_specs=[
            whole(c.shape),
            whole((1, D_MODEL)),
            pl.BlockSpec((D_MODEL, D_MODEL), lambda j: (0, j)),
            pl.BlockSpec((1, D_MODEL), lambda j: (0, j)),
            whole(ctx.shape),
            whole((1, D_MODEL)),
            pl.BlockSpec((D_MODEL, RET_QK),
                         lambda j: (0, OFF_K // RET_QK + jnp.clip(j - CTX_FIRST_STEP, 0, ctx_steps - 1))),
            whole(theta.shape),
        ],
        out_specs=(whole(mod.shape), sspec, sspec, *[whole(t.shape) for t in tables]),
        out_shape=(mod, state, state, *tables),
        scratch_shapes=[pltpu.VMEM((bsz * lc, D_MODEL), BF16), pltpu.VMEM((bsz * lc, RET_QK), F32),
                        pltpu.VMEM((RET_HEADS, 2, lc, RET_DK), F32)],
        compiler_params=pltpu.CompilerParams(dimension_semantics=("arbitrary",),
                                             vmem_limit_bytes=VMEM_LIMIT),
    )(c, c_ctx.reshape(1, D_MODEL), w_ada, b_ada, ctx, gain, w_in, theta)


def _inproj_kernel(x_ref, mod_ref, gain_ref, cos_ref, sin_ref, win32_ref, wco32_ref, cw_ref, cb_ref,
                   s0_ref, rowb_ref, cdt_ref,
                   yc_ref, q_ref, kt_ref, v_ref, sg_ref, sgb_ref, ob_ref, t_ref, win_ref, wco_ref,
                   win_stage_ref, wco_stage_ref, win_sem, wco_sem, *, tiles_per_seq):
    step = pl.program_id(0)

    @pl.when(step == 0)
    def _():
        _stream_cast(win32_ref, win_ref, win_stage_ref, win_sem, IN_W_CHUNKS, 0)
        _stream_cast(wco32_ref, wco_ref, wco_stage_ref, wco_sem, CO_W_CHUNKS, 0)

    @pl.when(step >= IN_PROLOGUE_STEPS)
    def _():
        _inproj_tile(x_ref, mod_ref, gain_ref, cos_ref, sin_ref, win_ref, wco_ref, cw_ref, cb_ref,
                     s0_ref, rowb_ref, cdt_ref, yc_ref, q_ref, kt_ref, v_ref, sg_ref, sgb_ref, ob_ref,
                     t_ref, (step - IN_PROLOGUE_STEPS) % tiles_per_seq == 0)


def _inproj_tile(x_ref, mod_ref, gain_ref, cos_ref, sin_ref, win_ref, wco_ref, cw_ref, cb_ref,
                 s0_ref, rowb_ref, cdt_ref,
                 yc_ref, q_ref, kt_ref, v_ref, sg_ref, sgb_ref, ob_ref, t_ref, first_tile):
    @pl.when(first_tile)
    def _():
        t_ref[...] = s0_ref[0]

    m = mod_ref[0]
    h = _rms(x_ref[...]) * gain_ref[...] * (1.0 + m[1:2]) + m[0:1]
    hb = h.astype(BF16)

    def proj(off, width):
        return _dot(hb, win_ref[:, off:off + width])

    u = proj(OFF_CC, CONV_W) * proj(OFF_CX, CONV_W)
    col = lax.broadcasted_iota(jnp.int32, (TM_IN, CONV_W), 0) % GRID_W
    prev = jnp.where(col == 0, 0.0, pltpu.roll(u, 1, 0))
    nxt = jnp.where(col == GRID_W - 1, 0.0, pltpu.roll(u, TM_IN - 1, 0))
    cw = cw_ref[...]
    conv = cw[0:1] * prev + cw[1:2] * u + cw[2:3] * nxt + cb_ref[...]
    z = (proj(OFF_CB, CONV_W) * conv).astype(BF16)
    yc_ref[...] = (_sigmoid(proj(OFF_GA, D_MODEL)) * _dot(z, wco_ref[...])).astype(BF16)
    sgb_ref[...] = _sigmoid(proj(OFF_GB, D_MODEL)).astype(BF16)

    cos = cos_ref[...]
    sin = sin_ref[...]
    half = RET_DK // 2
    for off, ref, scale in ((OFF_Q, q_ref, 1.0), (OFF_K, kt_ref, RET_DK ** -0.5)):
        t = proj(off, RET_QK)
        for hh in range(RET_HEADS):
            lo = hh * RET_DK
            t1 = t[:, lo:lo + half]
            t2 = t[:, lo + half:lo + RET_DK]
            r1 = (t1 * cos - t2 * sin) * scale
            r2 = (t2 * cos + t1 * sin) * scale
            if ref is q_ref:
                ref[:, lo:lo + half] = r1.astype(BF16)
                ref[:, lo + half:lo + RET_DK] = r2.astype(BF16)
            else:
                ref[lo:lo + half, :] = r1.T.astype(BF16)
                ref[lo + half:lo + RET_DK, :] = r2.T.astype(BF16)
    v_ref[...] = proj(OFF_V, RET_V).astype(BF16)
    g = proj(OFF_G, RET_V)
    sg_ref[...] = (g * _sigmoid(g)).astype(BF16)

    for hh in range(RET_HEADS):
        qs = q_ref[:, hh * RET_DK:(hh + 1) * RET_DK] * rowb_ref[hh, 0]
        kst = kt_ref[hh * RET_DK:(hh + 1) * RET_DK, :] * rowb_ref[hh, 1]
        vh = v_ref[:, hh * RET_DV:(hh + 1) * RET_DV]
        t = t_ref[hh]
        ob_ref[:, hh * RET_DV:(hh + 1) * RET_DV] = _dot(qs, t.astype(BF16)).astype(BF16)
        t_ref[hh] = cdt_ref[hh, 1] * t + _dot(kst, vh)


def _inproj(x2, modt, gain, cos, sin, win, wco, cw, cb, s_bw, rowb, cdt, bsz, seq):
    nt = seq // TM_IN

    def tile(i):
        t = jnp.maximum(i - IN_PROLOGUE_STEPS, 0)
        return t // nt, nt - 1 - t % nt

    def row_spec(width):
        return pl.BlockSpec((TM_IN, width), lambda i: (tile(i)[0] * nt + tile(i)[1], 0))

    def out(width):
        return jax.ShapeDtypeStruct((bsz * seq, width), BF16)

    rope = pl.BlockSpec((TM_IN, RET_DK // 2), lambda i: (tile(i)[1], 0))
    return pl.pallas_call(
        functools.partial(_inproj_kernel, tiles_per_seq=nt),
        name="inproj",
        grid=(IN_PROLOGUE_STEPS + bsz * nt,),
        in_specs=[
            row_spec(D_MODEL),
            pl.BlockSpec((1, N_MOD, D_MODEL), lambda i: (tile(i)[0], 0, 0)),
            pl.BlockSpec((1, D_MODEL), lambda i: (0, 0)),
            rope, rope,
            pl.BlockSpec(memory_space=pl.ANY),
            pl.BlockSpec(memory_space=pl.ANY),
            pl.BlockSpec(cw.shape, lambda i: (0, 0)),
            pl.BlockSpec((1, CONV_W), lambda i: (0, 0)),
            pl.BlockSpec((1, RET_HEADS, RET_DK, RET_DV), lambda i: (tile(i)[0], 0, 0, 0)),
            _const_spec(rowb.shape),
            _const_spec(cdt.shape),
        ],
        out_specs=(row_spec(D_MODEL), row_spec(RET_QK),
                   pl.BlockSpec((RET_QK, TM_IN), lambda i: (0, tile(i)[0] * nt + tile(i)[1])),
                   row_spec(RET_V), row_spec(RET_V), row_spec(D_MODEL), row_spec(RET_V)),
        out_shape=(out(D_MODEL), out(RET_QK), jax.ShapeDtypeStruct((RET_QK, bsz * seq), BF16),
                   out(RET_V), out(RET_V), out(D_MODEL), out(RET_V)),
        scratch_shapes=[pltpu.VMEM((RET_HEADS, RET_DK, RET_DV), F32),
                        pltpu.VMEM(win.shape, BF16), pltpu.VMEM(wco.shape, BF16),
                        pltpu.VMEM((STAGE_SLOTS, win.shape[0] // IN_W_CHUNKS, win.shape[1]), F32),
                        pltpu.VMEM((STAGE_SLOTS, wco.shape[0] // CO_W_CHUNKS, wco.shape[1]), F32),
                        pltpu.SemaphoreType.DMA((STAGE_SLOTS,)), pltpu.SemaphoreType.DMA((STAGE_SLOTS,))],
        compiler_params=pltpu.CompilerParams(dimension_semantics=("arbitrary",),
                                             vmem_limit_bytes=VMEM_LIMIT),
    )(x2, modt, gain, cos, sin, win, wco, cw, cb, s_bw, rowb, cdt)


def _ret_fw_kernel(q_ref, kt_ref, v_ref, sg_ref, ob_ref, yc_ref, sgb_ref, x_ref, s0_ref, dmat_ref,
                   rowf_ref, cdt_ref, wro32_ref, wo32_ref, mod_ref, gpost_ref, wup32_ref, wdn32_ref,
                   x1_ref, wup_out_ref, wdn_out_ref, s_ref, gat_ref, wro_ref, wo_ref, *, steps_per_seq):
    step = pl.program_id(0)
    _cast_chunks(step, ((wro32_ref, wro_ref, 0, RET_W_CHUNKS), (wo32_ref, wo_ref, 0, RET_W_CHUNKS)))

    @pl.when(step >= RET_W_CHUNKS)
    def _():
        wup_out_ref[...] = wup32_ref[...].astype(BF16)
        wdn_out_ref[...] = wdn32_ref[...].astype(BF16)
        _ret_fw_tile(q_ref, kt_ref, v_ref, sg_ref, ob_ref, yc_ref, sgb_ref, x_ref, s0_ref, dmat_ref,
                     rowf_ref, cdt_ref, wro_ref, wo_ref, mod_ref, gpost_ref, x1_ref, s_ref, gat_ref,
                     (step - RET_W_CHUNKS) % steps_per_seq == 0)


def _ret_fw_tile(q_ref, kt_ref, v_ref, sg_ref, ob_ref, yc_ref, sgb_ref, x_ref, s0_ref, dmat_ref,
                 rowf_ref, cdt_ref, wro_ref, wo_ref, mod_ref, gpost_ref,
                 x1_ref, s_ref, gat_ref, first_tile):
    @pl.when(first_tile)
    def _():
        s_ref[...] = s0_ref[0]

    heads = range(RET_HEADS)
    for ci in range(TS_RET // RET_C):
        rows = slice(ci * RET_C, (ci + 1) * RET_C)
        qb = [q_ref[rows, h * RET_DK:(h + 1) * RET_DK] for h in heads]
        kt = [kt_ref[h * RET_DK:(h + 1) * RET_DK, rows] for h in heads]
        vh = [v_ref[rows, h * RET_DV:(h + 1) * RET_DV] for h in heads]
        scores = [(_dot(qb[h], kt[h]) * dmat_ref[h]).astype(BF16) for h in heads]
        s = [s_ref[h] for h in heads]
        o = [(_dot(scores[h], vh[h]) + _dot(qb[h] * rowf_ref[h, 0], s[h].astype(BF16))
              + ob_ref[rows, h * RET_DV:(h + 1) * RET_DV].astype(F32)) for h in heads]
        for h in heads:
            s_ref[h] = cdt_ref[h, 0] * s[h] + _dot(kt[h] * rowf_ref[h, 1], vh[h])
        for h in heads:
            gate = sg_ref[rows, h * RET_DV:(h + 1) * RET_DV].astype(F32)
            gat_ref[rows, h * RET_DV:(h + 1) * RET_DV] = (gate * _rms(o[h])).astype(BF16)

    for p in range(len(RET_EPI_BOUNDS) - 1):
        rows = slice(RET_EPI_BOUNDS[p], RET_EPI_BOUNDS[p + 1])
        y_ret = _dot(gat_ref[rows, :], wro_ref[...])
        merged = yc_ref[rows, :].astype(F32) + sgb_ref[rows, :].astype(F32) * y_ret
        y = _dot(merged.astype(BF16), wo_ref[...])
        x1_ref[rows, :] = x_ref[rows, :] + mod_ref[0][2:3] * (_rms(y) * gpost_ref[...])


def _ret_fw(q, k, v, sg, ob, yc, sgb, x2, s_fw, dmat, rowf, cdt, wro, wo, modt, gpost, wup, wdn,
            bsz, seq):
    ns = seq // TS_RET
    passed = [_pass_spec(wup.shape, bsz * ns, RET_W_CHUNKS),
              _pass_spec(wdn.shape, FFN_DN_PASS_STEPS, RET_W_CHUNKS)]

    def tile(i):
        return jnp.maximum(i - RET_W_CHUNKS, 0)

    def row_spec(width):
        return pl.BlockSpec((TS_RET, width), lambda i: (tile(i), 0))

    return pl.pallas_call(
        functools.partial(_ret_fw_kernel, steps_per_seq=ns),
        name="ret_fw",
        grid=(RET_W_CHUNKS + bsz * ns,),
        in_specs=[
            row_spec(RET_QK), pl.BlockSpec((RET_QK, TS_RET), lambda i: (0, tile(i))),
            row_spec(RET_V), row_spec(RET_V), row_spec(RET_V),
            row_spec(D_MODEL), row_spec(D_MODEL), row_spec(D_MODEL),
            pl.BlockSpec((1, RET_HEADS, RET_DK, RET_DV), lambda i: (tile(i) // ns, 0, 0, 0)),
            _const_spec(dmat.shape), _const_spec(rowf.shape), _const_spec(cdt.shape),
            _chunk_spec(wro.shape, RET_W_CHUNKS, 0), _chunk_spec(wo.shape, RET_W_CHUNKS, 0),
            pl.BlockSpec((1, N_MOD, D_MODEL), lambda i: (tile(i) // ns, 0, 0)),
            pl.BlockSpec((1, D_MODEL), lambda i: (0, 0)),
            *passed,
        ],
        out_specs=(row_spec(D_MODEL), *passed),
        out_shape=(jax.ShapeDtypeStruct((bsz * seq, D_MODEL), F32),
                   jax.ShapeDtypeStruct(wup.shape, BF16), jax.ShapeDtypeStruct(wdn.shape, BF16)),
        scratch_shapes=[pltpu.VMEM((RET_HEADS, RET_DK, RET_DV), F32),
                        pltpu.VMEM((TS_RET, RET_V), BF16),
                        pltpu.VMEM(wro.shape, BF16), pltpu.VMEM(wo.shape, BF16)],
        compiler_params=pltpu.CompilerParams(dimension_semantics=("arbitrary",),
                                             vmem_limit_bytes=VMEM_LIMIT),
    )(q, k, v, sg, ob, yc, sgb, x2, s_fw, dmat, rowf, cdt, wro, wo, modt, gpost, wup, wdn)


def _ffn_kernel(x1_ref, mod_ref, gpre_ref, gpost_ref, wup_ref, cw_ref, cb_ref, wdn_ref,
                out_ref, acc_ref, *, grid_rows):
    tm = grid_rows * FFN_COLS
    x1 = x1_ref[0].reshape(tm, D_MODEL)
    m = mod_ref[0]
    hb = (_rms(x1) * gpre_ref[...] * (1.0 + m[4:5]) + m[3:4]).astype(BF16)

    def conv(u, off, width):
        w = cw_ref[:, off:off + width]
        pad = jnp.zeros((FFN_COLS, width), F32)
        above = jnp.concatenate([pad, u[:tm - FFN_COLS]], axis=0)
        below = jnp.concatenate([u[FFN_COLS:], pad], axis=0)
        return w[0:1] * above + w[1:2] * u + w[2:3] * below + cb_ref[:, off:off + width]

    def up(c0, width):
        return (_dot(hb, wup_ref[:, c0:c0 + width]),
                _dot(hb, wup_ref[:, D_FF + c0:D_FF + c0 + width]))

    ahead = up(*FFN_BLOCKS[0])
    for j, (c0, width) in enumerate(FFN_BLOCKS):
        ug, uv = ahead
        if j + 1 < len(FFN_BLOCKS):
            ahead = up(*FFN_BLOCKS[j + 1])
        g = conv(ug, c0, width)
        val = conv(uv, D_FF + c0, width)
        a = (g * _sigmoid(g) * val).astype(BF16)
        if j == 0:
            acc_ref[...] = _dot(a, wdn_ref[c0:c0 + width, :])
        elif j + 1 < len(FFN_BLOCKS):
            acc_ref[...] += _dot(a, wdn_ref[c0:c0 + width, :])
        else:
            gr = grid_rows // FFN_EPI_PIECES
            for p in range(FFN_EPI_PIECES):
                rows = slice(p * gr * FFN_COLS, (p + 1) * gr * FFN_COLS)
                f = acc_ref[rows, :] + _dot(a[rows, :], wdn_ref[c0:c0 + width, :])
                res = x1[rows, :] + m[5:6] * (_rms(f) * gpost_ref[...])
                out_ref[0, p * gr:(p + 1) * gr] = res.reshape(gr, FFN_COLS, D_MODEL)


def _ffn(x1g, modt, gpre, gpost, wup, cw, cb, wdn):
    bsz, grid_rows, grid_w, _ = x1g.shape
    strips = grid_w // FFN_COLS
    strip = pl.BlockSpec((1, grid_rows, FFN_COLS, D_MODEL), lambda i: (i // strips, 0, i % strips, 0))
    vec = pl.BlockSpec((1, D_MODEL), lambda i: (0, 0))
    return pl.pallas_call(
        functools.partial(_ffn_kernel, grid_rows=grid_rows),
        name="ffn",
        grid=(bsz * strips,),
        in_specs=[
            strip,
            pl.BlockSpec((1, N_MOD, D_MODEL), lambda i: (i // strips, 0, 0)),
            vec, vec,
            _const_spec(wup.shape),
            pl.BlockSpec(cw.shape, lambda i: (0, 0)),
            pl.BlockSpec((1, 2 * D_FF), lambda i: (0, 0)),
            _const_spec(wdn.shape),
        ],
        out_specs=strip,
        out_shape=jax.ShapeDtypeStruct(x1g.shape, F32),
        scratch_shapes=[pltpu.VMEM((grid_rows * FFN_COLS, D_MODEL), F32)],
        compiler_params=pltpu.CompilerParams(dimension_semantics=("arbitrary",),
                                             vmem_limit_bytes=VMEM_LIMIT),
    )(x1g, modt, gpre, gpost, wup, cw, cb, wdn)


def _rotary_tables(n):
    t = np.arange(n)
    row = (t // GRID_W).astype(np.float64)
    col = (t % GRID_W).astype(np.float64)
    inv = ROPE_BASE ** (-np.arange(ROPE_PAIRS, dtype=np.float64) / ROPE_PAIRS)
    ang = np.concatenate([row[:, None] * inv, col[:, None] * inv], axis=-1)
    return jnp.asarray(np.cos(ang), F32), jnp.asarray(np.sin(ang), F32)


def kernel(x, c, ctx, c_ctx, w_ada, b_ada, norm_mix_pre, norm_mix_post, w_in, conv_w, conv_b,
           w_conv_out, ret_decay_fw, ret_decay_bw, w_ret_out, w_o, norm_ffn_pre, norm_ffn_post,
           w_ffn_up, ffn_conv_w, ffn_conv_b, w_ffn_down):
    bsz, seq, d = x.shape
    depth = w_ada.shape[0]
    assert d == D_MODEL and depth == 1 and bsz + 1 <= MOD_ROWS
    assert seq % TS_RET == 0 and seq % TM_IN == 0 and TM_IN % GRID_W == 0 and GRID_W % FFN_COLS == 0
    lc = ctx.shape[1]
    cos, sin = _rotary_tables(seq)
    x2 = x.reshape(bsz * seq, d)


    theta = jnp.concatenate([ret_decay_fw[0], ret_decay_bw[0]]).astype(F32)
    theta = jnp.broadcast_to(theta[:, None], (2 * RET_HEADS, RET_DV))
    modt, s_fw, s_bw, dmat, rowf, rowb, cdt = _prep(c, c_ctx, w_ada[0], b_ada, ctx, norm_mix_pre,
                                                    w_in[0], theta)

    yc, q, k, v, sg, sgb, ob = _inproj(x2, modt, norm_mix_pre, cos, sin, w_in[0], w_conv_out[0],
                                       conv_w[0], conv_b, s_bw, rowb, cdt, bsz, seq)
    x1, wup, wdn = _ret_fw(q, k, v, sg, ob, yc, sgb, x2, s_fw, dmat, rowf, cdt, w_ret_out[0], w_o[0],
                           modt, norm_mix_post, w_ffn_up[0], w_ffn_down[0], bsz, seq)
    out = _ffn(x1.reshape(bsz, seq // GRID_W, GRID_W, d), modt, norm_ffn_pre, norm_ffn_post,
               wup, ffn_conv_w[0], ffn_conv_b, wdn)
    return out.reshape(bsz, seq, d)
```

```python
import functools

import jax
import jax.numpy as jnp
import numpy as np
from jax import lax
from jax.experimental import pallas as pl
from jax.experimental.pallas import tpu as pltpu

F32 = jnp.float32
BF16 = jnp.bfloat16

D_MODEL = 1024
GRID_W = 64
CONV_W = 1024
RET_HEADS = 4
RET_DK = 256
RET_DV = 512
RET_QK = RET_HEADS * RET_DK
RET_V = RET_HEADS * RET_DV
ROPE_PAIRS = RET_DK // 4
ROPE_BASE = 10000.0
D_FF = 2816
N_MOD = 6
EPS = 1e-6

OFF_CX = 0
OFF_CB = OFF_CX + CONV_W
OFF_CC = OFF_CB + CONV_W
OFF_Q = OFF_CC + CONV_W
OFF_K = OFF_Q + RET_QK
OFF_V = OFF_K + RET_QK
OFF_G = OFF_V + RET_V
OFF_GA = OFF_G + RET_V
OFF_GB = OFF_GA + D_MODEL
IN_COLS = OFF_GB + D_MODEL

RET_C = 256
TM_IN = RET_C
TS_RET = 512
RET_EPI_BOUNDS = (0, TS_RET // 2, TS_RET)
FFN_COLS = 8
FFN_BLOCKS = ((0, 1024), (1024, 1024), (2048, 768))
FFN_EPI_PIECES = 4
IN_PROLOGUE_STEPS = 1
STAGE_SLOTS = 4
IN_W_CHUNKS = 32
CO_W_CHUNKS = 8
RET_W_CHUNKS = 4
FFN_DN_PASS_STEPS = 16
MOD_ROWS = 8
CTX_FIRST_STEP = 2
BF16_SUBLANES = 16
V7X_VMEM_BYTES = 64 * 1024 * 1024
VMEM_LIMIT = V7X_VMEM_BYTES - 4 * 1024 * 1024


def _const_spec(shape):
    zeros = (0,) * len(shape)
    return pl.BlockSpec(shape, lambda *_: zeros, pipeline_mode=pl.Buffered(1))


def _chunk_spec(shape, n_chunks, axis):
    block = list(shape)
    block[axis] //= n_chunks
    assert block[axis] * n_chunks == shape[axis]

    def index_map(i):
        c = jnp.minimum(i, n_chunks - 1)
        return (c, 0) if axis == 0 else (0, c)

    return pl.BlockSpec(tuple(block), index_map)


def _pass_spec(shape, n_steps, first_step=0):
    rows = shape[0] // n_steps
    assert rows * n_steps == shape[0] and rows % BF16_SUBLANES == 0
    return pl.BlockSpec((rows, shape[1]), lambda i: (jnp.clip(i - first_step, 0, n_steps - 1), 0))


def _stream_cast(src_hbm, dst_ref, stage_ref, sem, n_chunks, axis):
    size = src_hbm.shape[axis] // n_chunks
    assert size * n_chunks == src_hbm.shape[axis]

    def window(c):
        chunk = slice(c * size, (c + 1) * size)
        return (chunk, slice(None)) if axis == 0 else (slice(None), chunk)

    n_slots = stage_ref.shape[0]

    def copy(c):
        slot = c % n_slots
        return pltpu.make_async_copy(src_hbm.at[window(c)], stage_ref.at[slot], sem.at[slot])

    for c in range(min(n_slots - 1, n_chunks)):
        copy(c).start()
    for c in range(n_chunks):
        ahead = c + n_slots - 1
        if ahead < n_chunks:
            copy(ahead).start()
        copy(c).wait()
        dst_ref[window(c)] = stage_ref[c % n_slots].astype(BF16)


def _cast_chunks(step, weights):
    n_steps = max(n for _, _, _, n in weights)

    @pl.when(step < n_steps)
    def _():
        for c in range(n_steps):
            @pl.when(step == c)
            def _():
                for src_ref, dst_ref, axis, n in weights:
                    if c < n:
                        size = src_ref.shape[axis]
                        if axis == 0:
                            dst_ref[c * size:(c + 1) * size, :] = src_ref[...].astype(BF16)
                        else:
                            dst_ref[:, c * size:(c + 1) * size] = src_ref[...].astype(BF16)


def _rms(xf):
    return xf * lax.rsqrt(jnp.mean(xf * xf, axis=-1, keepdims=True) + EPS)


def _sigmoid(x):
    return 1.0 / (1.0 + jnp.exp(-x))


def _dot(a, b):
    return jnp.dot(a, b, preferred_element_type=F32)


def _dot_nt(a, b):
    return lax.dot_general(a, b, (((1,), (1,)), ((), ())), preferred_element_type=F32)


def _dot_tn(a, b):
    return lax.dot_general(a, b, (((0,), (0,)), ((), ())), preferred_element_type=F32)


def _tables_kernel(theta_ref, dmat_ref, rowf_ref, rowb_ref, cdt_ref, ctxw_ref, *, ctx_len):
    th = theta_ref[...]
    lg = -(jnp.maximum(-th, 0.0) + jnp.log1p(jnp.exp(-jnp.abs(th))))
    n_i = lax.broadcasted_iota(jnp.int32, (RET_C, RET_C), 0)
    m_i = lax.broadcasted_iota(jnp.int32, (RET_C, RET_C), 1)
    diff = (n_i - m_i).astype(F32)
    idx = lax.broadcasted_iota(jnp.int32, (RET_C, RET_DK), 0).astype(F32)
    tok = lax.broadcasted_iota(jnp.int32, (RET_DK, RET_C), 1).astype(F32)
    midx = lax.broadcasted_iota(jnp.int32, (ctx_len, RET_DK), 0).astype(F32)
    for h in range(RET_HEADS):
        lf = lg[h:h + 1, :]
        lb = lg[RET_HEADS + h:RET_HEADS + h + 1, :]
        lf_c, lb_c = lf[:, :RET_C], lb[:, :RET_C]
        dmat_ref[h] = jnp.exp(jnp.where(diff >= 0.0, diff * lf_c, -diff * lb_c))
        lf_k, lb_k = lf[:, :RET_DK], lb[:, :RET_DK]
        rowf_ref[h, 0] = jnp.exp((idx + 1.0) * lf_k).astype(BF16)
        rowf_ref[h, 1] = jnp.exp((RET_C - 1.0 - tok) * lf_k).astype(BF16)
        rowb_ref[h, 0] = jnp.exp((RET_C - idx) * lb_k).astype(BF16)
        rowb_ref[h, 1] = jnp.exp(tok * lb_k).astype(BF16)
        cdt_ref[h, 0] = jnp.exp(RET_C * lf)
        cdt_ref[h, 1] = jnp.exp(RET_C * lb)
        ctxw_ref[h, 0] = jnp.exp((ctx_len - 1.0 - midx) * lf_k)
        ctxw_ref[h, 1] = jnp.exp(midx * lb_k)


def _prep_kernel(c_ref, cctx_ref, wada_ref, bada_ref, ctx_ref, gain_ref, w_ref, theta_ref,
                 mod_ref, sfw_ref, sbw_ref, dmat_ref, rowf_ref, rowb_ref, cdt_ref,
                 hc_ref, kc_ref, ctxw_ref):
    j = pl.program_id(0)
    bsz, lc, _ = ctx_ref.shape

    cv = jnp.concatenate([c_ref[...], cctx_ref[...], jnp.zeros((MOD_ROWS - bsz - 1, D_MODEL), F32)], axis=0)
    s = cv * _sigmoid(cv)
    res = _dot(s.astype(BF16), wada_ref[...].astype(BF16)) + bada_ref[...]
    for k in range(N_MOD):
        @pl.when(j == k)
        def _():
            mod_ref[:, k, :] = res

    heads_per_step = RET_QK // RET_DV

    def project():
        return _dot(hc_ref[...], w_ref[...].astype(BF16))

    @pl.when(j == CTX_FIRST_STEP)
    def _():
        _tables_kernel(theta_ref, dmat_ref, rowf_ref, rowb_ref, cdt_ref, ctxw_ref, ctx_len=lc)
        shift = mod_ref[bsz:bsz + 1, 0, :]
        scale = mod_ref[bsz:bsz + 1, 1, :]
        xc = ctx_ref[...].reshape(bsz * lc, D_MODEL)
        hc_ref[...] = (_rms(xc) * gain_ref[...] * (1.0 + scale) + shift).astype(BF16)
        kc_ref[...] = project() * RET_DK ** -0.5

    for step in range(1, RET_HEADS // heads_per_step + 1):
        @pl.when(j == CTX_FIRST_STEP + step)
        def _():
            p = project()
            for hh in range(heads_per_step):
                h = (step - 1) * heads_per_step + hh
                for b in range(bsz):
                    rows = slice(b * lc, (b + 1) * lc)
                    kh = kc_ref[rows, h * RET_DK:(h + 1) * RET_DK]
                    vh = p[rows, hh * RET_DV:(hh + 1) * RET_DV].astype(BF16)
                    kfb = jnp.concatenate([(kh * ctxw_ref[h, 0]).astype(BF16),
                                           (kh * ctxw_ref[h, 1]).astype(BF16)], axis=1)
                    s_both = _dot_tn(kfb, vh)
                    sfw_ref[b, hh] = s_both[:RET_DK]
                    sbw_ref[b, hh] = s_both[RET_DK:]


def _prep(c, c_ctx, w_ada, b_ada, ctx, gain, w_in, theta):
    bsz, lc, _ = ctx.shape
    assert w_ada.shape[1] == N_MOD * D_MODEL
    assert OFF_K % RET_QK == 0 and OFF_V == OFF_K + RET_QK and RET_QK % RET_DV == 0
    heads_per_step = RET_QK // RET_DV
    ctx_steps = 1 + RET_HEADS // heads_per_step
    assert CTX_FIRST_STEP + ctx_steps <= N_MOD
    state = jax.ShapeDtypeStruct((bsz, RET_HEADS, RET_DK, RET_DV), F32)
    sspec = pl.BlockSpec((bsz, heads_per_step, RET_DK, RET_DV),
                         lambda j: (0, jnp.clip(j - CTX_FIRST_STEP - 1, 0, ctx_steps - 2), 0, 0))
    rows = jax.ShapeDtypeStruct((RET_HEADS, 2, RET_C, RET_DK), BF16)
    mod = jax.ShapeDtypeStruct((MOD_ROWS, N_MOD, D_MODEL), F32)
    tables = (jax.ShapeDtypeStruct((RET_HEADS, RET_C, RET_C), F32), rows, rows,
              jax.ShapeDtypeStruct((RET_HEADS, 2, 1, RET_DV), F32))

    def whole(shape):
        return pl.BlockSpec(shape, lambda j: (0,) * len(shape))

    return pl.pallas_call(
        _prep_kernel,
        name="prep",
        grid=(N_MOD,),
        in_specs=[
            whole(c.shape),
            whole((1, D_MODEL)),
            pl.BlockSpec((D_MODEL, D_MODEL), lambda j: (0, j)),
            pl.BlockSpec((1, D_MODEL), lambda j: (0, j)),
            whole(ctx.shape),
            whole((1, D_MODEL)),
            pl.BlockSpec((D_MODEL, RET_QK),
                         lambda j: (0, OFF_K // RET_QK + jnp.clip(j - CTX_FIRST_STEP, 0, ctx_steps - 1))),
            whole(theta.shape),
        ],
        out_specs=(whole(mod.shape), sspec, sspec, *[whole(t.shape) for t in tables]),
        out_shape=(mod, state, state, *tables),
        scratch_shapes=[pltpu.VMEM((bsz * lc, D_MODEL), BF16), pltpu.VMEM((bsz * lc, RET_QK), F32),
                        pltpu.VMEM((RET_HEADS, 2, lc, RET_DK), F32)],
        compiler_params=pltpu.CompilerParams(dimension_semantics=("arbitrary",),
                                             vmem_limit_bytes=VMEM_LIMIT),
    )(c, c_ctx.reshape(1, D_MODEL), w_ada, b_ada, ctx, gain, w_in, theta)


def _inproj_kernel(x_ref, mod_ref, gain_ref, cos_ref, sin_ref, win32_ref, wco32_ref, cw_ref, cb_ref,
                   s0_ref, rowb_ref, cdt_ref,
                   yc_ref, q_ref, kt_ref, v_ref, sg_ref, sgb_ref, ob_ref, t_ref, win_ref, wco_ref,
                   win_stage_ref, wco_stage_ref, win_sem, wco_sem, *, tiles_per_seq):
    step = pl.program_id(0)

    @pl.when(step == 0)
    def _():
        _stream_cast(win32_ref, win_ref, win_stage_ref, win_sem, IN_W_CHUNKS, 0)
        _stream_cast(wco32_ref, wco_ref, wco_stage_ref, wco_sem, CO_W_CHUNKS, 0)

    @pl.when(step >= IN_PROLOGUE_STEPS)
    def _():
        _inproj_tile(x_ref, mod_ref, gain_ref, cos_ref, sin_ref, win_ref, wco_ref, cw_ref, cb_ref,
                     s0_ref, rowb_ref, cdt_ref, yc_ref, q_ref, kt_ref, v_ref, sg_ref, sgb_ref, ob_ref,
                     t_ref, (step - IN_PROLOGUE_STEPS) % tiles_per_seq == 0)


def _inproj_tile(x_ref, mod_ref, gain_ref, cos_ref, sin_ref, win_ref, wco_ref, cw_ref, cb_ref,
                 s0_ref, rowb_ref, cdt_ref,
                 yc_ref, q_ref, kt_ref, v_ref, sg_ref, sgb_ref, ob_ref, t_ref, first_tile):
    @pl.when(first_tile)
    def _():
        t_ref[...] = s0_ref[0]

    m = mod_ref[0]
    h = _rms(x_ref[...]) * gain_ref[...] * (1.0 + m[1:2]) + m[0:1]
    hb = h.astype(BF16)

    def proj(off, width):
        return _dot(hb, win_ref[:, off:off + width])

    u = proj(OFF_CC, CONV_W) * proj(OFF_CX, CONV_W)
    col = lax.broadcasted_iota(jnp.int32, (TM_IN, CONV_W), 0) % GRID_W
    prev = jnp.where(col == 0, 0.0, pltpu.roll(u, 1, 0))
    nxt = jnp.where(col == GRID_W - 1, 0.0, pltpu.roll(u, TM_IN - 1, 0))
    cw = cw_ref[...]
    conv = cw[0:1] * prev + cw[1:2] * u + cw[2:3] * nxt + cb_ref[...]
    z = (proj(OFF_CB, CONV_W) * conv).astype(BF16)
    yc_ref[...] = (_sigmoid(proj(OFF_GA, D_MODEL)) * _dot(z, wco_ref[...])).astype(BF16)
    sgb_ref[...] = _sigmoid(proj(OFF_GB, D_MODEL)).astype(BF16)

    cos = cos_ref[...]
    sin = sin_ref[...]
    half = RET_DK // 2
    for off, ref, scale in ((OFF_Q, q_ref, 1.0), (OFF_K, kt_ref, RET_DK ** -0.5)):
        t = proj(off, RET_QK)
        for hh in range(RET_HEADS):
            lo = hh * RET_DK
            t1 = t[:, lo:lo + half]
            t2 = t[:, lo + half:lo + RET_DK]
            r1 = (t1 * cos - t2 * sin) * scale
            r2 = (t2 * cos + t1 * sin) * scale
            if ref is q_ref:
                ref[:, lo:lo + half] = r1.astype(BF16)
                ref[:, lo + half:lo + RET_DK] = r2.astype(BF16)
            else:
                ref[lo:lo + half, :] = r1.T.astype(BF16)
                ref[lo + half:lo + RET_DK, :] = r2.T.astype(BF16)
    v_ref[...] = proj(OFF_V, RET_V).astype(BF16)
    g = proj(OFF_G, RET_V)
    sg_ref[...] = (g * _sigmoid(g)).astype(BF16)

    for hh in range(RET_HEADS):
        qs = q_ref[:, hh * RET_DK:(hh + 1) * RET_DK] * rowb_ref[hh, 0]
        kst = kt_ref[hh * RET_DK:(hh + 1) * RET_DK, :] * rowb_ref[hh, 1]
        vh = v_ref[:, hh * RET_DV:(hh + 1) * RET_DV]
        t = t_ref[hh]
        ob_ref[:, hh * RET_DV:(hh + 1) * RET_DV] = _dot(qs, t.astype(BF16)).astype(BF16)
        t_ref[hh] = cdt_ref[hh, 1] * t + _dot(kst, vh)


def _inproj(x2, modt, gain, cos, sin, win, wco, cw, cb, s_bw, rowb, cdt, bsz, seq):
    nt = seq // TM_IN

    def tile(i):
        t = jnp.maximum(i - IN_PROLOGUE_STEPS, 0)
        return t // nt, nt - 1 - t % nt

    def row_spec(width):
        return pl.BlockSpec((TM_IN, width), lambda i: (tile(i)[0] * nt + tile(i)[1], 0))

    def out(width):
        return jax.ShapeDtypeStruct((bsz * seq, width), BF16)

    rope = pl.BlockSpec((TM_IN, RET_DK // 2), lambda i: (tile(i)[1], 0))
    return pl.pallas_call(
        functools.partial(_inproj_kernel, tiles_per_seq=nt),
        name="inproj",
        grid=(IN_PROLOGUE_STEPS + bsz * nt,),
        in_specs=[
            row_spec(D_MODEL),
            pl.BlockSpec((1, N_MOD, D_MODEL), lambda i: (tile(i)[0], 0, 0)),
            pl.BlockSpec((1, D_MODEL), lambda i: (0, 0)),
            rope, rope,
            pl.BlockSpec(memory_space=pl.ANY),
            pl.BlockSpec(memory_space=pl.ANY),
            pl.BlockSpec(cw.shape, lambda i: (0, 0)),
            pl.BlockSpec((1, CONV_W), lambda i: (0, 0)),
            pl.BlockSpec((1, RET_HEADS, RET_DK, RET_DV), lambda i: (tile(i)[0], 0, 0, 0)),
            _const_spec(rowb.shape),
            _const_spec(cdt.shape),
        ],
        out_specs=(row_spec(D_MODEL), row_spec(RET_QK),
                   pl.BlockSpec((RET_QK, TM_IN), lambda i: (0, tile(i)[0] * nt + tile(i)[1])),
                   row_spec(RET_V), row_spec(RET_V), row_spec(D_MODEL), row_spec(RET_V)),
        out_shape=(out(D_MODEL), out(RET_QK), jax.ShapeDtypeStruct((RET_QK, bsz * seq), BF16),
                   out(RET_V), out(RET_V), out(D_MODEL), out(RET_V)),
        scratch_shapes=[pltpu.VMEM((RET_HEADS, RET_DK, RET_DV), F32),
                        pltpu.VMEM(win.shape, BF16), pltpu.VMEM(wco.shape, BF16),
                        pltpu.VMEM((STAGE_SLOTS, win.shape[0] // IN_W_CHUNKS, win.shape[1]), F32),
                        pltpu.VMEM((STAGE_SLOTS, wco.shape[0] // CO_W_CHUNKS, wco.shape[1]), F32),
                        pltpu.SemaphoreType.DMA((STAGE_SLOTS,)), pltpu.SemaphoreType.DMA((STAGE_SLOTS,))],
        compiler_params=pltpu.CompilerParams(dimension_semantics=("arbitrary",),
                                             vmem_limit_bytes=VMEM_LIMIT),
    )(x2, modt, gain, cos, sin, win, wco, cw, cb, s_bw, rowb, cdt)


def _ret_fw_kernel(q_ref, kt_ref, v_ref, sg_ref, ob_ref, yc_ref, sgb_ref, x_ref, s0_ref, dmat_ref,
                   rowf_ref, cdt_ref, wro32_ref, wo32_ref, mod_ref, gpost_ref, wup32_ref, wdn32_ref,
                   x1_ref, wup_out_ref, wdn_out_ref, s_ref, gat_ref, wro_ref, wo_ref, *, steps_per_seq):
    step = pl.program_id(0)
    _cast_chunks(step, ((wro32_ref, wro_ref, 0, RET_W_CHUNKS), (wo32_ref, wo_ref, 0, RET_W_CHUNKS)))

    @pl.when(step >= RET_W_CHUNKS)
    def _():
        wup_out_ref[...] = wup32_ref[...].astype(BF16)
        wdn_out_ref[...] = wdn32_ref[...].astype(BF16)
        _ret_fw_tile(q_ref, kt_ref, v_ref, sg_ref, ob_ref, yc_ref, sgb_ref, x_ref, s0_ref, dmat_ref,
                     rowf_ref, cdt_ref, wro_ref, wo_ref, mod_ref, gpost_ref, x1_ref, s_ref, gat_ref,
                     (step - RET_W_CHUNKS) % steps_per_seq == 0)


def _ret_fw_tile(q_ref, kt_ref, v_ref, sg_ref, ob_ref, yc_ref, sgb_ref, x_ref, s0_ref, dmat_ref,
                 rowf_ref, cdt_ref, wro_ref, wo_ref, mod_ref, gpost_ref,
                 x1_ref, s_ref, gat_ref, first_tile):
    @pl.when(first_tile)
    def _():
        s_ref[...] = s0_ref[0]

    heads = range(RET_HEADS)
    for ci in range(TS_RET // RET_C):
        rows = slice(ci * RET_C, (ci + 1) * RET_C)
        qb = [q_ref[rows, h * RET_DK:(h + 1) * RET_DK] for h in heads]
        kt = [kt_ref[h * RET_DK:(h + 1) * RET_DK, rows] for h in heads]
        vh = [v_ref[rows, h * RET_DV:(h + 1) * RET_DV] for h in heads]
        scores = [(_dot(qb[h], kt[h]) * dmat_ref[h]).astype(BF16) for h in heads]
        s = [s_ref[h] for h in heads]
        o = [(_dot(scores[h], vh[h]) + _dot(qb[h] * rowf_ref[h, 0], s[h].astype(BF16))
              + ob_ref[rows, h * RET_DV:(h + 1) * RET_DV].astype(F32)) for h in heads]
        for h in heads:
            s_ref[h] = cdt_ref[h, 0] * s[h] + _dot(kt[h] * rowf_ref[h, 1], vh[h])
        for h in heads:
            gate = sg_ref[rows, h * RET_DV:(h + 1) * RET_DV].astype(F32)
            gat_ref[rows, h * RET_DV:(h + 1) * RET_DV] = (gate * _rms(o[h])).astype(BF16)

    for p in range(len(RET_EPI_BOUNDS) - 1):
        rows = slice(RET_EPI_BOUNDS[p], RET_EPI_BOUNDS[p + 1])
        y_ret = _dot(gat_ref[rows, :], wro_ref[...])
        merged = yc_ref[rows, :].astype(F32) + sgb_ref[rows, :].astype(F32) * y_ret
        y = _dot(merged.astype(BF16), wo_ref[...])
        x1_ref[rows, :] = x_ref[rows, :] + mod_ref[0][2:3] * (_rms(y) * gpost_ref[...])


def _ret_fw(q, k, v, sg, ob, yc, sgb, x2, s_fw, dmat, rowf, cdt, wro, wo, modt, gpost, wup, wdn,
            bsz, seq):
    ns = seq // TS_RET
    passed = [_pass_spec(wup.shape, bsz * ns, RET_W_CHUNKS),
              _pass_spec(wdn.shape, FFN_DN_PASS_STEPS, RET_W_CHUNKS)]

    def tile(i):
        return jnp.maximum(i - RET_W_CHUNKS, 0)

    def row_spec(width):
        return pl.BlockSpec((TS_RET, width), lambda i: (tile(i), 0))

    return pl.pallas_call(
        functools.partial(_ret_fw_kernel, steps_per_seq=ns),
        name="ret_fw",
        grid=(RET_W_CHUNKS + bsz * ns,),
        in_specs=[
            row_spec(RET_QK), pl.BlockSpec((RET_QK, TS_RET), lambda i: (0, tile(i))),
            row_spec(RET_V), row_spec(RET_V), row_spec(RET_V),
            row_spec(D_MODEL), row_spec(D_MODEL), row_spec(D_MODEL),
            pl.BlockSpec((1, RET_HEADS, RET_DK, RET_DV), lambda i: (tile(i) // ns, 0, 0, 0)),
            _const_spec(dmat.shape), _const_spec(rowf.shape), _const_spec(cdt.shape),
            _chunk_spec(wro.shape, RET_W_CHUNKS, 0), _chunk_spec(wo.shape, RET_W_CHUNKS, 0),
            pl.BlockSpec((1, N_MOD, D_MODEL), lambda i: (tile(i) // ns, 0, 0)),
            pl.BlockSpec((1, D_MODEL), lambda i: (0, 0)),
            *passed,
        ],
        out_specs=(row_spec(D_MODEL), *passed),
        out_shape=(jax.ShapeDtypeStruct((bsz * seq, D_MODEL), F32),
                   jax.ShapeDtypeStruct(wup.shape, BF16), jax.ShapeDtypeStruct(wdn.shape, BF16)),
        scratch_shapes=[pltpu.VMEM((RET_HEADS, RET_DK, RET_DV), F32),
                        pltpu.VMEM((TS_RET, RET_V), BF16),
                        pltpu.VMEM(wro.shape, BF16), pltpu.VMEM(wo.shape, BF16)],
        compiler_params=pltpu.CompilerParams(dimension_semantics=("arbitrary",),
                                             vmem_limit_bytes=VMEM_LIMIT),
    )(q, k, v, sg, ob, yc, sgb, x2, s_fw, dmat, rowf, cdt, wro, wo, modt, gpost, wup, wdn)


def _ffn_kernel(x1_ref, mod_ref, gpre_ref, gpost_ref, wup_ref, cw_ref, cb_ref, wdn_ref,
                out_ref, acc_ref, *, grid_rows):
    tm = grid_rows * FFN_COLS
    x1 = x1_ref[0].reshape(tm, D_MODEL)
    m = mod_ref[0]
    hb = (_rms(x1) * gpre_ref[...] * (1.0 + m[4:5]) + m[3:4]).astype(BF16)

    def conv(u, off, width):
        w = cw_ref[:, off:off + width]
        pad = jnp.zeros((FFN_COLS, width), F32)
        above = jnp.concatenate([pad, u[:tm - FFN_COLS]], axis=0)
        below = jnp.concatenate([u[FFN_COLS:], pad], axis=0)
        return w[0:1] * above + w[1:2] * u + w[2:3] * below + cb_ref[:, off:off + width]

    def up(c0, width):
        return (_dot(hb, wup_ref[:, c0:c0 + width]),
                _dot(hb, wup_ref[:, D_FF + c0:D_FF + c0 + width]))

    ahead = up(*FFN_BLOCKS[0])
    for j, (c0, width) in enumerate(FFN_BLOCKS):
        ug, uv = ahead
        if j + 1 < len(FFN_BLOCKS):
            ahead = up(*FFN_BLOCKS[j + 1])
        g = conv(ug, c0, width)
        val = conv(uv, D_FF + c0, width)
        a = (g * _sigmoid(g) * val).astype(BF16)
        if j == 0:
            acc_ref[...] = _dot(a, wdn_ref[c0:c0 + width, :])
        elif j + 1 < len(FFN_BLOCKS):
            acc_ref[...] += _dot(a, wdn_ref[c0:c0 + width, :])
        else:
            gr = grid_rows // FFN_EPI_PIECES
            for p in range(FFN_EPI_PIECES):
                rows = slice(p * gr * FFN_COLS, (p + 1) * gr * FFN_COLS)
                f = acc_ref[rows, :] + _dot(a[rows, :], wdn_ref[c0:c0 + width, :])
                res = x1[rows, :] + m[5:6] * (_rms(f) * gpost_ref[...])
                out_ref[0, p * gr:(p + 1) * gr] = res.reshape(gr, FFN_COLS, D_MODEL)


def _ffn(x1g, modt, gpre, gpost, wup, cw, cb, wdn):
    bsz, grid_rows, grid_w, _ = x1g.shape
    strips = grid_w // FFN_COLS
    strip = pl.BlockSpec((1, grid_rows, FFN_COLS, D_MODEL), lambda i: (i // strips, 0, i % strips, 0))
    vec = pl.BlockSpec((1, D_MODEL), lambda i: (0, 0))
    return pl.pallas_call(
        functools.partial(_ffn_kernel, grid_rows=grid_rows),
        name="ffn",
        grid=(bsz * strips,),
        in_specs=[
            strip,
            pl.BlockSpec((1, N_MOD, D_MODEL), lambda i: (i // strips, 0, 0)),
            vec, vec,
            _const_spec(wup.shape),
            pl.BlockSpec(cw.shape, lambda i: (0, 0)),
            pl.BlockSpec((1, 2 * D_FF), lambda i: (0, 0)),
            _const_spec(wdn.shape),
        ],
        out_specs=strip,
        out_shape=jax.ShapeDtypeStruct(x1g.shape, F32),
        scratch_shapes=[pltpu.VMEM((grid_rows * FFN_COLS, D_MODEL), F32)],
        compiler_params=pltpu.CompilerParams(dimension_semantics=("arbitrary",),
                                             vmem_limit_bytes=VMEM_LIMIT),
    )(x1g, modt, gpre, gpost, wup, cw, cb, wdn)


def _rotary_tables(n):
    t = np.arange(n)
    row = (t // GRID_W).astype(np.float64)
    col = (t % GRID_W).astype(np.float64)
    inv = ROPE_BASE ** (-np.arange(ROPE_PAIRS, dtype=np.float64) / ROPE_PAIRS)
    ang = np.concatenate([row[:, None] * inv, col[:, None] * inv], axis=-1)
    return jnp.asarray(np.cos(ang), F32), jnp.asarray(np.sin(ang), F32)


def kernel(x, c, ctx, c_ctx, w_ada, b_ada, norm_mix_pre, norm_mix_post, w_in, conv_w, conv_b,
           w_conv_out, ret_decay_fw, ret_decay_bw, w_ret_out, w_o, norm_ffn_pre, norm_ffn_post,
           w_ffn_up, ffn_conv_w, ffn_conv_b, w_ffn_down):
    bsz, seq, d = x.shape
    depth = w_ada.shape[0]
    assert d == D_MODEL and depth == 1 and bsz + 1 <= MOD_ROWS
    assert seq % TS_RET == 0 and seq % TM_IN == 0 and TM_IN % GRID_W == 0 and GRID_W % FFN_COLS == 0
    lc = ctx.shape[1]
    cos, sin = _rotary_tables(seq)
    x2 = x.reshape(bsz * seq, d)


    theta = jnp.concatenate([ret_decay_fw[0], ret_decay_bw[0]]).astype(F32)
    theta = jnp.broadcast_to(theta[:, None], (2 * RET_HEADS, RET_DV))
    modt, s_fw, s_bw, dmat, rowf, rowb, cdt = _prep(c, c_ctx, w_ada[0], b_ada, ctx, norm_mix_pre,
                                                    w_in[0], theta)

    yc, q, k, v, sg, sgb, ob = _inproj(x2, modt, norm_mix_pre, cos, sin, w_in[0], w_conv_out[0],
                                       conv_w[0], conv_b, s_bw, rowb, cdt, bsz, seq)
    x1, wup, wdn = _ret_fw(q, k, v, sg, ob, yc, sgb, x2, s_fw, dmat, rowf, cdt, w_ret_out[0], w_o[0],
                           modt, norm_mix_post, w_ffn_up[0], w_ffn_down[0], bsz, seq)
    out = _ffn(x1.reshape(bsz, seq // GRID_W, GRID_W, d), modt, norm_ffn_pre, norm_ffn_post,
               wup, ffn_conv_w[0], ffn_conv_b, wdn)
    return out.reshape(bsz, seq, d)
```
